```python
import math
import jax, jax.numpy as jnp
from jax import lax
import numpy as np

D_MODEL = 4096
BATCH = 2
SEQ = 4096
DEPTH = 2

HEAD_DIM = 128
RMS_EPS = 1e-6
D_RNN = D_MODEL // 2
RNN_BLOCKS = 16
RNN_BLOCK = D_RNN // RNN_BLOCKS
CONV_WIDTH = 4
LRU_C = 8.0
FOX_HEADS = (D_MODEL // 2) // HEAD_DIM
FOX_WIDTH = FOX_HEADS * HEAD_DIM
Q_BLOCK = 128
DSW_PATTERNS = ((128, 1), (512, 4), (2048, 16))
N_DSW_GROUPS = len(DSW_PATTERNS)
DSW_HEADS = (D_MODEL // 2) // HEAD_DIM
DSW_WIDTH = DSW_HEADS * HEAD_DIM
DSW_BLOCK = 128
N_BUCKETS = 32
MAX_DISTANCE = 2048
N_GROUPS = 4
EXPERTS_PER_GROUP = 8
N_EXPERTS = N_GROUPS * EXPERTS_PER_GROUP
TOP_K = 2
D_EXPERT = 3 * D_MODEL // 16
MOE_BLOCK = 128
N_EVEN = (DEPTH + 1) // 2
N_ODD = DEPTH // 2
AB_IN = 2 * D_RNN + 3 * FOX_WIDTH + FOX_HEADS
C_IN = N_DSW_GROUPS * 3 * DSW_WIDTH

kernel_name = 'hybrid_rglru_fox_dilated_hmoe'


def rms_norm(x, g):
    xf = x.astype(jnp.float32)
    y = xf * lax.rsqrt(jnp.mean(xf * xf, axis=-1, keepdims=True) + RMS_EPS)
    return (y * g.astype(jnp.float32)).astype(x.dtype)


def causal_depthwise_conv(x, w, b):
    c = x.shape[-1]
    y = lax.conv_general_dilated(x, w[:, None, :].astype(x.dtype), window_strides=(1,),
                                 padding=((CONV_WIDTH - 1, 0),),
                                 dimension_numbers=('NWC', 'WIO', 'NWC'),
                                 feature_group_count=c)
    return y + b


def rg_lru(x, wa, ba, wi, bi, lam):
    bsz, t, c = x.shape
    xb = x.reshape(bsz, t, RNN_BLOCKS, RNN_BLOCK)
    r = jax.nn.sigmoid((jnp.einsum('btni,nij->btnj', xb, wa).reshape(bsz, t, c) + ba).astype(jnp.float32))
    i = jax.nn.sigmoid((jnp.einsum('btni,nij->btnj', xb, wi).reshape(bsz, t, c) + bi).astype(jnp.float32))
    log_a = -LRU_C * jax.nn.softplus(-lam.astype(jnp.float32)) * r
    a = jnp.exp(log_a)
    u = jnp.sqrt(-jnp.expm1(2.0 * log_a)) * (i * x.astype(jnp.float32))

    def combine(left, right):
        a1, b1 = left
        a2, b2 = right
        return a1 * a2, a2 * b1 + b2

    _, h = lax.associative_scan(combine, (a, u), axis=1)
    return h.astype(x.dtype)


def forgetting_attention(q, k, v, log_f):
    bsz, t, nh, dh = q.shape
    nblk = t // Q_BLOCK
    cum = jnp.cumsum(log_f, axis=1)
    qb = q.reshape(bsz, nblk, Q_BLOCK, nh, dh).transpose(1, 0, 3, 2, 4)
    cb = cum.reshape(bsz, nblk, Q_BLOCK, nh).transpose(1, 0, 3, 2)
    kt = k.transpose(0, 2, 1, 3)
    vt = v.transpose(0, 2, 1, 3)
    ct = cum.transpose(0, 2, 1)
    kpos = jnp.arange(t)
    scale = 1.0 / math.sqrt(dh)

    def block(args):
        qi, ci, bi = args
        s = jnp.einsum('bhqd,bhkd->bhqk', qi, kt).astype(jnp.float32) * scale
        s = s + (ci[..., None] - ct[:, :, None, :])
        qpos = bi * Q_BLOCK + jnp.arange(Q_BLOCK)
        s = jnp.where(kpos[None, :] <= qpos[:, None], s, -jnp.inf)
        p = jax.nn.softmax(s, axis=-1)
        return jnp.einsum('bhqk,bhkd->bhqd', p.astype(vt.dtype), vt)

    o = lax.map(block, (qb, cb, jnp.arange(nblk)))
    return o.transpose(1, 0, 3, 2, 4).reshape(bsz, t, nh * dh)


def t5_bucket(dist):
    max_exact = N_BUCKETS // 2
    large = max_exact + (jnp.log(jnp.maximum(dist, 1).astype(jnp.float32) / max_exact)
                         / math.log(MAX_DISTANCE / max_exact) * (N_BUCKETS - max_exact)).astype(jnp.int32)
    return jnp.where(dist < max_exact, dist, jnp.minimum(large, N_BUCKETS - 1))


def dilated_group(q, k, v, table, window, dil):
    bsz, t, nh, dh = q.shape
    L = t // dil
    nblk = -(-L // DSW_BLOCK)
    lp = nblk * DSW_BLOCK
    span = window // dil

    def to_blocks(a):
        a = a.reshape(bsz, L, dil, nh, dh).transpose(0, 2, 3, 1, 4)
        a = jnp.pad(a, ((0, 0), (0, 0), (0, 0), (0, lp - L), (0, 0)))
        return a.reshape(bsz, dil, nh, nblk, DSW_BLOCK, dh)

    def with_prev(a):
        prev = jnp.concatenate([jnp.zeros_like(a[:, :, :, :1]), a[:, :, :, :-1]], axis=3)
        return jnp.concatenate([prev, a], axis=4)

    qb = to_blocks(q)
    kk = with_prev(to_blocks(k))
    vv = with_prev(to_blocks(v))
    qi = jnp.arange(DSW_BLOCK)[:, None]
    kj = jnp.arange(2 * DSW_BLOCK)[None, :]
    delta = qi + DSW_BLOCK - kj
    band = (delta >= 0) & (delta <= span)
    valid = band[None] & ((jnp.arange(nblk)[:, None, None] > 0) | (kj >= DSW_BLOCK)[None])
    bias = table[t5_bucket(jnp.maximum(delta, 0) * dil)].transpose(2, 0, 1).astype(jnp.float32)
    s = jnp.einsum('bzhnqd,bzhnkd->bzhnqk', qb, kk).astype(jnp.float32) / math.sqrt(dh)
    s = jnp.where(valid, s + bias[None, None, :, None], -jnp.inf)
    m = jnp.max(s, axis=-1, keepdims=True)
    e = jnp.exp(s - m)
    den = jnp.sum(e, axis=-1)
    o = jnp.einsum('bzhnqk,bzhnkd->bzhnqd', e, vv.astype(jnp.float32)) / den[..., None]
    lse = m[..., 0] + jnp.log(den)
    o = o.reshape(bsz, dil, nh, lp, dh)[:, :, :, :L].transpose(0, 3, 1, 2, 4).reshape(bsz, t, nh, dh)
    lse = lse.reshape(bsz, dil, nh, lp)[..., :L].transpose(0, 3, 1, 2).reshape(bsz, t, nh)
    return o, lse


def ab_mixer(h, w_in, conv_w, conv_b, wa, ba, wi, bi, lam, bf, qn, kn, w_out):
    bsz, t, _ = h.shape
    z = h @ w_in
    gate_a, xa, q, k, v, fz = jnp.split(z, [D_RNN, 2 * D_RNN, 2 * D_RNN + FOX_WIDTH,
                                            2 * D_RNN + 2 * FOX_WIDTH, 2 * D_RNN + 3 * FOX_WIDTH], axis=-1)
    ya = rg_lru(causal_depthwise_conv(xa, conv_w, conv_b), wa, ba, wi, bi, lam) * jax.nn.gelu(gate_a)
    q = rms_norm(q.reshape(bsz, t, FOX_HEADS, HEAD_DIM), qn)
    k = rms_norm(k.reshape(bsz, t, FOX_HEADS, HEAD_DIM), kn)
    v = v.reshape(bsz, t, FOX_HEADS, HEAD_DIM)
    log_f = jax.nn.log_sigmoid(fz.astype(jnp.float32) + bf.astype(jnp.float32))
    yb = forgetting_attention(q, k, v, log_f)
    return jnp.concatenate([ya, yb.astype(ya.dtype)], axis=-1) @ w_out


def dilated_mixer(h, w_in, qn, kn, table, w_out):
    bsz, t, _ = h.shape
    z = (h @ w_in).reshape(bsz, t, N_DSW_GROUPS, 3, DSW_HEADS, HEAD_DIM)
    outs = []
    lses = []
    for g, (window, dil) in enumerate(DSW_PATTERNS):
        q = rms_norm(z[:, :, g, 0], qn[g])
        k = rms_norm(z[:, :, g, 1], kn[g])
        o, lse = dilated_group(q, k, z[:, :, g, 2], table[:, g * DSW_HEADS:(g + 1) * DSW_HEADS], window, dil)
        outs.append(o)
        lses.append(lse)
    alpha = jax.nn.softmax(jnp.stack(lses), axis=0)
    o = jnp.einsum('gbth,gbthd->bthd', alpha, jnp.stack(outs))
    return o.reshape(bsz, t, DSW_WIDTH).astype(h.dtype) @ w_out


def hier_moe(x, w_group, b_group, w_router, b_router, w_gate, w_up, w_down):
    bsz, t, d = x.shape
    n_tok = bsz * t
    xf = x.reshape(n_tok, d)
    glog = (xf @ w_group).astype(jnp.float32) + b_group.astype(jnp.float32)
    gsel = jnp.argmax(glog, axis=-1)
    pg = jnp.take_along_axis(jax.nn.softmax(glog, axis=-1), gsel[:, None], axis=1)[:, 0]
    elog = ((xf @ w_router).astype(jnp.float32) + b_router.astype(jnp.float32)).reshape(n_tok, N_GROUPS, EXPERTS_PER_GROUP)
    elog = jnp.take_along_axis(elog, gsel[:, None, None], axis=1)[:, 0]
    top_v, top_j = lax.top_k(elog, TOP_K)
    weights = pg[:, None] * jax.nn.softmax(top_v, axis=-1)
    eid = gsel[:, None] * EXPERTS_PER_GROUP + top_j
    n_as = n_tok * TOP_K
    flat_e = eid.reshape(n_as)
    flat_tok = jnp.repeat(jnp.arange(n_tok), TOP_K)
    flat_w = weights.reshape(n_as)
    order = jnp.argsort(flat_e)
    se, stok, sw = flat_e[order], flat_tok[order], flat_w[order]
    counts = jax.ops.segment_sum(jnp.ones_like(flat_e), flat_e, num_segments=N_EXPERTS)
    start = jnp.cumsum(counts) - counts
    pcounts = (counts + MOE_BLOCK - 1) // MOE_BLOCK * MOE_BLOCK
    pend = jnp.cumsum(pcounts)
    pstart = pend - pcounts
    dest = pstart[se] + jnp.arange(n_as) - start[se]
    nb = -(-n_as // MOE_BLOCK) + N_EXPERTS
    p_rows = nb * MOE_BLOCK
    buf_tok = jnp.full((p_rows,), n_tok, jnp.int32).at[dest].set(stok)
    buf_w = jnp.zeros((p_rows,), jnp.float32).at[dest].set(sw)
    blk_e = jnp.minimum(jnp.searchsorted(pend, jnp.arange(nb) * MOE_BLOCK, side='right'), N_EXPERTS - 1)
    xpad = jnp.concatenate([xf, jnp.zeros((1, d), xf.dtype)], axis=0)

    def expert_block(args):
        tok, e = args
        xb = xpad[tok]
        return (jax.nn.silu(xb @ w_gate[e]) * (xb @ w_up[e])) @ w_down[e]

    yb = lax.map(expert_block, (buf_tok.reshape(nb, MOE_BLOCK), blk_e))
    y = jax.ops.segment_sum(yb.reshape(p_rows, d) * buf_w[:, None].astype(yb.dtype), buf_tok,
                            num_segments=n_tok + 1)[:n_tok]
    return y.reshape(bsz, t, d)


def setup_inputs(seed: int = 0) -> dict:
    key = jax.random.key(seed)
    ks = jax.random.split(key, 32)

    def nrm(k, shape, scale):
        return scale * jax.random.normal(k, shape, jnp.float32)

    u = jax.random.uniform(ks[10], (N_EVEN, D_RNN), jnp.float32, 0.9, 0.999)
    a0 = u ** (1.0 / LRU_C)
    return {
        'x': nrm(ks[0], (BATCH, SEQ, D_MODEL), 1.0),
        'mix_norm': 1.0 + nrm(ks[1], (DEPTH, D_MODEL), 0.02),
        'ffn_norm': 1.0 + nrm(ks[2], (DEPTH, D_MODEL), 0.02),
        'w_in_ab': nrm(ks[3], (N_EVEN, D_MODEL, AB_IN), D_MODEL ** -0.5),
        'conv_w': nrm(ks[4], (N_EVEN, CONV_WIDTH, D_RNN), CONV_WIDTH ** -0.5),
        'conv_b': nrm(ks[5], (N_EVEN, D_RNN), 0.01),
        'lru_wa': nrm(ks[6], (N_EVEN, RNN_BLOCKS, RNN_BLOCK, RNN_BLOCK), RNN_BLOCK ** -0.5),
        'lru_ba': nrm(ks[7], (N_EVEN, D_RNN), 0.01),
        'lru_wi': nrm(ks[8], (N_EVEN, RNN_BLOCKS, RNN_BLOCK, RNN_BLOCK), RNN_BLOCK ** -0.5),
        'lru_bi': nrm(ks[9], (N_EVEN, D_RNN), 0.01),
        'lru_lambda': jnp.log(a0) - jnp.log1p(-a0),
        'fox_bf': 2.0 + nrm(ks[11], (N_EVEN, FOX_HEADS), 0.1),
        'fox_qnorm': 1.0 + nrm(ks[12], (N_EVEN, HEAD_DIM), 0.02),
        'fox_knorm': 1.0 + nrm(ks[13], (N_EVEN, HEAD_DIM), 0.02),
        'w_out_ab': nrm(ks[14], (N_EVEN, D_RNN + FOX_WIDTH, D_MODEL), (D_RNN + FOX_WIDTH) ** -0.5),
        'w_in_c': nrm(ks[15], (N_ODD, D_MODEL, C_IN), D_MODEL ** -0.5),
        'dsw_qnorm': 1.0 + nrm(ks[16], (N_ODD, N_DSW_GROUPS, HEAD_DIM), 0.02),
        'dsw_knorm': 1.0 + nrm(ks[17], (N_ODD, N_DSW_GROUPS, HEAD_DIM), 0.02),
        'w_out_c': nrm(ks[18], (N_ODD, DSW_WIDTH, D_MODEL), DSW_WIDTH ** -0.5),
        'rel_bias': nrm(ks[19], (N_BUCKETS, N_DSW_GROUPS * DSW_HEADS), 0.2),
        'moe_w_group': nrm(ks[20], (DEPTH, D_MODEL, N_GROUPS), D_MODEL ** -0.5),
        'moe_b_group': nrm(ks[21], (DEPTH, N_GROUPS), 0.01),
        'moe_w_router': nrm(ks[22], (DEPTH, D_MODEL, N_EXPERTS), D_MODEL ** -0.5),
        'moe_b_router': nrm(ks[23], (DEPTH, N_EXPERTS), 0.01),
        'moe_w_gate': nrm(ks[24], (DEPTH, N_EXPERTS, D_MODEL, D_EXPERT), D_MODEL ** -0.5),
        'moe_w_up': nrm(ks[25], (DEPTH, N_EXPERTS, D_MODEL, D_EXPERT), D_MODEL ** -0.5),
        'moe_w_down': nrm(ks[26], (DEPTH, N_EXPERTS, D_EXPERT, D_MODEL), D_EXPERT ** -0.5),
    }


def reference(x, mix_norm, ffn_norm, w_in_ab, conv_w, conv_b, lru_wa, lru_ba, lru_wi, lru_bi,
              lru_lambda, fox_bf, fox_qnorm, fox_knorm, w_out_ab, w_in_c, dsw_qnorm, dsw_knorm,
              w_out_c, rel_bias, moe_w_group, moe_b_group, moe_w_router, moe_b_router,
              moe_w_gate, moe_w_up, moe_w_down):
    for layer in range(DEPTH):
        h = rms_norm(x, mix_norm[layer])
        i = layer // 2
        if layer % 2 == 0:
            y = ab_mixer(h, w_in_ab[i], conv_w[i], conv_b[i], lru_wa[i], lru_ba[i], lru_wi[i],
                         lru_bi[i], lru_lambda[i], fox_bf[i], fox_qnorm[i], fox_knorm[i], w_out_ab[i])
        else:
            y = dilated_mixer(h, w_in_c[i], dsw_qnorm[i], dsw_knorm[i], rel_bias, w_out_c[i])
        x = x + y.astype(x.dtype)
        h = rms_norm(x, ffn_norm[layer])
        x = x + hier_moe(h, moe_w_group[layer], moe_b_group[layer], moe_w_router[layer],
                         moe_b_router[layer], moe_w_gate[layer], moe_w_up[layer],
                         moe_w_down[layer]).astype(x.dtype)
    return x
```

```python
import functools
import math

import jax
import jax.numpy as jnp
from jax import lax
from jax.experimental import pallas as pl
from jax.experimental.pallas import tpu as pltpu

F32 = jnp.float32
BF16 = jnp.bfloat16

LANES = 128
SUBLANES = 8
VMEM_LIMIT = 56 * 1024 * 1024

HEAD_DIM = 128
RMS_EPS = 1e-6
RNN_BLOCKS = 16
CONV_WIDTH = 4
LRU_C = 8.0
DSW_PATTERNS = ((128, 1), (512, 4), (2048, 16))
DSW_BLOCK = 128
N_BUCKETS = 32
MAX_DISTANCE = 2048
N_GROUPS = 4
EXPERTS_PER_GROUP = 8
N_EXPERTS = N_GROUPS * EXPERTS_PER_GROUP
TOP_K = 2

MOE_TM = 256


def _params(*sem):
    return pltpu.CompilerParams(dimension_semantics=sem, vmem_limit_bytes=VMEM_LIMIT)


def _rmsnorm_kernel(x_ref, g_ref, o_ref):
    x = x_ref[...]
    ms = jnp.mean(x * x, axis=-1, keepdims=True)
    o_ref[...] = (x * lax.rsqrt(ms + RMS_EPS) * g_ref[...]).astype(o_ref.dtype)


def _rmsnorm(x, g, tm=256):
    m, d = x.shape
    return pl.pallas_call(
        _rmsnorm_kernel,
        grid=(m // tm,),
        in_specs=[pl.BlockSpec((tm, d), lambda i: (i, 0)),
                  pl.BlockSpec((1, d), lambda i: (0, 0))],
        out_specs=pl.BlockSpec((tm, d), lambda i: (i, 0)),
        out_shape=jax.ShapeDtypeStruct((m, d), BF16),
        compiler_params=_params("parallel"),
        name="rmsnorm",
    )(x, g.reshape(1, d))


def _proj_f32_kernel(a_ref, w_ref, o_ref):
    o_ref[...] = jnp.dot(a_ref[...], w_ref[...].astype(BF16), preferred_element_type=F32)


def _proj_f32(a, w, li, col0, ncols, tm=1024, tn=512):
    m, k = a.shape
    jb = col0 // tn
    return pl.pallas_call(
        _proj_f32_kernel,
        grid=(m // tm, ncols // tn),
        in_specs=[pl.BlockSpec((tm, k), lambda i, j: (i, 0)),
                  pl.BlockSpec((None, k, tn), lambda i, j: (li, 0, jb + j))],
        out_specs=pl.BlockSpec((tm, tn), lambda i, j: (i, j)),
        out_shape=jax.ShapeDtypeStruct((m, ncols), F32),
        compiler_params=_params("parallel", "arbitrary"),
        name="proj_f32",
    )(a, w)


def _proj_res_kernel(a_ref, w_ref, r_ref, o_ref):
    o_ref[...] = r_ref[...] + jnp.dot(a_ref[...], w_ref[...].astype(BF16),
                                      preferred_element_type=F32)


def _proj_res(a, w, li, res, tm=1024, tn=512):
    m, k = a.shape
    n = w.shape[2]
    return pl.pallas_call(
        _proj_res_kernel,
        grid=(m // tm, n // tn),
        in_specs=[pl.BlockSpec((tm, k), lambda i, j: (i, 0)),
                  pl.BlockSpec((None, k, tn), lambda i, j: (li, 0, j)),
                  pl.BlockSpec((tm, tn), lambda i, j: (i, j))],
        out_specs=pl.BlockSpec((tm, tn), lambda i, j: (i, j)),
        out_shape=jax.ShapeDtypeStruct((m, n), F32),
        compiler_params=_params("parallel", "arbitrary"),
        name="proj_res",
    )(a, w, res)


def _proj_heads_kernel(a_ref, w_ref, nw_ref, o_ref, y_scr, *, dil, n_norm_blocks, hpb, tm):
    j = pl.program_id(2)
    acc = jnp.dot(a_ref[...], w_ref[...].astype(BF16), preferred_element_type=F32)
    for hh in range(hpb):
        y_scr[hh] = acc[:, hh * HEAD_DIM:(hh + 1) * HEAD_DIM]

    @pl.when(j < n_norm_blocks)
    def _():
        for hh in range(hpb):
            y = y_scr[hh]
            ms = jnp.mean(y * y, axis=-1, keepdims=True)
            y_scr[hh] = y * lax.rsqrt(ms + RMS_EPS) * nw_ref[:, hh * HEAD_DIM:(hh + 1) * HEAD_DIM]

    rows = tm // dil
    for z in range(dil):
        for hh in range(hpb):
            if dil == 1:
                o_ref[0, z, hh] = y_scr[hh].astype(o_ref.dtype)
            else:
                o_ref[0, z, hh] = y_scr[hh, pl.ds(z, rows, stride=dil), :].astype(o_ref.dtype)


def _proj_heads(a, w, li, col0, ncols, normw, n_norm_cols, bsz, dil, tm=1024, tn=512):
    m, k = a.shape
    t = m // bsz
    nt = t // tm
    hpb = tn // HEAD_DIM
    jb = col0 // tn
    kern = functools.partial(_proj_heads_kernel, dil=dil, n_norm_blocks=n_norm_cols // tn,
                             hpb=hpb, tm=tm)
    return pl.pallas_call(
        kern,
        grid=(bsz, nt, ncols // tn),
        in_specs=[pl.BlockSpec((tm, k), lambda b, i, j: (b * nt + i, 0)),
                  pl.BlockSpec((None, k, tn), lambda b, i, j: (li, 0, jb + j)),
                  pl.BlockSpec((1, tn), lambda b, i, j: (0, j))],
        out_specs=pl.BlockSpec((1, dil, hpb, tm // dil, HEAD_DIM),
                               lambda b, i, j: (b, 0, j, i, 0)),
        out_shape=jax.ShapeDtypeStruct((bsz, dil, ncols // HEAD_DIM, t // dil, HEAD_DIM), BF16),
        scratch_shapes=[pltpu.VMEM((hpb, tm, HEAD_DIM), F32)],
        compiler_params=_params("parallel", "parallel", "arbitrary"),
        name=f"proj_heads_d{dil}",
    )(a, w, normw)


def _forget_cum_kernel(a_ref, w_ref, b_ref, o_ref, carry_ref, *, tm):
    i = pl.program_id(1)

    @pl.when(i == 0)
    def _():
        carry_ref[...] = jnp.zeros_like(carry_ref)

    fz = jnp.dot(a_ref[...], w_ref[...].astype(BF16), preferred_element_type=F32) + b_ref[...]
    c = jax.nn.log_sigmoid(fz)
    row = lax.broadcasted_iota(jnp.int32, c.shape, 0)
    s = 1
    while s < tm:
        c = c + jnp.where(row >= s, pltpu.roll(c, s, 0), 0.0)
        s *= 2
    c = c + carry_ref[0:1, :]
    o_ref[0] = c
    carry_ref[...] = jnp.broadcast_to(c[tm - 1:tm, :], carry_ref.shape)


def _forget_cum(a, w_f, b_f, bsz, tm=512):
    m, k = a.shape
    t = m // bsz
    nt = t // tm
    return pl.pallas_call(
        functools.partial(_forget_cum_kernel, tm=tm),
        grid=(bsz, nt),
        in_specs=[pl.BlockSpec((tm, k), lambda b, i: (b * nt + i, 0)),
                  pl.BlockSpec((k, LANES), lambda b, i: (0, 0)),
                  pl.BlockSpec((1, LANES), lambda b, i: (0, 0))],
        out_specs=pl.BlockSpec((1, tm, LANES), lambda b, i: (b, i, 0)),
        out_shape=jax.ShapeDtypeStruct((bsz, t, LANES), F32),
        scratch_shapes=[pltpu.VMEM((SUBLANES, LANES), F32)],
        compiler_params=_params("parallel", "arbitrary"),
        name="forget_cum",
    )(a, w_f, b_f)


def _gelu_tanh(x):
    return 0.5 * x * (1.0 + jnp.tanh(math.sqrt(2.0 / math.pi) * (x + 0.044715 * (x * x * x))))


def _rglru_kernel(g_ref, x_ref, cw_ref, cb_ref, wa_ref, ba_ref, wi_ref, bi_ref, lam_ref,
                  o_ref, xs_scr, a_scr, u_scr, h_scr, carry_scr, *, tc, nblk):
    ti = pl.program_id(2)
    halo = SUBLANES

    @pl.when(ti == 0)
    def _():
        xs_scr[0:halo, :] = jnp.zeros((halo, xs_scr.shape[1]), F32)
        carry_scr[...] = jnp.zeros_like(carry_scr)

    xs_scr[halo:halo + tc, :] = x_ref[0]
    xc = cb_ref[...] + cw_ref[CONV_WIDTH - 1:CONV_WIDTH, :] * xs_scr[halo:halo + tc, :]
    for j in range(CONV_WIDTH - 1):
        sh = CONV_WIDTH - 1 - j
        xc = xc + cw_ref[j:j + 1, :] * xs_scr[halo - sh:halo - sh + tc, :]
    xs_scr[0:halo, :] = xs_scr[tc:tc + halo, :]

    neg_sp = -LRU_C * jax.nn.softplus(-lam_ref[...])
    for n in range(nblk):
        cs = slice(n * HEAD_DIM, (n + 1) * HEAD_DIM)
        xb = xc[:, cs]
        xbb = xb.astype(BF16)
        r = jax.nn.sigmoid(jnp.dot(xbb, wa_ref[n].astype(BF16), preferred_element_type=F32)
                           + ba_ref[:, cs])
        ig = jax.nn.sigmoid(jnp.dot(xbb, wi_ref[n].astype(BF16), preferred_element_type=F32)
                            + bi_ref[:, cs])
        log_a = neg_sp[:, cs] * r
        a = jnp.exp(log_a)
        a_scr[:, cs] = a
        u_scr[:, cs] = jnp.sqrt(-jnp.tanh(log_a) * (a * a + 1.0)) * (ig * xb)

    row = lax.broadcasted_iota(jnp.int32, (SUBLANES, a_scr.shape[1]), 0)

    def body(g, carry):
        r0 = pl.multiple_of(g * SUBLANES, SUBLANES)
        a = a_scr[pl.ds(r0, SUBLANES), :]
        u = u_scr[pl.ds(r0, SUBLANES), :]
        s = 1
        while s < SUBLANES:
            keep = row >= s
            u = jnp.where(keep, a * pltpu.roll(u, s, 0) + u, u)
            a = jnp.where(keep, a * pltpu.roll(a, s, 0), a)
            s *= 2
        h = a * carry + u
        h_scr[pl.ds(r0, SUBLANES), :] = h
        return jnp.broadcast_to(h[SUBLANES - 1:SUBLANES, :], h.shape)

    carry = lax.fori_loop(0, tc // SUBLANES, body, carry_scr[...])
    carry_scr[...] = carry
    o_ref[0] = (h_scr[...] * _gelu_tanh(g_ref[0])).astype(o_ref.dtype)


def _rglru(zga, bsz, conv_w, conv_b, wa, ba, wi, bi, lam, tc=256, cw=512):
    _, t, c2 = zga.shape
    c = c2 // 2
    nblk = cw // HEAD_DIM
    ncb = c // cw
    vec = lambda v: v.reshape(1, c)
    vspec = pl.BlockSpec((1, cw), lambda b, j, i: (0, j))
    kern = functools.partial(_rglru_kernel, tc=tc, nblk=nblk)
    return pl.pallas_call(
        kern,
        grid=(bsz, ncb, t // tc),
        in_specs=[pl.BlockSpec((1, tc, cw), lambda b, j, i: (b, i, j)),
                  pl.BlockSpec((1, tc, cw), lambda b, j, i: (b, i, ncb + j)),
                  pl.BlockSpec((CONV_WIDTH, cw), lambda b, j, i: (0, j)),
                  vspec,
                  pl.BlockSpec((nblk, HEAD_DIM, HEAD_DIM), lambda b, j, i: (j, 0, 0)),
                  vspec,
                  pl.BlockSpec((nblk, HEAD_DIM, HEAD_DIM), lambda b, j, i: (j, 0, 0)),
                  vspec, vspec],
        out_specs=pl.BlockSpec((1, tc, cw), lambda b, j, i: (b, i, j)),
        out_shape=jax.ShapeDtypeStruct((bsz, t, c), BF16),
        scratch_shapes=[pltpu.VMEM((tc + SUBLANES, cw), F32),
                        pltpu.VMEM((tc, cw), F32),
                        pltpu.VMEM((tc, cw), F32),
                        pltpu.VMEM((tc, cw), F32),
                        pltpu.VMEM((SUBLANES, cw), F32)],
        compiler_params=_params("parallel", "parallel", "arbitrary"),
        name="rglru",
    )(zga, zga, conv_w, vec(conv_b), wa, vec(ba), wi, vec(bi), vec(lam))


def _fox_kernel(q_ref, k_ref, v_ref, cq_ref, ck_ref, o_ref, m_scr, l_scr, acc_scr, *, tq, tk):
    qi = pl.program_id(2)
    ki = pl.program_id(3)

    @pl.when(ki == 0)
    def _():
        m_scr[...] = jnp.full_like(m_scr, -jnp.inf)
        l_scr[...] = jnp.zeros_like(l_scr)
        acc_scr[...] = jnp.zeros_like(acc_scr)

    def step(masked):
        s = lax.dot_general(q_ref[0, 0, 0], k_ref[0, 0, 0], (((1,), (1,)), ((), ())),
                            preferred_element_type=F32) * (1.0 / math.sqrt(HEAD_DIM))
        s = s + (cq_ref[0, 0] - ck_ref[0, 0])
        if masked:
            qpos = lax.broadcasted_iota(jnp.int32, (tq, tk), 0)
            kpos = lax.broadcasted_iota(jnp.int32, (tq, tk), 1)
            s = jnp.where(kpos <= qpos, s, -jnp.inf)
        m_prev = m_scr[...]
        m_new = jnp.maximum(m_prev, jnp.max(s, axis=-1, keepdims=True))
        alpha = jnp.exp(m_prev - m_new)
        p = jnp.exp(s - m_new)
        l_scr[...] = alpha * l_scr[...] + jnp.sum(p, axis=-1, keepdims=True)
        acc_scr[...] = alpha * acc_scr[...] + jnp.dot(p.astype(BF16), v_ref[0, 0, 0],
                                                      preferred_element_type=F32)
        m_scr[...] = m_new

    @pl.when(ki < qi)
    def _():
        step(False)

    @pl.when(ki == qi)
    def _():
        step(True)
        o_ref[0] = (acc_scr[...] / l_scr[...]).astype(o_ref.dtype)


def _fox_attention(qkv, cum_col, cum_row, tq=512):
    bsz, _, h3, t, dh = qkv.shape
    nh = h3 // 3
    tk = tq
    nq = t // tq
    kern = functools.partial(_fox_kernel, tq=tq, tk=tk)
    return pl.pallas_call(
        kern,
        grid=(bsz, nh, nq, nq),
        in_specs=[pl.BlockSpec((1, 1, 1, tq, dh), lambda b, h, i, j: (b, 0, h, i, 0)),
                  pl.BlockSpec((1, 1, 1, tk, dh), lambda b, h, i, j: (b, 0, nh + h, jnp.minimum(i, j), 0)),
                  pl.BlockSpec((1, 1, 1, tk, dh), lambda b, h, i, j: (b, 0, 2 * nh + h, jnp.minimum(i, j), 0)),
                  pl.BlockSpec((1, 1, tq, 1), lambda b, h, i, j: (b, h, i, 0)),
                  pl.BlockSpec((1, 1, 1, tk), lambda b, h, i, j: (b, h, 0, jnp.minimum(i, j)))],
        out_specs=pl.BlockSpec((1, tq, dh), lambda b, h, i, j: (b, i, h)),
        out_shape=jax.ShapeDtypeStruct((bsz, t, nh * dh), BF16),
        scratch_shapes=[pltpu.VMEM((tq, 1), F32), pltpu.VMEM((tq, 1), F32),
                        pltpu.VMEM((tq, dh), F32)],
        compiler_params=_params("parallel", "parallel", "parallel", "arbitrary"),
        name="fox_attention",
    )(qkv, qkv, qkv, cum_col, cum_row)


def _dsw_kernel(q_ref, kc_ref, kp_ref, vc_ref, vp_ref, bias_ref, o_ref, lse_ref, *, lc, nh):
    c = pl.program_id(2)
    blk = DSW_BLOCK
    nb = lc // blk
    lane = lax.broadcasted_iota(jnp.int32, (blk, LANES), 1)
    col = lax.broadcasted_iota(jnp.int32, (blk, 2 * blk), 1)
    scale = 1.0 / math.sqrt(HEAD_DIM)

    def head(h, _):
        bias = bias_ref[h]
        for n in range(nb):
            rs = slice(n * blk, (n + 1) * blk)
            q = q_ref[0, 0, h, rs, :]
            if n == 0:
                kprev = kp_ref[0, 0, h]
                vprev = vp_ref[0, 0, h]
            else:
                ps = slice((n - 1) * blk, n * blk)
                kprev = kc_ref[0, 0, h, ps, :]
                vprev = vc_ref[0, 0, h, ps, :]
            kk = jnp.concatenate([kprev, kc_ref[0, 0, h, rs, :]], axis=0)
            vv = jnp.concatenate([vprev, vc_ref[0, 0, h, rs, :]], axis=0)
            s = lax.dot_general(q, kk, (((1,), (1,)), ((), ())),
                                preferred_element_type=F32) * scale + bias
            if n == 0:
                s = jnp.where((c > 0) | (col >= blk), s, -jnp.inf)
            m = jnp.max(s, axis=-1, keepdims=True)
            e = jnp.exp(s - m)
            den = jnp.sum(e, axis=-1, keepdims=True)
            o = jnp.dot(e.astype(BF16), vv, preferred_element_type=F32) / den
            o_ref[0, 0, rs, pl.ds(pl.multiple_of(h * HEAD_DIM, HEAD_DIM), HEAD_DIM)] = (
                o.astype(o_ref.dtype))
            lse = m + jnp.log(den)
            lse_ref[0, 0, rs, :] = jnp.where(lane == h, lse, lse_ref[0, 0, rs, :])
        return 0

    lax.fori_loop(0, nh, head, 0)


def _dsw_group(qkv, bias, lc):
    bsz, dil, h3, l, dh = qkv.shape
    nh = h3 // 3
    nc = l // lc
    bpc = lc // DSW_BLOCK
    kern = functools.partial(_dsw_kernel, lc=lc, nh=nh)
    cur = lambda off: pl.BlockSpec((1, 1, nh, lc, dh), lambda b, z, c: (b, z, off, c, 0))
    prev = lambda off: pl.BlockSpec((1, 1, nh, DSW_BLOCK, dh),
                                    lambda b, z, c: (b, z, off, jnp.maximum(c * bpc - 1, 0), 0))
    return pl.pallas_call(
        kern,
        grid=(bsz, dil, nc),
        in_specs=[cur(0), cur(1), prev(1), cur(2), prev(2),
                  pl.BlockSpec((nh, DSW_BLOCK, 2 * DSW_BLOCK), lambda b, z, c: (0, 0, 0))],
        out_specs=[pl.BlockSpec((1, 1, lc, nh * dh), lambda b, z, c: (b, z, c, 0)),
                   pl.BlockSpec((1, 1, lc, LANES), lambda b, z, c: (b, z, c, 0))],
        out_shape=[jax.ShapeDtypeStruct((bsz, dil, l, nh * dh), BF16),
                   jax.ShapeDtypeStruct((bsz, dil, l, LANES), F32)],
        compiler_params=_params("parallel", "parallel", "arbitrary"),
        name=f"dsw_attention_d{dil}",
    )(qkv, qkv, qkv, qkv, qkv, bias)


def _dsw_merge_kernel(*refs, dils, tm, nh):
    ng = len(dils)
    o_refs = refs[:ng]
    l_refs = refs[ng:2 * ng]
    out_ref = refs[2 * ng]
    o_scr = refs[2 * ng + 1:2 * ng + 1 + ng]
    l_scr = refs[2 * ng + 1 + ng:]
    for g, dil in enumerate(dils):
        rows = tm // dil
        for z in range(dil):
            rsel = slice(None) if dil == 1 else pl.ds(z, rows, stride=dil)
            l_scr[g][rsel, :] = l_refs[g][0, z]
            for h in range(nh):
                o_scr[g][h, rsel, :] = o_refs[g][0, z, :, h * HEAD_DIM:(h + 1) * HEAD_DIM].astype(F32)
    lses = [l_scr[g][...] for g in range(ng)]
    m = functools.reduce(jnp.maximum, lses)
    ws = [jnp.exp(l - m) for l in lses]
    inv = 1.0 / functools.reduce(lambda a, b: a + b, ws)
    for h in range(nh):
        acc = None
        for g in range(ng):
            alpha = (ws[g] * inv)[:, h:h + 1]
            term = alpha * o_scr[g][h]
            acc = term if acc is None else acc + term
        out_ref[:, h * HEAD_DIM:(h + 1) * HEAD_DIM] = acc.astype(out_ref.dtype)


def _dsw_merge(os_, lses, tm=512):
    bsz, _, t, w = os_[0].shape
    nh = w // HEAD_DIM
    dils = tuple(o.shape[1] for o in os_)
    t = t * dils[0]
    nt = t // tm
    kern = functools.partial(_dsw_merge_kernel, dils=dils, tm=tm, nh=nh)
    o_specs = [pl.BlockSpec((1, d, tm // d, w), lambda b, i: (b, 0, i, 0)) for d in dils]
    l_specs = [pl.BlockSpec((1, d, tm // d, LANES), lambda b, i: (b, 0, i, 0)) for d in dils]
    return pl.pallas_call(
        kern,
        grid=(bsz, nt),
        in_specs=o_specs + l_specs,
        out_specs=pl.BlockSpec((tm, w), lambda b, i: (b * nt + i, 0)),
        out_shape=jax.ShapeDtypeStruct((bsz * t, w), BF16),
        scratch_shapes=([pltpu.VMEM((nh, tm, HEAD_DIM), F32) for _ in dils]
                        + [pltpu.VMEM((tm, LANES), F32) for _ in dils]),
        compiler_params=_params("parallel", "parallel"),
        name="dsw_merge",
    )(*os_, *lses)


def _t5_bucket(dist):
    max_exact = N_BUCKETS // 2
    large = max_exact + (jnp.log(jnp.maximum(dist, 1).astype(F32) / max_exact)
                         / math.log(MAX_DISTANCE / max_exact) * (N_BUCKETS - max_exact)).astype(jnp.int32)
    return jnp.where(dist < max_exact, dist, jnp.minimum(large, N_BUCKETS - 1))


def _dsw_bias(table, window, dil):
    span = window // dil
    qi = jnp.arange(DSW_BLOCK)[:, None]
    kj = jnp.arange(2 * DSW_BLOCK)[None, :]
    delta = qi + DSW_BLOCK - kj
    band = (delta >= 0) & (delta <= span)
    bias = table[_t5_bucket(jnp.maximum(delta, 0) * dil)].transpose(2, 0, 1).astype(F32)
    return jnp.where(band[None], bias, -jnp.inf)


def _router_kernel(x_ref, g_ref, w_ref, h_ref, lg_ref):
    x = x_ref[...]
    ms = jnp.mean(x * x, axis=-1, keepdims=True)
    h = x * lax.rsqrt(ms + RMS_EPS) * g_ref[...]
    h_ref[...] = h.astype(h_ref.dtype)
    lg_ref[...] = jnp.dot(h, w_ref[...], preferred_element_type=F32,
                          precision=lax.Precision.HIGHEST)


def _router(x, g, w_rt, tm=256):
    m, d = x.shape
    return pl.pallas_call(
        _router_kernel,
        grid=(m // tm,),
        in_specs=[pl.BlockSpec((tm, d), lambda i: (i, 0)),
                  pl.BlockSpec((1, d), lambda i: (0, 0)),
                  pl.BlockSpec((d, LANES), lambda i: (0, 0))],
        out_specs=[pl.BlockSpec((tm, d), lambda i: (i, 0)),
                   pl.BlockSpec((tm, LANES), lambda i: (i, 0))],
        out_shape=[jax.ShapeDtypeStruct((m, d), BF16),
                   jax.ShapeDtypeStruct((m, LANES), F32)],
        compiler_params=_params("parallel"),
        name="moe_router",
    )(x, g.reshape(1, d), w_rt)


def _ffn_up_kernel(be_ref, nv_ref, x_ref, wg_ref, wu_ref, o_ref):
    r = pl.program_id(1)

    @pl.when(r < nv_ref[0])
    def _():
        x = x_ref[...]
        g = jnp.dot(x, wg_ref[0].astype(BF16), preferred_element_type=F32)
        u = jnp.dot(x, wu_ref[0].astype(BF16), preferred_element_type=F32)
        o_ref[...] = (jax.nn.silu(g) * u).astype(o_ref.dtype)


def _ffn_down_kernel(be_ref, nv_ref, a_ref, wd_ref, o_ref):
    r = pl.program_id(0)

    @pl.when(r < nv_ref[0])
    def _():
        o_ref[...] = jnp.dot(a_ref[...], wd_ref[0].astype(BF16), preferred_element_type=F32)


def _expert_ffn(xg, blk_e, nvalid, w_gate, w_up, w_down, li, tm=MOE_TM, tf=384):
    p, d = xg.shape
    f = w_gate.shape[3]
    nb = p // tm
    nf = f // tf
    last = lambda r, nv: jnp.minimum(r, nv[0] - 1)
    act = pl.pallas_call(
        _ffn_up_kernel,
        grid_spec=pltpu.PrefetchScalarGridSpec(
            num_scalar_prefetch=2,
            grid=(nf, nb),
            in_specs=[pl.BlockSpec((tm, d), lambda j, r, be, nv: (last(r, nv), 0)),
                      pl.BlockSpec((None, 1, d, tf), lambda j, r, be, nv: (li, be[last(r, nv)], 0, j)),
                      pl.BlockSpec((None, 1, d, tf), lambda j, r, be, nv: (li, be[last(r, nv)], 0, j))],
            out_specs=pl.BlockSpec((tm, tf), lambda j, r, be, nv: (last(r, nv), j)),
        ),
        out_shape=jax.ShapeDtypeStruct((p, f), BF16),
        compiler_params=_params("arbitrary", "arbitrary"),
        name="moe_ffn_up",
    )(blk_e, nvalid, xg, w_gate, w_up)
    return pl.pallas_call(
        _ffn_down_kernel,
        grid_spec=pltpu.PrefetchScalarGridSpec(
            num_scalar_prefetch=2,
            grid=(nb,),
            in_specs=[pl.BlockSpec((tm, f), lambda r, be, nv: (last(r, nv), 0)),
                      pl.BlockSpec((None, 1, f, d), lambda r, be, nv: (li, be[last(r, nv)], 0, 0))],
            out_specs=pl.BlockSpec((tm, d), lambda r, be, nv: (last(r, nv), 0)),
        ),
        out_shape=jax.ShapeDtypeStruct((p, d), F32),
        compiler_params=_params("arbitrary"),
        name="moe_ffn_down",
    )(blk_e, nvalid, act, w_down)


def _hier_moe(x, norm_g, w_group, b_group, w_router, b_router, w_gate, w_up, w_down, li):
    n_tok, d = x.shape
    pad = LANES - N_GROUPS - N_EXPERTS
    w_rt = jnp.concatenate([w_group, w_router, jnp.zeros((d, pad), F32)], axis=1)
    h, logits = _router(x, norm_g, w_rt)

    glog = logits[:, :N_GROUPS] + b_group.astype(F32)
    gsel = jnp.argmax(glog, axis=-1)
    pg = jnp.take_along_axis(jax.nn.softmax(glog, axis=-1), gsel[:, None], axis=1)[:, 0]
    elog = (logits[:, N_GROUPS:N_GROUPS + N_EXPERTS] + b_router.astype(F32)).reshape(
        n_tok, N_GROUPS, EXPERTS_PER_GROUP)
    elog = jnp.take_along_axis(elog, gsel[:, None, None], axis=1)[:, 0]
    top_v, top_j = lax.top_k(elog, TOP_K)
    weights = pg[:, None] * jax.nn.softmax(top_v, axis=-1)
    eid = (gsel[:, None] * EXPERTS_PER_GROUP + top_j).astype(jnp.int32)

    tm = MOE_TM
    n_as = n_tok * TOP_K
    flat_e = eid.reshape(n_as)
    order = jnp.argsort(flat_e).astype(jnp.int32)
    counts = jnp.sum((flat_e[:, None] == jnp.arange(N_EXPERTS)[None, :]).astype(jnp.int32), axis=0)
    start = jnp.cumsum(counts) - counts
    pcounts = (counts + tm - 1) // tm * tm
    pend = jnp.cumsum(pcounts)
    pstart = pend - pcounts
    nb = -(-n_as // tm) + N_EXPERTS
    p_rows = nb * tm
    blk_e = jnp.minimum(jnp.searchsorted(pend, jnp.arange(nb) * tm, side='right'),
                        N_EXPERTS - 1).astype(jnp.int32)
    nvalid = (pend[-1:] // tm).astype(jnp.int32)
    prow = jnp.arange(p_rows, dtype=jnp.int32)
    row_e = jnp.repeat(blk_e, tm)
    within = prow - pstart[row_e]
    is_real = within < counts[row_e]
    src = jnp.where(is_real, order[jnp.clip(start[row_e] + within, 0, n_as - 1)], 0)
    buf_tok = src // TOP_K
    inv = jnp.zeros((n_as,), jnp.int32).at[order].set(jnp.arange(n_as, dtype=jnp.int32))
    dest = pstart[flat_e] + inv - start[flat_e]

    xg = jnp.take(h, buf_tok, axis=0)
    yb = _expert_ffn(xg, blk_e, nvalid, w_gate, w_up, w_down, li)
    dest = dest.reshape(n_tok, TOP_K)
    y = (weights[:, 0:1] * jnp.take(yb, dest[:, 0], axis=0)
         + weights[:, 1:2] * jnp.take(yb, dest[:, 1], axis=0))
    return x + y


def _ab_layer(x, norm_g, w_in, li, conv_w, conv_b, wa, ba, wi, bi, lam, bf, qn, kn, w_out, bsz):
    n_tok, d = x.shape
    t = n_tok // bsz
    d_rnn = conv_w.shape[1]
    nh = bf.shape[0]
    fw = nh * HEAD_DIM
    h = _rmsnorm(x, norm_g)
    zga = _proj_f32(h, w_in, li, 0, 2 * d_rnn).reshape(bsz, t, 2 * d_rnn)
    ya = _rglru(zga, bsz, conv_w, conv_b, wa, ba, wi, bi, lam)
    normw = jnp.concatenate([jnp.tile(qn, nh), jnp.tile(kn, nh), jnp.ones((fw,), F32)]).reshape(1, 3 * fw)
    qkv = _proj_heads(h, w_in, li, 2 * d_rnn, 3 * fw, normw, 2 * fw, bsz, 1)
    w_f = jnp.pad(w_in[li, :, 2 * d_rnn + 3 * fw:], ((0, 0), (0, LANES - nh)))
    b_f = jnp.pad(bf.astype(F32), (0, LANES - nh)).reshape(1, LANES)
    cum = _forget_cum(h, w_f, b_f, bsz)[:, :, :nh]
    cum_t = cum.transpose(0, 2, 1)
    yb = _fox_attention(qkv, cum_t[..., None], cum_t[:, :, None, :])
    y = jnp.concatenate([ya.reshape(n_tok, d_rnn), yb.reshape(n_tok, fw)], axis=-1)
    return _proj_res(y, w_out, li, x)


def _dilated_layer(x, norm_g, w_in, li, qn, kn, table, w_out, bsz):
    n_tok, d = x.shape
    t = n_tok // bsz
    nh = table.shape[1] // len(DSW_PATTERNS)
    fw = nh * HEAD_DIM
    h = _rmsnorm(x, norm_g)
    os_, lses = [], []
    for g, (window, dil) in enumerate(DSW_PATTERNS):
        normw = jnp.concatenate([jnp.tile(qn[g], nh), jnp.tile(kn[g], nh),
                                 jnp.ones((fw,), F32)]).reshape(1, 3 * fw)
        qkv = _proj_heads(h, w_in, li, g * 3 * fw, 3 * fw, normw, 2 * fw, bsz, dil)
        bias = _dsw_bias(table[:, g * nh:(g + 1) * nh], window, dil)
        o, lse = _dsw_group(qkv, bias, min(t // dil, 1024))
        os_.append(o)
        lses.append(lse)
    y = _dsw_merge(os_, lses)
    return _proj_res(y, w_out, li, x)


def kernel(x, mix_norm, ffn_norm, w_in_ab, conv_w, conv_b, lru_wa, lru_ba, lru_wi, lru_bi,
           lru_lambda, fox_bf, fox_qnorm, fox_knorm, w_out_ab, w_in_c, dsw_qnorm, dsw_knorm,
           w_out_c, rel_bias, moe_w_group, moe_b_group, moe_w_router, moe_b_router,
           moe_w_gate, moe_w_up, moe_w_down):
    bsz, t, d = x.shape
    depth = mix_norm.shape[0]
    xf = x.reshape(bsz * t, d)
    for layer in range(depth):
        i = layer // 2
        if layer % 2 == 0:
            xf = _ab_layer(xf, mix_norm[layer], w_in_ab, i, conv_w[i], conv_b[i], lru_wa[i],
                           lru_ba[i], lru_wi[i], lru_bi[i], lru_lambda[i], fox_bf[i],
                           fox_qnorm[i], fox_knorm[i], w_out_ab, bsz)
        else:
            xf = _dilated_layer(xf, mix_norm[layer], w_in_c, i, dsw_qnorm[i], dsw_knorm[i],
                                rel_bias, w_out_c, bsz)
        xf = _hier_moe(xf, ffn_norm[layer], moe_w_group[layer], moe_b_group[layer],
                       moe_w_router[layer], moe_b_router[layer], moe_w_gate, moe_w_up,
                       moe_w_down, layer)
    return xf.reshape(bsz, t, d)
```

```python
import functools
import math

import jax
import jax.numpy as jnp
from jax import lax
from jax.experimental import pallas as pl
from jax.experimental.pallas import tpu as pltpu

F32 = jnp.float32
BF16 = jnp.bfloat16

LANES = 128
SUBLANES = 8
VMEM_LIMIT = 56 * 1024 * 1024

HEAD_DIM = 128
RMS_EPS = 1e-6
RNN_BLOCKS = 16
CONV_WIDTH = 4
LRU_C = 8.0
DSW_PATTERNS = ((128, 1), (512, 4), (2048, 16))
DSW_BLOCK = 128
N_BUCKETS = 32
MAX_DISTANCE = 2048
N_GROUPS = 4
EXPERTS_PER_GROUP = 8
N_EXPERTS = N_GROUPS * EXPERTS_PER_GROUP
TOP_K = 2
LOG2E = math.log2(math.e)

MOE_TM = 256


def _params(*sem):
    return pltpu.CompilerParams(dimension_semantics=sem, vmem_limit_bytes=VMEM_LIMIT)


def _rmsnorm_kernel(x_ref, g_ref, o_ref):
    x = x_ref[...]
    ms = jnp.mean(x * x, axis=-1, keepdims=True)
    o_ref[...] = (x * lax.rsqrt(ms + RMS_EPS) * g_ref[...]).astype(o_ref.dtype)


def _rmsnorm(x, g, tm=256):
    m, d = x.shape
    return pl.pallas_call(
        _rmsnorm_kernel,
        grid=(m // tm,),
        in_specs=[pl.BlockSpec((tm, d), lambda i: (i, 0)),
                  pl.BlockSpec((1, d), lambda i: (0, 0))],
        out_specs=pl.BlockSpec((tm, d), lambda i: (i, 0)),
        out_shape=jax.ShapeDtypeStruct((m, d), BF16),
        compiler_params=_params("parallel"),
        name="rmsnorm",
    )(x, g.reshape(1, d))


def _proj_f32_kernel(a_ref, w_ref, o_ref):
    o_ref[...] = jnp.dot(a_ref[...], w_ref[...].astype(BF16), preferred_element_type=F32)


def _proj_f32(a, w, li, col0, ncols, tm=1024, tn=512):
    m, k = a.shape
    jb = col0 // tn
    return pl.pallas_call(
        _proj_f32_kernel,
        grid=(m // tm, ncols // tn),
        in_specs=[pl.BlockSpec((tm, k), lambda i, j: (i, 0)),
                  pl.BlockSpec((None, k, tn), lambda i, j: (li, 0, jb + j))],
        out_specs=pl.BlockSpec((tm, tn), lambda i, j: (i, j)),
        out_shape=jax.ShapeDtypeStruct((m, ncols), F32),
        compiler_params=_params("parallel", "arbitrary"),
        name="proj_f32",
    )(a, w)


def _proj_res_kernel(*refs, n_in):
    a_refs, w_refs = refs[:n_in], refs[n_in:2 * n_in]
    r_ref, o_ref = refs[2 * n_in], refs[2 * n_in + 1]
    acc = r_ref[...]
    for a_ref, w_ref in zip(a_refs, w_refs):
        acc = acc + jnp.dot(a_ref[...], w_ref[...].astype(BF16), preferred_element_type=F32)
    o_ref[...] = acc


def _proj_res(a_list, w, li, res, tm=1024, tn=512):
    m, k = a_list[0].shape
    n_in = len(a_list)
    n = w.shape[2]
    a_specs = [pl.BlockSpec((tm, k), lambda i, j: (i, 0)) for _ in a_list]
    w_specs = [pl.BlockSpec((None, k, tn), lambda i, j, p=p: (li, p, j)) for p in range(n_in)]
    return pl.pallas_call(
        functools.partial(_proj_res_kernel, n_in=n_in),
        grid=(m // tm, n // tn),
        in_specs=a_specs + w_specs + [pl.BlockSpec((tm, tn), lambda i, j: (i, j))],
        out_specs=pl.BlockSpec((tm, tn), lambda i, j: (i, j)),
        out_shape=jax.ShapeDtypeStruct((m, n), F32),
        compiler_params=_params("parallel", "arbitrary"),
        name="proj_res",
    )(*a_list, *([w] * n_in), res)


def _proj_heads_kernel(a_ref, w_ref, nw_ref, o_ref, y_scr, *, dil, n_norm_blocks, hpb, tm):
    j = pl.program_id(2)
    acc = jnp.dot(a_ref[...], w_ref[...].astype(BF16), preferred_element_type=F32)
    for hh in range(hpb):
        y_scr[hh] = acc[:, hh * HEAD_DIM:(hh + 1) * HEAD_DIM]

    @pl.when(j < n_norm_blocks)
    def _():
        for hh in range(hpb):
            y = y_scr[hh]
            ms = jnp.mean(y * y, axis=-1, keepdims=True)
            y_scr[hh] = y * lax.rsqrt(ms + RMS_EPS) * nw_ref[:, hh * HEAD_DIM:(hh + 1) * HEAD_DIM]

    rows = tm // dil
    for z in range(dil):
        for hh in range(hpb):
            if dil == 1:
                o_ref[0, z, hh] = y_scr[hh].astype(o_ref.dtype)
            else:
                o_ref[0, z, hh] = y_scr[hh, pl.ds(z, rows, stride=dil), :].astype(o_ref.dtype)


def _proj_heads(a, w, li, col0, ncols, normw, n_norm_cols, bsz, dil, tm=1024, tn=512):
    m, k = a.shape
    t = m // bsz
    nt = t // tm
    hpb = tn // HEAD_DIM
    jb = col0 // tn
    kern = functools.partial(_proj_heads_kernel, dil=dil, n_norm_blocks=n_norm_cols // tn,
                             hpb=hpb, tm=tm)
    return pl.pallas_call(
        kern,
        grid=(bsz, nt, ncols // tn),
        in_specs=[pl.BlockSpec((tm, k), lambda b, i, j: (b * nt + i, 0)),
                  pl.BlockSpec((None, k, tn), lambda b, i, j: (li, 0, jb + j)),
                  pl.BlockSpec((1, tn), lambda b, i, j: (0, j))],
        out_specs=pl.BlockSpec((1, dil, hpb, tm // dil, HEAD_DIM),
                               lambda b, i, j: (b, 0, j, i, 0)),
        out_shape=jax.ShapeDtypeStruct((bsz, dil, ncols // HEAD_DIM, t // dil, HEAD_DIM), BF16),
        scratch_shapes=[pltpu.VMEM((hpb, tm, HEAD_DIM), F32)],
        compiler_params=_params("parallel", "parallel", "arbitrary"),
        name=f"proj_heads_d{dil}",
    )(a, w, normw)


def _forget_cum_kernel(a_ref, w_ref, b_ref, o_ref, carry_ref, *, tm):
    i = pl.program_id(1)

    @pl.when(i == 0)
    def _():
        carry_ref[...] = jnp.zeros_like(carry_ref)

    fz = jnp.dot(a_ref[...], w_ref[...].astype(BF16), preferred_element_type=F32) + b_ref[...]
    c = jax.nn.log_sigmoid(fz)
    row = lax.broadcasted_iota(jnp.int32, c.shape, 0)
    s = 1
    while s < tm:
        c = c + jnp.where(row >= s, pltpu.roll(c, s, 0), 0.0)
        s *= 2
    c = c + carry_ref[0:1, :]
    o_ref[0] = c
    carry_ref[...] = jnp.broadcast_to(c[tm - 1:tm, :], carry_ref.shape)


def _forget_cum(a, w_f, b_f, bsz, tm=512):
    m, k = a.shape
    t = m // bsz
    nt = t // tm
    return pl.pallas_call(
        functools.partial(_forget_cum_kernel, tm=tm),
        grid=(bsz, nt),
        in_specs=[pl.BlockSpec((tm, k), lambda b, i: (b * nt + i, 0)),
                  pl.BlockSpec((k, LANES), lambda b, i: (0, 0)),
                  pl.BlockSpec((1, LANES), lambda b, i: (0, 0))],
        out_specs=pl.BlockSpec((1, tm, LANES), lambda b, i: (b, i, 0)),
        out_shape=jax.ShapeDtypeStruct((bsz, t, LANES), F32),
        scratch_shapes=[pltpu.VMEM((SUBLANES, LANES), F32)],
        compiler_params=_params("parallel", "arbitrary"),
        name="forget_cum",
    )(a, w_f, b_f)


def _gelu_tanh(x):
    return 0.5 * x * (1.0 + jnp.tanh(math.sqrt(2.0 / math.pi) * (x + 0.044715 * (x * x * x))))


def _rglru_kernel(g_ref, x_ref, cw_ref, cb_ref, wa_ref, ba_ref, wi_ref, bi_ref, lam_ref,
                  o_ref, xs_scr, a_scr, u_scr, h_scr, carry_scr, *, tc, nblk):
    ti = pl.program_id(2)
    halo = SUBLANES

    @pl.when(ti == 0)
    def _():
        xs_scr[0:halo, :] = jnp.zeros((halo, xs_scr.shape[1]), F32)
        carry_scr[...] = jnp.zeros_like(carry_scr)

    xs_scr[halo:halo + tc, :] = x_ref[0]
    xc = cb_ref[...] + cw_ref[CONV_WIDTH - 1:CONV_WIDTH, :] * xs_scr[halo:halo + tc, :]
    for j in range(CONV_WIDTH - 1):
        sh = CONV_WIDTH - 1 - j
        xc = xc + cw_ref[j:j + 1, :] * xs_scr[halo - sh:halo - sh + tc, :]
    xs_scr[0:halo, :] = xs_scr[tc:tc + halo, :]

    neg_sp = -LRU_C * jax.nn.softplus(-lam_ref[...])
    for n in range(nblk):
        cs = slice(n * HEAD_DIM, (n + 1) * HEAD_DIM)
        xb = xc[:, cs]
        xbb = xb.astype(BF16)
        r = jax.nn.sigmoid(jnp.dot(xbb, wa_ref[n].astype(BF16), preferred_element_type=F32)
                           + ba_ref[:, cs])
        ig = jax.nn.sigmoid(jnp.dot(xbb, wi_ref[n].astype(BF16), preferred_element_type=F32)
                            + bi_ref[:, cs])
        log_a = neg_sp[:, cs] * r
        a = jnp.exp(log_a)
        a_scr[:, cs] = a
        u_scr[:, cs] = jnp.sqrt(-jnp.tanh(log_a) * (a * a + 1.0)) * (ig * xb)

    row = lax.broadcasted_iota(jnp.int32, (SUBLANES, a_scr.shape[1]), 0)

    def body(g, carry):
        r0 = pl.multiple_of(g * SUBLANES, SUBLANES)
        a = a_scr[pl.ds(r0, SUBLANES), :]
        u = u_scr[pl.ds(r0, SUBLANES), :]
        s = 1
        while s < SUBLANES:
            keep = row >= s
            u = jnp.where(keep, a * pltpu.roll(u, s, 0) + u, u)
            a = jnp.where(keep, a * pltpu.roll(a, s, 0), a)
            s *= 2
        h = a * carry + u
        h_scr[pl.ds(r0, SUBLANES), :] = h
        return jnp.broadcast_to(h[SUBLANES - 1:SUBLANES, :], h.shape)

    carry = lax.fori_loop(0, tc // SUBLANES, body, carry_scr[...])
    carry_scr[...] = carry
    o_ref[0] = (h_scr[...] * _gelu_tanh(g_ref[0])).astype(o_ref.dtype)


def _rglru(zga, bsz, conv_w, conv_b, wa, ba, wi, bi, lam, tc=256, cw=512):
    _, t, c2 = zga.shape
    c = c2 // 2
    nblk = cw // HEAD_DIM
    ncb = c // cw
    vec = lambda v: v.reshape(1, c)
    vspec = pl.BlockSpec((1, cw), lambda b, j, i: (0, j))
    kern = functools.partial(_rglru_kernel, tc=tc, nblk=nblk)
    return pl.pallas_call(
        kern,
        grid=(bsz, ncb, t // tc),
        in_specs=[pl.BlockSpec((1, tc, cw), lambda b, j, i: (b, i, j)),
                  pl.BlockSpec((1, tc, cw), lambda b, j, i: (b, i, ncb + j)),
                  pl.BlockSpec((CONV_WIDTH, cw), lambda b, j, i: (0, j)),
                  vspec,
                  pl.BlockSpec((nblk, HEAD_DIM, HEAD_DIM), lambda b, j, i: (j, 0, 0)),
                  vspec,
                  pl.BlockSpec((nblk, HEAD_DIM, HEAD_DIM), lambda b, j, i: (j, 0, 0)),
                  vspec, vspec],
        out_specs=pl.BlockSpec((1, tc, cw), lambda b, j, i: (b, i, j)),
        out_shape=jax.ShapeDtypeStruct((bsz, t, c), BF16),
        scratch_shapes=[pltpu.VMEM((tc + SUBLANES, cw), F32),
                        pltpu.VMEM((tc, cw), F32),
                        pltpu.VMEM((tc, cw), F32),
                        pltpu.VMEM((tc, cw), F32),
                        pltpu.VMEM((SUBLANES, cw), F32)],
        compiler_params=_params("parallel", "parallel", "arbitrary"),
        name="rglru",
    )(zga, zga, conv_w, vec(conv_b), wa, vec(ba), wi, vec(bi), vec(lam))


FOX_STRIP = 128
FOX_QK_SCALE = LOG2E / math.sqrt(HEAD_DIM)


def _fox_kernel(q_ref, k_ref, v_ref, cq_ref, ck_ref, o_ref, m_scr, l_scr, acc_scr, cq_scr, *, tq, hp):
    g = pl.program_id(1)
    qi = pl.program_id(2)
    tk = tq
    sr = FOX_STRIP
    lane = lax.broadcasted_iota(jnp.int32, (tq, LANES), 1)
    for hh in range(hp):
        cq = jnp.sum(jnp.where(lane == g * hp + hh, cq_ref[0], 0.0), axis=-1, keepdims=True)
        cq_scr[hh] = jnp.broadcast_to(cq * LOG2E, (tq, LANES))
        m_scr[hh] = jnp.full(m_scr.shape[1:], -jnp.inf, F32)
        l_scr[hh] = jnp.zeros(l_scr.shape[1:], F32)
        acc_scr[hh] = jnp.zeros(acc_scr.shape[1:], F32)
    tri = (lax.broadcasted_iota(jnp.int32, (sr, LANES), 1)
           <= lax.broadcasted_iota(jnp.int32, (sr, LANES), 0))

    def scores(hh, r, j, diagonal):
        rs = slice(r * sr, (r + 1) * sr)
        k0 = pl.multiple_of(j * tk, tk)
        nk = (r + 1) * sr if diagonal else tk
        return lax.dot_general(q_ref[0, 0, hh, rs, :], k_ref[0, 0, hh, pl.ds(k0, nk), :],
                               (((1,), (1,)), ((), ())), preferred_element_type=F32)

    def softmax_pv(s, hh, r, j, ck, diagonal):
        rs = slice(r * sr, (r + 1) * sr)
        k0 = pl.multiple_of(j * tk, tk)
        nk = s.shape[1]
        cq = cq_scr[hh, rs, :]
        parts = []
        for c in range(nk // LANES):
            cs = slice(c * LANES, (c + 1) * LANES)
            part = s[:, cs] + (cq - ck[:, cs])
            if diagonal and c == r:
                part = jnp.where(tri, part, -jnp.inf)
            parts.append(part)
        m_prev = m_scr[hh, rs, :]
        m_new = jnp.maximum(m_prev, jnp.max(functools.reduce(jnp.maximum, parts),
                                            axis=-1, keepdims=True))
        alpha = jnp.exp2(m_prev - m_new)
        ps = [jnp.exp2(part - m_new) for part in parts]
        row_sum = jnp.sum(functools.reduce(lambda a, b: a + b, ps), axis=-1, keepdims=True)
        l_scr[hh, rs, :] = alpha * l_scr[hh, rs, :] + row_sum
        p = jnp.concatenate(ps, axis=1).astype(BF16)
        acc_scr[hh, rs, :] = alpha * acc_scr[hh, rs, :] + jnp.dot(
            p, v_ref[0, 0, hh, pl.ds(k0, nk), :], preferred_element_type=F32)
        m_scr[hh, rs, :] = m_new

    def kv_step(j, diagonal):
        cks = [ck_ref[0, hh, pl.ds(j, 1), :] * LOG2E for hh in range(hp)]
        strips = [(hh, r) for hh in range(hp) for r in range(tq // sr)]
        s_next = scores(*strips[0], j, diagonal)
        for n, (hh, r) in enumerate(strips):
            s_cur = s_next
            if n + 1 < len(strips):
                s_next = scores(*strips[n + 1], j, diagonal)
            softmax_pv(s_cur, hh, r, j, cks[hh], diagonal)

    def below_diagonal(j, carry):
        kv_step(j, False)
        return carry

    lax.fori_loop(0, qi, below_diagonal, 0)
    kv_step(qi, True)
    for hh in range(hp):
        o_ref[0, :, hh * HEAD_DIM:(hh + 1) * HEAD_DIM] = (acc_scr[hh] / l_scr[hh]).astype(o_ref.dtype)


def _fox_attention(qkv, cum_col, cum_row, tq=512, hp=2):
    bsz, _, h3, t, dh = qkv.shape
    nh = h3 // 3
    nq = t // tq
    ng = nh // hp
    kern = functools.partial(_fox_kernel, tq=tq, hp=hp)
    return pl.pallas_call(
        kern,
        grid=(bsz, ng, nq),
        in_specs=[pl.BlockSpec((1, 1, hp, tq, dh), lambda b, g, i: (b, 0, g, i, 0)),
                  pl.BlockSpec((1, 1, hp, t, dh), lambda b, g, i: (b, 0, ng + g, 0, 0)),
                  pl.BlockSpec((1, 1, hp, t, dh), lambda b, g, i: (b, 0, 2 * ng + g, 0, 0)),
                  pl.BlockSpec((1, tq, LANES), lambda b, g, i: (b, i, 0)),
                  pl.BlockSpec((1, hp, nq, tq), lambda b, g, i: (b, g, 0, 0))],
        out_specs=pl.BlockSpec((1, tq, hp * dh), lambda b, g, i: (b, i, g)),
        out_shape=jax.ShapeDtypeStruct((bsz, t, nh * dh), BF16),
        scratch_shapes=[pltpu.VMEM((hp, tq, LANES), F32), pltpu.VMEM((hp, tq, LANES), F32),
                        pltpu.VMEM((hp, tq, dh), F32), pltpu.VMEM((hp, tq, LANES), F32)],
        compiler_params=_params("parallel", "parallel", "arbitrary"),
        name="fox_attention",
    )(qkv, qkv, qkv, cum_col, cum_row)


def _dsw_kernel(q_ref, kc_ref, kp_ref, vc_ref, vp_ref, bias_ref, o_ref, lse_ref, *, lc, nh):
    c = pl.program_id(2)
    blk = DSW_BLOCK
    nb = lc // blk
    lane = lax.broadcasted_iota(jnp.int32, (blk, LANES), 1)
    col = lax.broadcasted_iota(jnp.int32, (blk, 2 * blk), 1)
    scale = 1.0 / math.sqrt(HEAD_DIM)
    lse_ref[...] = jnp.zeros_like(lse_ref)

    def head(h, _):
        bias = bias_ref[h]
        for n in range(nb):
            rs = slice(n * blk, (n + 1) * blk)
            q = q_ref[0, 0, h, rs, :]
            if n == 0:
                kprev = kp_ref[0, 0, h]
                vprev = vp_ref[0, 0, h]
            else:
                ps = slice((n - 1) * blk, n * blk)
                kprev = kc_ref[0, 0, h, ps, :]
                vprev = vc_ref[0, 0, h, ps, :]
            kk = jnp.concatenate([kprev, kc_ref[0, 0, h, rs, :]], axis=0)
            vv = jnp.concatenate([vprev, vc_ref[0, 0, h, rs, :]], axis=0)
            s = lax.dot_general(q, kk, (((1,), (1,)), ((), ())),
                                preferred_element_type=F32) * scale + bias
            if n == 0:
                s = jnp.where((c > 0) | (col >= blk), s, -jnp.inf)
            m = jnp.max(s, axis=-1, keepdims=True)
            e = jnp.exp(s - m)
            den = jnp.sum(e, axis=-1, keepdims=True)
            o = jnp.dot(e.astype(BF16), vv, preferred_element_type=F32) / den
            o_ref[0, 0, rs, pl.ds(pl.multiple_of(h * HEAD_DIM, HEAD_DIM), HEAD_DIM)] = (
                o.astype(o_ref.dtype))
            lse = m + jnp.log(den)
            lse_ref[0, 0, rs, :] = jnp.where(lane == h, lse, lse_ref[0, 0, rs, :])
        return 0

    lax.fori_loop(0, nh, head, 0)


def _dsw_group(qkv, bias, lc):
    bsz, dil, h3, l, dh = qkv.shape
    nh = h3 // 3
    nc = l // lc
    bpc = lc // DSW_BLOCK
    kern = functools.partial(_dsw_kernel, lc=lc, nh=nh)
    cur = lambda off: pl.BlockSpec((1, 1, nh, lc, dh), lambda b, z, c: (b, z, off, c, 0))
    prev = lambda off: pl.BlockSpec((1, 1, nh, DSW_BLOCK, dh),
                                    lambda b, z, c: (b, z, off, jnp.maximum(c * bpc - 1, 0), 0))
    return pl.pallas_call(
        kern,
        grid=(bsz, dil, nc),
        in_specs=[cur(0), cur(1), prev(1), cur(2), prev(2),
                  pl.BlockSpec((nh, DSW_BLOCK, 2 * DSW_BLOCK), lambda b, z, c: (0, 0, 0))],
        out_specs=[pl.BlockSpec((1, 1, lc, nh * dh), lambda b, z, c: (b, z, c, 0)),
                   pl.BlockSpec((1, 1, lc, LANES), lambda b, z, c: (b, z, c, 0))],
        out_shape=[jax.ShapeDtypeStruct((bsz, dil, l, nh * dh), BF16),
                   jax.ShapeDtypeStruct((bsz, dil, l, LANES), F32)],
        compiler_params=_params("parallel", "parallel", "arbitrary"),
        name=f"dsw_attention_d{dil}",
    )(qkv, qkv, qkv, qkv, qkv, bias)


def _dsw_merge_kernel(*refs, dils, tm, nh):
    ng = len(dils)
    o_refs = refs[:ng]
    l_refs = refs[ng:2 * ng]
    out_ref = refs[2 * ng]
    o_scr = refs[2 * ng + 1:2 * ng + 1 + ng]
    l_scr = refs[2 * ng + 1 + ng:]
    for g, dil in enumerate(dils):
        rows = tm // dil
        for z in range(dil):
            rsel = slice(None) if dil == 1 else pl.ds(z, rows, stride=dil)
            l_scr[g][rsel, :] = l_refs[g][0, z]
            for h in range(nh):
                o_scr[g][h, rsel, :] = o_refs[g][0, z, :, h * HEAD_DIM:(h + 1) * HEAD_DIM].astype(F32)
    lses = [l_scr[g][...] for g in range(ng)]
    m = functools.reduce(jnp.maximum, lses)
    ws = [jnp.exp(l - m) for l in lses]
    inv = 1.0 / functools.reduce(lambda a, b: a + b, ws)
    for h in range(nh):
        acc = None
        for g in range(ng):
            alpha = (ws[g] * inv)[:, h:h + 1]
            term = alpha * o_scr[g][h]
            acc = term if acc is None else acc + term
        out_ref[:, h * HEAD_DIM:(h + 1) * HEAD_DIM] = acc.astype(out_ref.dtype)


def _dsw_merge(os_, lses, tm=512):
    bsz, _, t, w = os_[0].shape
    nh = w // HEAD_DIM
    dils = tuple(o.shape[1] for o in os_)
    t = t * dils[0]
    nt = t // tm
    kern = functools.partial(_dsw_merge_kernel, dils=dils, tm=tm, nh=nh)
    o_specs = [pl.BlockSpec((1, d, tm // d, w), lambda b, i: (b, 0, i, 0)) for d in dils]
    l_specs = [pl.BlockSpec((1, d, tm // d, LANES), lambda b, i: (b, 0, i, 0)) for d in dils]
    return pl.pallas_call(
        kern,
        grid=(bsz, nt),
        in_specs=o_specs + l_specs,
        out_specs=pl.BlockSpec((tm, w), lambda b, i: (b * nt + i, 0)),
        out_shape=jax.ShapeDtypeStruct((bsz * t, w), BF16),
        scratch_shapes=([pltpu.VMEM((nh, tm, HEAD_DIM), F32) for _ in dils]
                        + [pltpu.VMEM((tm, LANES), F32) for _ in dils]),
        compiler_params=_params("parallel", "parallel"),
        name="dsw_merge",
    )(*os_, *lses)


def _t5_bucket(dist):
    max_exact = N_BUCKETS // 2
    large = max_exact + (jnp.log(jnp.maximum(dist, 1).astype(F32) / max_exact)
                         / math.log(MAX_DISTANCE / max_exact) * (N_BUCKETS - max_exact)).astype(jnp.int32)
    return jnp.where(dist < max_exact, dist, jnp.minimum(large, N_BUCKETS - 1))


def _dsw_bias(table, window, dil):
    span = window // dil
    qi = jnp.arange(DSW_BLOCK)[:, None]
    kj = jnp.arange(2 * DSW_BLOCK)[None, :]
    delta = qi + DSW_BLOCK - kj
    band = (delta >= 0) & (delta <= span)
    bias = table[_t5_bucket(jnp.maximum(delta, 0) * dil)].transpose(2, 0, 1).astype(F32)
    return jnp.where(band[None], bias, -jnp.inf)


def _router_kernel(x_ref, g_ref, w_ref, lg_ref):
    x = x_ref[...]
    ms = jnp.mean(x * x, axis=-1, keepdims=True)
    h = x * lax.rsqrt(ms + RMS_EPS) * g_ref[...]
    lg_ref[...] = jnp.dot(h, w_ref[...], preferred_element_type=F32,
                          precision=lax.Precision.HIGHEST)


def _router(x, g, w_rt, tm=256):
    m, d = x.shape
    return pl.pallas_call(
        _router_kernel,
        grid=(m // tm,),
        in_specs=[pl.BlockSpec((tm, d), lambda i: (i, 0)),
                  pl.BlockSpec((1, d), lambda i: (0, 0)),
                  pl.BlockSpec((d, LANES), lambda i: (0, 0))],
        out_specs=pl.BlockSpec((tm, LANES), lambda i: (i, 0)),
        out_shape=jax.ShapeDtypeStruct((m, LANES), F32),
        compiler_params=_params("parallel"),
        name="moe_router",
    )(x, g.reshape(1, d), w_rt)


def _issue_row_gather(idx_ref, src_hbm, dst_ref, sem, n_rows):
    def issue(i, carry):
        row = idx_ref[0, 0, i]
        pltpu.make_async_copy(src_hbm.at[pl.ds(row, 1)], dst_ref.at[pl.ds(i, 1)], sem).start()
        return carry
    lax.fori_loop(0, n_rows, issue, 0, unroll=8)


def _wait_row_gather(src_hbm, dst_ref, sem, n_rows):
    pltpu.make_async_copy(src_hbm.at[pl.ds(0, n_rows)], dst_ref, sem).wait()


def _gather_norm_kernel(nv_ref, idx_ref, idx_next_ref, x_hbm, g_ref, o_ref, xs_scr, sems, *, tm):
    r = pl.program_id(0)
    nv = nv_ref[0]
    slot = r % 2

    @pl.when(r == 0)
    def _():
        _issue_row_gather(idx_ref, x_hbm, xs_scr.at[0], sems.at[0], tm)

    @pl.when(r + 1 < nv)
    def _():
        _issue_row_gather(idx_next_ref, x_hbm, xs_scr.at[1 - slot], sems.at[1 - slot], tm)

    @pl.when(r < nv)
    def _():
        _wait_row_gather(x_hbm, xs_scr.at[slot], sems.at[slot], tm)
        x = xs_scr[slot]
        ms = jnp.mean(x * x, axis=-1, keepdims=True)
        o_ref[...] = (x * lax.rsqrt(ms + RMS_EPS) * g_ref[...]).astype(o_ref.dtype)

    @pl.when(r >= nv)
    def _():
        o_ref[...] = jnp.zeros_like(o_ref)


def _gather_norm(x, g, buf_tok, nvalid, tm):
    n, d = x.shape
    nb = buf_tok.shape[0] // tm
    idx = buf_tok.reshape(nb, 1, tm)
    last = lambda r, nv: jnp.minimum(r, nv[0] - 1)
    return pl.pallas_call(
        functools.partial(_gather_norm_kernel, tm=tm),
        grid_spec=pltpu.PrefetchScalarGridSpec(
            num_scalar_prefetch=1,
            grid=(nb,),
            in_specs=[pl.BlockSpec((1, 1, tm), lambda r, nv: (last(r, nv), 0, 0),
                                   memory_space=pltpu.SMEM),
                      pl.BlockSpec((1, 1, tm), lambda r, nv: (last(r + 1, nv), 0, 0),
                                   memory_space=pltpu.SMEM),
                      pl.BlockSpec(memory_space=pl.ANY),
                      pl.BlockSpec((1, d), lambda r, nv: (0, 0))],
            out_specs=pl.BlockSpec((tm, d), lambda r, nv: (r, 0)),
            scratch_shapes=[pltpu.VMEM((2, tm, d), F32), pltpu.SemaphoreType.DMA((2,))],
        ),
        out_shape=jax.ShapeDtypeStruct((nb * tm, d), BF16),
        compiler_params=_params("arbitrary"),
        name="moe_gather_norm",
    )(nvalid, idx, idx, x, g.reshape(1, d))


def _combine_kernel(idx_ref, idx_next_ref, x_ref, w_ref, y_hbm, *rest, tm, with_norm):
    if with_norm:
        g_ref, o_ref, h_ref, ys_scr, sems = rest
    else:
        o_ref, ys_scr, sems = rest
    r = pl.program_id(0)
    nr = pl.num_programs(0)
    slot = r % 2
    n_rows = TOP_K * tm

    @pl.when(r == 0)
    def _():
        _issue_row_gather(idx_ref, y_hbm, ys_scr.at[0], sems.at[0], n_rows)

    @pl.when(r + 1 < nr)
    def _():
        _issue_row_gather(idx_next_ref, y_hbm, ys_scr.at[1 - slot], sems.at[1 - slot], n_rows)

    _wait_row_gather(y_hbm, ys_scr.at[slot], sems.at[slot], n_rows)
    w0 = w_ref[:, 0:LANES]
    w1 = w_ref[:, LANES:2 * LANES]
    d = x_ref.shape[1]
    ssq = jnp.zeros((tm, LANES), F32)
    for c in range(d // LANES):
        cs = slice(c * LANES, (c + 1) * LANES)
        o = x_ref[:, cs] + (w0 * ys_scr[slot, 0:tm, cs] + w1 * ys_scr[slot, tm:2 * tm, cs])
        o_ref[:, cs] = o
        if with_norm:
            ssq = ssq + o * o
    if with_norm:
        inv = lax.rsqrt(jnp.sum(ssq, axis=-1, keepdims=True) / d + RMS_EPS)
        h_ref[...] = (o_ref[...] * inv * g_ref[...]).astype(h_ref.dtype)


def _combine(x, w_exp, yb, dest, next_norm_g, tm=128):
    n, d = x.shape
    nr = n // tm
    with_norm = next_norm_g is not None
    in_specs = [pl.BlockSpec((1, 1, TOP_K * tm), lambda r: (r, 0, 0), memory_space=pltpu.SMEM),
                pl.BlockSpec((1, 1, TOP_K * tm), lambda r: (jnp.minimum(r + 1, nr - 1), 0, 0),
                             memory_space=pltpu.SMEM),
                pl.BlockSpec((tm, d), lambda r: (r, 0)),
                pl.BlockSpec((tm, 2 * LANES), lambda r: (r, 0)),
                pl.BlockSpec(memory_space=pl.ANY)]
    args = [dest, dest, x, w_exp, yb]
    out_specs = [pl.BlockSpec((tm, d), lambda r: (r, 0))]
    out_shape = [jax.ShapeDtypeStruct((n, d), F32)]
    if with_norm:
        in_specs.append(pl.BlockSpec((1, d), lambda r: (0, 0)))
        args.append(next_norm_g.reshape(1, d))
        out_specs.append(pl.BlockSpec((tm, d), lambda r: (r, 0)))
        out_shape.append(jax.ShapeDtypeStruct((n, d), BF16))
    res = pl.pallas_call(
        functools.partial(_combine_kernel, tm=tm, with_norm=with_norm),
        grid=(nr,),
        in_specs=in_specs,
        out_specs=out_specs,
        out_shape=out_shape,
        scratch_shapes=[pltpu.VMEM((2, TOP_K * tm, d), F32), pltpu.SemaphoreType.DMA((2,))],
        compiler_params=_params("arbitrary"),
        name="moe_combine",
    )(*args)
    return (res[0], res[1]) if with_norm else (res[0], None)


def _expert_changed(be_ref, r):
    return (r == 0) | (be_ref[r] != be_ref[jnp.maximum(r - 1, 0)])


def _ffn_up_kernel(be_ref, nv_ref, x_ref, wg_ref, wu_ref, o_ref, w_scr, *, tf):
    r = pl.program_id(1)
    valid = r < nv_ref[0]

    @pl.when(valid & _expert_changed(be_ref, r))
    def _():
        w_scr[:, 0:tf] = wg_ref[0].astype(BF16)
        w_scr[:, tf:2 * tf] = wu_ref[0].astype(BF16)

    @pl.when(valid)
    def _():
        gu = jnp.dot(x_ref[...], w_scr[...], preferred_element_type=F32)
        o_ref[...] = (jax.nn.silu(gu[:, 0:tf]) * gu[:, tf:2 * tf]).astype(o_ref.dtype)

    @pl.when(jnp.logical_not(valid))
    def _():
        o_ref[...] = jnp.zeros_like(o_ref)


def _ffn_down_kernel(be_ref, nv_ref, a_ref, wd_ref, o_ref, w_scr):
    r = pl.program_id(0)
    valid = r < nv_ref[0]

    @pl.when(valid & _expert_changed(be_ref, r))
    def _():
        w_scr[...] = wd_ref[0].astype(BF16)

    @pl.when(valid)
    def _():
        o_ref[...] = jnp.dot(a_ref[...], w_scr[...], preferred_element_type=F32)

    @pl.when(jnp.logical_not(valid))
    def _():
        o_ref[...] = jnp.zeros_like(o_ref)


def _expert_ffn(xg, blk_e, nvalid, w_gate, w_up, w_down, li, tm, tf=384):
    p, d = xg.shape
    f = w_gate.shape[3]
    nb = p // tm
    nf = f // tf
    last = lambda r, nv: jnp.minimum(r, nv[0] - 1)
    act = pl.pallas_call(
        functools.partial(_ffn_up_kernel, tf=tf),
        grid_spec=pltpu.PrefetchScalarGridSpec(
            num_scalar_prefetch=2,
            grid=(nf, nb),
            in_specs=[pl.BlockSpec((tm, d), lambda j, r, be, nv: (last(r, nv), 0)),
                      pl.BlockSpec((None, 1, d, tf), lambda j, r, be, nv: (li, be[last(r, nv)], 0, j)),
                      pl.BlockSpec((None, 1, d, tf), lambda j, r, be, nv: (li, be[last(r, nv)], 0, j))],
            out_specs=pl.BlockSpec((tm, tf), lambda j, r, be, nv: (r, j)),
            scratch_shapes=[pltpu.VMEM((d, 2 * tf), BF16)],
        ),
        out_shape=jax.ShapeDtypeStruct((p, f), BF16),
        compiler_params=_params("arbitrary", "arbitrary"),
        name="moe_ffn_up",
    )(blk_e, nvalid, xg, w_gate, w_up)
    return pl.pallas_call(
        _ffn_down_kernel,
        grid_spec=pltpu.PrefetchScalarGridSpec(
            num_scalar_prefetch=2,
            grid=(nb,),
            in_specs=[pl.BlockSpec((tm, f), lambda r, be, nv: (last(r, nv), 0)),
                      pl.BlockSpec((None, 1, f, d), lambda r, be, nv: (li, be[last(r, nv)], 0, 0))],
            out_specs=pl.BlockSpec((tm, d), lambda r, be, nv: (r, 0)),
            scratch_shapes=[pltpu.VMEM((f, d), BF16)],
        ),
        out_shape=jax.ShapeDtypeStruct((p, d), F32),
        compiler_params=_params("arbitrary"),
        name="moe_ffn_down",
    )(blk_e, nvalid, act, w_down)


def _hier_moe(x, norm_g, w_group, b_group, w_router, b_router, w_gate, w_up, w_down, li,
              next_norm_g, tm=MOE_TM, tc=128):
    n_tok, d = x.shape
    pad = LANES - N_GROUPS - N_EXPERTS
    w_rt = jnp.concatenate([w_group, w_router, jnp.zeros((d, pad), F32)], axis=1)
    logits = _router(x, norm_g, w_rt)

    glog = logits[:, :N_GROUPS] + b_group.astype(F32)
    gsel = jnp.argmax(glog, axis=-1)
    pg = jnp.take_along_axis(jax.nn.softmax(glog, axis=-1), gsel[:, None], axis=1)[:, 0]
    elog = (logits[:, N_GROUPS:N_GROUPS + N_EXPERTS] + b_router.astype(F32)).reshape(
        n_tok, N_GROUPS, EXPERTS_PER_GROUP)
    elog = jnp.take_along_axis(elog, gsel[:, None, None], axis=1)[:, 0]
    top_v, top_j = lax.top_k(elog, TOP_K)
    weights = pg[:, None] * jax.nn.softmax(top_v, axis=-1)
    eid = (gsel[:, None] * EXPERTS_PER_GROUP + top_j).astype(jnp.int32)

    n_as = n_tok * TOP_K
    flat_e = eid.reshape(n_as)
    ids = jnp.arange(n_as, dtype=jnp.int32)
    sorted_e, order = lax.sort((flat_e, ids), num_keys=1)
    _, inv = lax.sort((order, ids), num_keys=1)
    experts = jnp.arange(N_EXPERTS, dtype=jnp.int32)
    start = jnp.searchsorted(sorted_e, experts, side='left').astype(jnp.int32)
    end = jnp.searchsorted(sorted_e, experts, side='right').astype(jnp.int32)
    counts = end - start
    pcounts = (counts + tm - 1) // tm * tm
    pend = jnp.cumsum(pcounts)
    pstart = pend - pcounts
    nb = -(-n_as // tm) + N_EXPERTS
    p_rows = nb * tm
    blk_e = jnp.minimum(jnp.searchsorted(pend, jnp.arange(nb, dtype=jnp.int32) * tm, side='right'),
                        N_EXPERTS - 1).astype(jnp.int32)
    nvalid = (pend[-1:] // tm).astype(jnp.int32)
    prow = jnp.arange(p_rows, dtype=jnp.int32)
    row_e = jnp.repeat(blk_e, tm)
    within = prow - pstart[row_e]
    is_real = within < counts[row_e]
    src = jnp.where(is_real, order[jnp.clip(start[row_e] + within, 0, n_as - 1)], 0)
    buf_tok = src // TOP_K
    dest = (pstart[flat_e] + inv - start[flat_e]).reshape(n_tok, TOP_K)

    xg = _gather_norm(x, norm_g, buf_tok, nvalid, tm)
    yb = _expert_ffn(xg, blk_e, nvalid, w_gate, w_up, w_down, li, tm)
    dest_t = dest.reshape(n_tok // tc, tc, TOP_K).transpose(0, 2, 1).reshape(n_tok // tc, 1, TOP_K * tc)
    w_exp = jnp.repeat(weights, LANES, axis=1)
    return _combine(x, w_exp, yb, dest_t, next_norm_g, tm=tc)


def _ab_layer(x, h, w_in, li, conv_w, conv_b, wa, ba, wi, bi, lam, bf, qn, kn, w_out, bsz):
    n_tok, d = x.shape
    t = n_tok // bsz
    d_rnn = conv_w.shape[1]
    nh = bf.shape[0]
    fw = nh * HEAD_DIM
    zga = _proj_f32(h, w_in, li, 0, 2 * d_rnn).reshape(bsz, t, 2 * d_rnn)
    ya = _rglru(zga, bsz, conv_w, conv_b, wa, ba, wi, bi, lam)
    normw = jnp.concatenate([jnp.tile(qn, nh), jnp.tile(kn * FOX_QK_SCALE, nh),
                             jnp.ones((fw,), F32)]).reshape(1, 3 * fw)
    qkv = _proj_heads(h, w_in, li, 2 * d_rnn, 3 * fw, normw, 2 * fw, bsz, 1)
    w_f = jnp.pad(w_in[li, :, 2 * d_rnn + 3 * fw:], ((0, 0), (0, LANES - nh)))
    b_f = jnp.pad(bf.astype(F32), (0, LANES - nh)).reshape(1, LANES)
    cum = _forget_cum(h, w_f, b_f, bsz)
    fox_tq = 512
    cum_row = cum[:, :, :nh].transpose(0, 2, 1).reshape(bsz, nh, t // fox_tq, fox_tq)
    yb = _fox_attention(qkv, cum, cum_row, tq=fox_tq)
    return _proj_res([ya.reshape(n_tok, d_rnn), yb.reshape(n_tok, fw)], w_out, li, x)


def _dilated_layer(x, h, w_in, li, qn, kn, table, w_out, bsz):
    n_tok, d = x.shape
    t = n_tok // bsz
    nh = table.shape[1] // len(DSW_PATTERNS)
    fw = nh * HEAD_DIM
    os_, lses = [], []
    for g, (window, dil) in enumerate(DSW_PATTERNS):
        normw = jnp.concatenate([jnp.tile(qn[g], nh), jnp.tile(kn[g], nh),
                                 jnp.ones((fw,), F32)]).reshape(1, 3 * fw)
        qkv = _proj_heads(h, w_in, li, g * 3 * fw, 3 * fw, normw, 2 * fw, bsz, dil)
        bias = _dsw_bias(table[:, g * nh:(g + 1) * nh], window, dil)
        o, lse = _dsw_group(qkv, bias, min(t // dil, 1024))
        os_.append(o)
        lses.append(lse)
    y = _dsw_merge(os_, lses)
    return _proj_res([y], w_out, li, x)


def kernel(x, mix_norm, ffn_norm, w_in_ab, conv_w, conv_b, lru_wa, lru_ba, lru_wi, lru_bi,
           lru_lambda, fox_bf, fox_qnorm, fox_knorm, w_out_ab, w_in_c, dsw_qnorm, dsw_knorm,
           w_out_c, rel_bias, moe_w_group, moe_b_group, moe_w_router, moe_b_router,
           moe_w_gate, moe_w_up, moe_w_down):
    bsz, t, d = x.shape
    depth = mix_norm.shape[0]
    xf = x.reshape(bsz * t, d)
    h = _rmsnorm(xf, mix_norm[0])
    for layer in range(depth):
        i = layer // 2
        if layer % 2 == 0:
            xf = _ab_layer(xf, h, w_in_ab, i, conv_w[i], conv_b[i], lru_wa[i],
                           lru_ba[i], lru_wi[i], lru_bi[i], lru_lambda[i], fox_bf[i],
                           fox_qnorm[i], fox_knorm[i], w_out_ab, bsz)
        else:
            xf = _dilated_layer(xf, h, w_in_c, i, dsw_qnorm[i], dsw_knorm[i],
                                rel_bias, w_out_c, bsz)
        next_g = mix_norm[layer + 1] if layer + 1 < depth else None
        xf, h = _hier_moe(xf, ffn_norm[layer], moe_w_group[layer], moe_b_group[layer],
                          moe_w_router[layer], moe_b_router[layer], moe_w_gate, moe_w_up,
                          moe_w_down, layer, next_g)
    return xf.reshape(bsz, t, d)
```

```python
import functools
import math

import jax
import jax.numpy as jnp
from jax import lax
from jax.experimental import pallas as pl
from jax.experimental.pallas import tpu as pltpu

F32 = jnp.float32
BF16 = jnp.bfloat16

LANES = 128
SUBLANES = 8
VMEM_LIMIT = 56 * 1024 * 1024

HEAD_DIM = 128
RMS_EPS = 1e-6
RNN_BLOCKS = 16
CONV_WIDTH = 4
LRU_C = 8.0
DSW_PATTERNS = ((128, 1), (512, 4), (2048, 16))
DSW_BLOCK = 128
DSW_CHAINS = 8
N_BUCKETS = 32
MAX_DISTANCE = 2048
N_GROUPS = 4
EXPERTS_PER_GROUP = 8
N_EXPERTS = N_GROUPS * EXPERTS_PER_GROUP
TOP_K = 2
LOG2E = math.log2(math.e)

MOE_TM = 256


def _params(*sem):
    return pltpu.CompilerParams(dimension_semantics=sem, vmem_limit_bytes=VMEM_LIMIT)


def _rmsnorm_kernel(x_ref, g_ref, o_ref):
    x = x_ref[...]
    ms = jnp.mean(x * x, axis=-1, keepdims=True)
    o_ref[...] = (x * lax.rsqrt(ms + RMS_EPS) * g_ref[...]).astype(o_ref.dtype)


def _rmsnorm(x, g, tm=256):
    m, d = x.shape
    return pl.pallas_call(
        _rmsnorm_kernel,
        grid=(m // tm,),
        in_specs=[pl.BlockSpec((tm, d), lambda i: (i, 0)),
                  pl.BlockSpec((1, d), lambda i: (0, 0))],
        out_specs=pl.BlockSpec((tm, d), lambda i: (i, 0)),
        out_shape=jax.ShapeDtypeStruct((m, d), BF16),
        compiler_params=_params("parallel"),
        name="rmsnorm",
    )(x, g.reshape(1, d))


def _proj_f32_kernel(a_ref, w_ref, o_ref):
    o_ref[...] = jnp.dot(a_ref[...], w_ref[...].astype(BF16), preferred_element_type=F32)


def _proj_f32(a, w, li, col0, ncols, tm=1024, tn=512):
    m, k = a.shape
    jb = col0 // tn
    return pl.pallas_call(
        _proj_f32_kernel,
        grid=(m // tm, ncols // tn),
        in_specs=[pl.BlockSpec((tm, k), lambda i, j: (i, 0)),
                  pl.BlockSpec((None, k, tn), lambda i, j: (li, 0, jb + j))],
        out_specs=pl.BlockSpec((tm, tn), lambda i, j: (i, j)),
        out_shape=jax.ShapeDtypeStruct((m, ncols), F32),
        compiler_params=_params("parallel", "arbitrary"),
        name="proj_f32",
    )(a, w)


def _proj_res_kernel(*refs, n_in):
    a_refs, w_refs = refs[:n_in], refs[n_in:2 * n_in]
    r_ref, o_ref = refs[2 * n_in], refs[2 * n_in + 1]
    acc = r_ref[...]
    for a_ref, w_ref in zip(a_refs, w_refs):
        acc = acc + jnp.dot(a_ref[...], w_ref[...].astype(BF16), preferred_element_type=F32)
    o_ref[...] = acc


def _proj_res(a_list, w, li, res, tm=1024, tn=512):
    m, k = a_list[0].shape
    n_in = len(a_list)
    n = w.shape[2]
    a_specs = [pl.BlockSpec((tm, k), lambda i, j: (i, 0)) for _ in a_list]
    w_specs = [pl.BlockSpec((None, k, tn), lambda i, j, p=p: (li, p, j)) for p in range(n_in)]
    return pl.pallas_call(
        functools.partial(_proj_res_kernel, n_in=n_in),
        grid=(m // tm, n // tn),
        in_specs=a_specs + w_specs + [pl.BlockSpec((tm, tn), lambda i, j: (i, j))],
        out_specs=pl.BlockSpec((tm, tn), lambda i, j: (i, j)),
        out_shape=jax.ShapeDtypeStruct((m, n), F32),
        compiler_params=_params("parallel", "arbitrary"),
        name="proj_res",
    )(*a_list, *([w] * n_in), res)


def _proj_heads_kernel(a_ref, w_ref, nw_ref, o_ref, y_scr, *, dil, n_norm_blocks, hpb, tm):
    j = pl.program_id(2)
    acc = jnp.dot(a_ref[...], w_ref[...].astype(BF16), preferred_element_type=F32)
    for hh in range(hpb):
        y_scr[hh] = acc[:, hh * HEAD_DIM:(hh + 1) * HEAD_DIM]

    @pl.when(j < n_norm_blocks)
    def _():
        for hh in range(hpb):
            y = y_scr[hh]
            ms = jnp.mean(y * y, axis=-1, keepdims=True)
            y_scr[hh] = y * lax.rsqrt(ms + RMS_EPS) * nw_ref[:, hh * HEAD_DIM:(hh + 1) * HEAD_DIM]

    rows = tm // dil
    for z in range(dil):
        for hh in range(hpb):
            if dil == 1:
                o_ref[0, z, hh] = y_scr[hh].astype(o_ref.dtype)
            else:
                o_ref[0, z, hh] = y_scr[hh, pl.ds(z, rows, stride=dil), :].astype(o_ref.dtype)


def _proj_heads(a, w, li, col0, ncols, normw, n_norm_cols, bsz, dil, tm=1024, tn=512):
    m, k = a.shape
    t = m // bsz
    nt = t // tm
    hpb = tn // HEAD_DIM
    jb = col0 // tn
    kern = functools.partial(_proj_heads_kernel, dil=dil, n_norm_blocks=n_norm_cols // tn,
                             hpb=hpb, tm=tm)
    return pl.pallas_call(
        kern,
        grid=(bsz, nt, ncols // tn),
        in_specs=[pl.BlockSpec((tm, k), lambda b, i, j: (b * nt + i, 0)),
                  pl.BlockSpec((None, k, tn), lambda b, i, j: (li, 0, jb + j)),
                  pl.BlockSpec((1, tn), lambda b, i, j: (0, j))],
        out_specs=pl.BlockSpec((1, dil, hpb, tm // dil, HEAD_DIM),
                               lambda b, i, j: (b, 0, j, i, 0)),
        out_shape=jax.ShapeDtypeStruct((bsz, dil, ncols // HEAD_DIM, t // dil, HEAD_DIM), BF16),
        scratch_shapes=[pltpu.VMEM((hpb, tm, HEAD_DIM), F32)],
        compiler_params=_params("parallel", "parallel", "arbitrary"),
        name=f"proj_heads_d{dil}",
    )(a, w, normw)


def _forget_cum_kernel(a_ref, w_ref, b_ref, o_ref, carry_ref, *, tm, nh):
    i = pl.program_id(1)

    @pl.when(i == 0)
    def _():
        carry_ref[...] = jnp.zeros_like(carry_ref)

    lane = lax.broadcasted_iota(jnp.int32, (tm, LANES), 1)
    fz = jnp.dot(a_ref[...], w_ref[...].astype(BF16), preferred_element_type=F32)
    c = jax.nn.log_sigmoid(jnp.where(lane < nh, fz + b_ref[...], 0.0))
    row = lax.broadcasted_iota(jnp.int32, c.shape, 0)
    s = 1
    while s < tm:
        c = c + jnp.where(row >= s, pltpu.roll(c, s, 0), 0.0)
        s *= 2
    c = c + carry_ref[0:1, :]
    o_ref[0] = c
    carry_ref[...] = jnp.broadcast_to(c[tm - 1:tm, :], carry_ref.shape)


def _forget_cum(a, w, li, col0, nh, b_f, bsz, tm=512):
    m, k = a.shape
    t = m // bsz
    nt = t // tm
    jb = col0 // LANES
    return pl.pallas_call(
        functools.partial(_forget_cum_kernel, tm=tm, nh=nh),
        grid=(bsz, nt),
        in_specs=[pl.BlockSpec((tm, k), lambda b, i: (b * nt + i, 0)),
                  pl.BlockSpec((None, k, LANES), lambda b, i: (li, 0, jb)),
                  pl.BlockSpec((1, LANES), lambda b, i: (0, 0))],
        out_specs=pl.BlockSpec((1, tm, LANES), lambda b, i: (b, i, 0)),
        out_shape=jax.ShapeDtypeStruct((bsz, t, LANES), F32),
        scratch_shapes=[pltpu.VMEM((SUBLANES, LANES), F32)],
        compiler_params=_params("parallel", "arbitrary"),
        name="forget_cum",
    )(a, w, b_f)


def _gelu_tanh(x):
    return 0.5 * x * (1.0 + jnp.tanh(math.sqrt(2.0 / math.pi) * (x + 0.044715 * (x * x * x))))


def _rglru_kernel(g_ref, x_ref, cw_ref, cb_ref, wa_ref, ba_ref, wi_ref, bi_ref, lam_ref,
                  o_ref, xs_scr, a_scr, u_scr, h_scr, carry_scr, *, tc, nblk):
    ti = pl.program_id(2)
    halo = SUBLANES

    @pl.when(ti == 0)
    def _():
        xs_scr[0:halo, :] = jnp.zeros((halo, xs_scr.shape[1]), F32)
        carry_scr[...] = jnp.zeros_like(carry_scr)

    xs_scr[halo:halo + tc, :] = x_ref[0]
    xc = cb_ref[...] + cw_ref[CONV_WIDTH - 1:CONV_WIDTH, :] * xs_scr[halo:halo + tc, :]
    for j in range(CONV_WIDTH - 1):
        sh = CONV_WIDTH - 1 - j
        xc = xc + cw_ref[j:j + 1, :] * xs_scr[halo - sh:halo - sh + tc, :]
    xs_scr[0:halo, :] = xs_scr[tc:tc + halo, :]

    neg_sp = -LRU_C * jax.nn.softplus(-lam_ref[...])
    for n in range(nblk):
        cs = slice(n * HEAD_DIM, (n + 1) * HEAD_DIM)
        xb = xc[:, cs]
        xbb = xb.astype(BF16)
        r = jax.nn.sigmoid(jnp.dot(xbb, wa_ref[n].astype(BF16), preferred_element_type=F32)
                           + ba_ref[:, cs])
        ig = jax.nn.sigmoid(jnp.dot(xbb, wi_ref[n].astype(BF16), preferred_element_type=F32)
                            + bi_ref[:, cs])
        log_a = neg_sp[:, cs] * r
        a = jnp.exp(log_a)
        a_scr[:, cs] = a
        u_scr[:, cs] = jnp.sqrt(-jnp.tanh(log_a) * (a * a + 1.0)) * (ig * xb)

    row = lax.broadcasted_iota(jnp.int32, (SUBLANES, a_scr.shape[1]), 0)

    def body(g, carry):
        r0 = pl.multiple_of(g * SUBLANES, SUBLANES)
        a = a_scr[pl.ds(r0, SUBLANES), :]
        u = u_scr[pl.ds(r0, SUBLANES), :]
        s = 1
        while s < SUBLANES:
            keep = row >= s
            u = jnp.where(keep, a * pltpu.roll(u, s, 0) + u, u)
            a = jnp.where(keep, a * pltpu.roll(a, s, 0), a)
            s *= 2
        h = a * carry + u
        h_scr[pl.ds(r0, SUBLANES), :] = h
        return jnp.broadcast_to(h[SUBLANES - 1:SUBLANES, :], h.shape)

    carry = lax.fori_loop(0, tc // SUBLANES, body, carry_scr[...])
    carry_scr[...] = carry
    o_ref[0] = (h_scr[...] * _gelu_tanh(g_ref[0])).astype(o_ref.dtype)


def _rglru(zga, bsz, conv_w, conv_b, wa, ba, wi, bi, lam, tc=256, cw=512):
    _, t, c2 = zga.shape
    c = c2 // 2
    nblk = cw // HEAD_DIM
    ncb = c // cw
    vec = lambda v: v.reshape(1, c)
    vspec = pl.BlockSpec((1, cw), lambda b, j, i: (0, j))
    kern = functools.partial(_rglru_kernel, tc=tc, nblk=nblk)
    return pl.pallas_call(
        kern,
        grid=(bsz, ncb, t // tc),
        in_specs=[pl.BlockSpec((1, tc, cw), lambda b, j, i: (b, i, j)),
                  pl.BlockSpec((1, tc, cw), lambda b, j, i: (b, i, ncb + j)),
                  pl.BlockSpec((CONV_WIDTH, cw), lambda b, j, i: (0, j)),
                  vspec,
                  pl.BlockSpec((nblk, HEAD_DIM, HEAD_DIM), lambda b, j, i: (j, 0, 0)),
                  vspec,
                  pl.BlockSpec((nblk, HEAD_DIM, HEAD_DIM), lambda b, j, i: (j, 0, 0)),
                  vspec, vspec],
        out_specs=pl.BlockSpec((1, tc, cw), lambda b, j, i: (b, i, j)),
        out_shape=jax.ShapeDtypeStruct((bsz, t, c), BF16),
        scratch_shapes=[pltpu.VMEM((tc + SUBLANES, cw), F32),
                        pltpu.VMEM((tc, cw), F32),
                        pltpu.VMEM((tc, cw), F32),
                        pltpu.VMEM((tc, cw), F32),
                        pltpu.VMEM((SUBLANES, cw), F32)],
        compiler_params=_params("parallel", "parallel", "arbitrary"),
        name="rglru",
    )(zga, zga, conv_w, vec(conv_b), wa, vec(ba), wi, vec(bi), vec(lam))


FOX_STRIP = 128
FOX_QK_SCALE = LOG2E / math.sqrt(HEAD_DIM)


def _fox_kernel(q_ref, k_ref, v_ref, cq_ref, ck_ref, o_ref, m_scr, l_scr, acc_scr, cq_scr, *, tq, hp):
    g = pl.program_id(1)
    qi = pl.program_id(2)
    tk = tq
    sr = FOX_STRIP
    lane = lax.broadcasted_iota(jnp.int32, (tq, LANES), 1)
    for hh in range(hp):
        cq = jnp.sum(jnp.where(lane == g * hp + hh, cq_ref[0], 0.0), axis=-1, keepdims=True)
        cq_scr[hh] = jnp.broadcast_to(cq * LOG2E, (tq, LANES))
        m_scr[hh] = jnp.full(m_scr.shape[1:], -jnp.inf, F32)
        l_scr[hh] = jnp.zeros(l_scr.shape[1:], F32)
        acc_scr[hh] = jnp.zeros(acc_scr.shape[1:], F32)
    tri = (lax.broadcasted_iota(jnp.int32, (sr, LANES), 1)
           <= lax.broadcasted_iota(jnp.int32, (sr, LANES), 0))

    def scores(hh, r, j, diagonal):
        rs = slice(r * sr, (r + 1) * sr)
        k0 = pl.multiple_of(j * tk, tk)
        nk = (r + 1) * sr if diagonal else tk
        return lax.dot_general(q_ref[0, 0, hh, rs, :], k_ref[0, 0, hh, pl.ds(k0, nk), :],
                               (((1,), (1,)), ((), ())), preferred_element_type=F32)

    def softmax_pv(s, hh, r, j, ck, diagonal):
        rs = slice(r * sr, (r + 1) * sr)
        k0 = pl.multiple_of(j * tk, tk)
        nk = s.shape[1]
        cq = cq_scr[hh, rs, :]
        parts = []
        for c in range(nk // LANES):
            cs = slice(c * LANES, (c + 1) * LANES)
            part = s[:, cs] + (cq - ck[:, cs])
            if diagonal and c == r:
                part = jnp.where(tri, part, -jnp.inf)
            parts.append(part)
        m_prev = m_scr[hh, rs, :]
        m_new = jnp.maximum(m_prev, jnp.max(functools.reduce(jnp.maximum, parts),
                                            axis=-1, keepdims=True))
        alpha = jnp.exp2(m_prev - m_new)
        ps = [jnp.exp2(part - m_new) for part in parts]
        row_sum = jnp.sum(functools.reduce(lambda a, b: a + b, ps), axis=-1, keepdims=True)
        l_scr[hh, rs, :] = alpha * l_scr[hh, rs, :] + row_sum
        p = jnp.concatenate(ps, axis=1).astype(BF16)
        acc_scr[hh, rs, :] = alpha * acc_scr[hh, rs, :] + jnp.dot(
            p, v_ref[0, 0, hh, pl.ds(k0, nk), :], preferred_element_type=F32)
        m_scr[hh, rs, :] = m_new

    def kv_step(j, diagonal):
        cks = [ck_ref[0, hh, pl.ds(j, 1), :] * LOG2E for hh in range(hp)]
        strips = [(hh, r) for hh in range(hp) for r in range(tq // sr)]
        s_next = scores(*strips[0], j, diagonal)
        for n, (hh, r) in enumerate(strips):
            s_cur = s_next
            if n + 1 < len(strips):
                s_next = scores(*strips[n + 1], j, diagonal)
            softmax_pv(s_cur, hh, r, j, cks[hh], diagonal)

    def below_diagonal(j, carry):
        kv_step(j, False)
        return carry

    lax.fori_loop(0, qi, below_diagonal, 0)
    kv_step(qi, True)
    for hh in range(hp):
        o_ref[0, :, hh * HEAD_DIM:(hh + 1) * HEAD_DIM] = (acc_scr[hh] / l_scr[hh]).astype(o_ref.dtype)


def _fox_attention(qkv, cum_col, cum_row, tq=512, hp=2):
    bsz, _, h3, t, dh = qkv.shape
    nh = h3 // 3
    nq = t // tq
    ng = nh // hp
    kern = functools.partial(_fox_kernel, tq=tq, hp=hp)
    return pl.pallas_call(
        kern,
        grid=(bsz, ng, nq),
        in_specs=[pl.BlockSpec((1, 1, hp, tq, dh), lambda b, g, i: (b, 0, g, i, 0)),
                  pl.BlockSpec((1, 1, hp, t, dh), lambda b, g, i: (b, 0, ng + g, 0, 0)),
                  pl.BlockSpec((1, 1, hp, t, dh), lambda b, g, i: (b, 0, 2 * ng + g, 0, 0)),
                  pl.BlockSpec((1, tq, LANES), lambda b, g, i: (b, i, 0)),
                  pl.BlockSpec((1, hp, nq, tq), lambda b, g, i: (b, g, 0, 0))],
        out_specs=pl.BlockSpec((1, tq, hp * dh), lambda b, g, i: (b, i, g)),
        out_shape=jax.ShapeDtypeStruct((bsz, t, nh * dh), BF16),
        scratch_shapes=[pltpu.VMEM((hp, tq, LANES), F32), pltpu.VMEM((hp, tq, LANES), F32),
                        pltpu.VMEM((hp, tq, dh), F32), pltpu.VMEM((hp, tq, LANES), F32)],
        compiler_params=_params("parallel", "parallel", "arbitrary"),
        name="fox_attention",
    )(qkv, qkv, qkv, cum_col, cum_row)


def _dsw_kernel(q_ref, kc_ref, kp_ref, vc_ref, vp_ref, bias_ref, o_ref, lse_ref, *, lc, nh):
    c = pl.program_id(2)
    blk = DSW_BLOCK
    nb = lc // blk
    lane = lax.broadcasted_iota(jnp.int32, (blk, LANES), 1)
    col = lax.broadcasted_iota(jnp.int32, (blk, 2 * blk), 1)
    scale = 1.0 / math.sqrt(HEAD_DIM)
    lse_ref[...] = jnp.zeros_like(lse_ref)

    def scores(h, n):
        rs = slice(n * blk, (n + 1) * blk)
        if n == 0:
            kprev = kp_ref[0, 0, h]
        else:
            kprev = kc_ref[0, 0, h, (n - 1) * blk:n * blk, :]
        kk = jnp.concatenate([kprev, kc_ref[0, 0, h, rs, :]], axis=0)
        return lax.dot_general(q_ref[0, 0, h, rs, :], kk, (((1,), (1,)), ((), ())),
                               preferred_element_type=F32)

    def softmax_pv(s, h, n):
        rs = slice(n * blk, (n + 1) * blk)
        if n == 0:
            vprev = vp_ref[0, 0, h]
        else:
            vprev = vc_ref[0, 0, h, (n - 1) * blk:n * blk, :]
        vv = jnp.concatenate([vprev, vc_ref[0, 0, h, rs, :]], axis=0)
        s = s * scale + bias_ref[h]
        if n == 0:
            s = jnp.where((c > 0) | (col >= blk), s, -jnp.inf)
        m = jnp.max(s, axis=-1, keepdims=True)
        e = jnp.exp(s - m)
        den = jnp.sum(e, axis=-1, keepdims=True)
        o = jnp.dot(e.astype(BF16), vv, preferred_element_type=F32) / den
        o_ref[0, 0, rs, pl.ds(pl.multiple_of(h * HEAD_DIM, HEAD_DIM), HEAD_DIM)] = (
            o.astype(o_ref.dtype))
        lse = m + jnp.log(den)
        lse_ref[0, 0, rs, :] = jnp.where(lane == h, lse, lse_ref[0, 0, rs, :])

    hu = math.gcd(nh, max(1, DSW_CHAINS // nb))

    def heads(i, carry):
        blocks = [(i * hu + hh, n) for hh in range(hu) for n in range(nb)]
        s_next = scores(*blocks[0])
        for k, (h, n) in enumerate(blocks):
            s_cur = s_next
            if k + 1 < len(blocks):
                s_next = scores(*blocks[k + 1])
            softmax_pv(s_cur, h, n)
        return carry

    lax.fori_loop(0, nh // hu, heads, 0)


def _dsw_group(qkv, bias, lc):
    bsz, dil, h3, l, dh = qkv.shape
    nh = h3 // 3
    nc = l // lc
    bpc = lc // DSW_BLOCK
    kern = functools.partial(_dsw_kernel, lc=lc, nh=nh)
    cur = lambda off: pl.BlockSpec((1, 1, nh, lc, dh), lambda b, z, c: (b, z, off, c, 0))
    prev = lambda off: pl.BlockSpec((1, 1, nh, DSW_BLOCK, dh),
                                    lambda b, z, c: (b, z, off, jnp.maximum(c * bpc - 1, 0), 0))
    return pl.pallas_call(
        kern,
        grid=(bsz, dil, nc),
        in_specs=[cur(0), cur(1), prev(1), cur(2), prev(2),
                  pl.BlockSpec((nh, DSW_BLOCK, 2 * DSW_BLOCK), lambda b, z, c: (0, 0, 0))],
        out_specs=[pl.BlockSpec((1, 1, lc, nh * dh), lambda b, z, c: (b, z, c, 0)),
                   pl.BlockSpec((1, 1, lc, LANES), lambda b, z, c: (b, z, c, 0))],
        out_shape=[jax.ShapeDtypeStruct((bsz, dil, l, nh * dh), BF16),
                   jax.ShapeDtypeStruct((bsz, dil, l, LANES), F32)],
        compiler_params=_params("parallel", "parallel", "arbitrary"),
        name=f"dsw_attention_d{dil}",
    )(qkv, qkv, qkv, qkv, qkv, bias)


def _dsw_merge_kernel(*refs, dils, tm, nh):
    ng = len(dils)
    o_refs = refs[:ng]
    l_refs = refs[ng:2 * ng]
    out_ref = refs[2 * ng]
    o_scr = refs[2 * ng + 1:2 * ng + 1 + ng]
    l_scr = refs[2 * ng + 1 + ng:]
    for g, dil in enumerate(dils):
        rows = tm // dil
        for z in range(dil):
            rsel = slice(None) if dil == 1 else pl.ds(z, rows, stride=dil)
            l_scr[g][rsel, :] = l_refs[g][0, z]
            for h in range(nh):
                o_scr[g][h, rsel, :] = o_refs[g][0, z, :, h * HEAD_DIM:(h + 1) * HEAD_DIM].astype(F32)
    lses = [l_scr[g][...] for g in range(ng)]
    m = functools.reduce(jnp.maximum, lses)
    ws = [jnp.exp(l - m) for l in lses]
    inv = 1.0 / functools.reduce(lambda a, b: a + b, ws)
    for h in range(nh):
        acc = None
        for g in range(ng):
            alpha = (ws[g] * inv)[:, h:h + 1]
            term = alpha * o_scr[g][h]
            acc = term if acc is None else acc + term
        out_ref[:, h * HEAD_DIM:(h + 1) * HEAD_DIM] = acc.astype(out_ref.dtype)


def _dsw_merge(os_, lses, tm=512):
    bsz, _, t, w = os_[0].shape
    nh = w // HEAD_DIM
    dils = tuple(o.shape[1] for o in os_)
    t = t * dils[0]
    nt = t // tm
    kern = functools.partial(_dsw_merge_kernel, dils=dils, tm=tm, nh=nh)
    o_specs = [pl.BlockSpec((1, d, tm // d, w), lambda b, i: (b, 0, i, 0)) for d in dils]
    l_specs = [pl.BlockSpec((1, d, tm // d, LANES), lambda b, i: (b, 0, i, 0)) for d in dils]
    return pl.pallas_call(
        kern,
        grid=(bsz, nt),
        in_specs=o_specs + l_specs,
        out_specs=pl.BlockSpec((tm, w), lambda b, i: (b * nt + i, 0)),
        out_shape=jax.ShapeDtypeStruct((bsz * t, w), BF16),
        scratch_shapes=([pltpu.VMEM((nh, tm, HEAD_DIM), F32) for _ in dils]
                        + [pltpu.VMEM((tm, LANES), F32) for _ in dils]),
        compiler_params=_params("parallel", "parallel"),
        name="dsw_merge",
    )(*os_, *lses)


def _t5_bucket(dist):
    max_exact = N_BUCKETS // 2
    large = max_exact + (jnp.log(jnp.maximum(dist, 1).astype(F32) / max_exact)
                         / math.log(MAX_DISTANCE / max_exact) * (N_BUCKETS - max_exact)).astype(jnp.int32)
    return jnp.where(dist < max_exact, dist, jnp.minimum(large, N_BUCKETS - 1))


def _dsw_bias(table, window, dil):
    span = window // dil
    qi = jnp.arange(DSW_BLOCK)[:, None]
    kj = jnp.arange(2 * DSW_BLOCK)[None, :]
    delta = qi + DSW_BLOCK - kj
    band = (delta >= 0) & (delta <= span)
    bucket = _t5_bucket(jnp.maximum(delta, 0) * dil)
    onehot = (bucket[None, :, :] == jnp.arange(N_BUCKETS)[:, None, None])
    table_t = table.astype(F32).T
    bias = jnp.sum(jnp.where(onehot[None], table_t[:, :, None, None], 0.0), axis=1)
    return jnp.where(band[None], bias, -jnp.inf)


def _router_kernel(x_ref, g_ref, w_ref, lg_ref):
    x = x_ref[...]
    ms = jnp.mean(x * x, axis=-1, keepdims=True)
    h = x * lax.rsqrt(ms + RMS_EPS) * g_ref[...]
    lg_ref[...] = jnp.dot(h, w_ref[...], preferred_element_type=F32,
                          precision=lax.Precision.HIGHEST)


def _router(x, g, w_rt, tm=256):
    m, d = x.shape
    return pl.pallas_call(
        _router_kernel,
        grid=(m // tm,),
        in_specs=[pl.BlockSpec((tm, d), lambda i: (i, 0)),
                  pl.BlockSpec((1, d), lambda i: (0, 0)),
                  pl.BlockSpec((d, LANES), lambda i: (0, 0))],
        out_specs=pl.BlockSpec((tm, LANES), lambda i: (i, 0)),
        out_shape=jax.ShapeDtypeStruct((m, LANES), F32),
        compiler_params=_params("parallel"),
        name="moe_router",
    )(x, g.reshape(1, d), w_rt)


TOKEN_PITCH = 40


def _issue_row_gather(idx_ref, src_hbm, dst_ref, sem, n_rows):
    def issue(i, carry):
        row = idx_ref[0, 0, i]
        pltpu.make_async_copy(src_hbm.at[pl.ds(row, 1)], dst_ref.at[pl.ds(i, 1)], sem).start()
        return carry
    lax.fori_loop(0, n_rows, issue, 0, unroll=8)


def _wait_row_gather(src_hbm, dst_ref, sem, n_rows):
    pltpu.make_async_copy(src_hbm.at[pl.ds(0, n_rows)], dst_ref, sem).wait()


def _gather_norm_kernel(nv_ref, idx_ref, idx_next_ref, x_hbm, g_ref, o_ref, xs_scr, sems, *, tm, d):
    r = pl.program_id(0)
    nv = nv_ref[0]
    slot = r % 2
    nc = d // LANES
    slot_rows = tm * TOKEN_PITCH

    def issue(idx, s):
        def one(i, carry):
            tok = idx[0, 0, i]
            row0 = pl.multiple_of(s * slot_rows + i * TOKEN_PITCH, SUBLANES)
            pltpu.make_async_copy(x_hbm.at[tok], xs_scr.at[pl.ds(row0, nc)], sems.at[s]).start()
            return carry
        lax.fori_loop(0, tm, one, 0, unroll=8)

    @pl.when(r == 0)
    def _():
        issue(idx_ref, 0)

    @pl.when(r + 1 < nv)
    def _():
        issue(idx_next_ref, 1 - slot)

    @pl.when(r < nv)
    def _():
        done = xs_scr.at[pl.ds(0, tm * nc)]
        pltpu.make_async_copy(done, done, sems.at[slot]).wait()
        base = pl.multiple_of(slot * slot_rows, SUBLANES)
        col = lambda c: xs_scr[pl.ds(base + c, tm, stride=TOKEN_PITCH), :]
        ssq = jnp.zeros((tm, LANES), F32)
        for c in range(nc):
            xc = col(c)
            ssq = ssq + xc * xc
        inv = lax.rsqrt(jnp.sum(ssq, axis=-1, keepdims=True) / d + RMS_EPS)
        for c in range(nc):
            cs = slice(c * LANES, (c + 1) * LANES)
            o_ref[:, cs] = (col(c) * inv * g_ref[:, cs]).astype(o_ref.dtype)

    @pl.when(r >= nv)
    def _():
        o_ref[...] = jnp.zeros_like(o_ref)


def _gather_norm(x, g, buf_tok, nvalid, tm):
    n, d = x.shape
    nb = buf_tok.shape[0] // tm
    idx = buf_tok.reshape(nb, 1, tm)
    x3 = x.reshape(n, d // LANES, LANES)
    last = lambda r, nv: jnp.minimum(r, nv[0] - 1)
    return pl.pallas_call(
        functools.partial(_gather_norm_kernel, tm=tm, d=d),
        grid_spec=pltpu.PrefetchScalarGridSpec(
            num_scalar_prefetch=1,
            grid=(nb,),
            in_specs=[pl.BlockSpec((1, 1, tm), lambda r, nv: (last(r, nv), 0, 0),
                                   memory_space=pltpu.SMEM),
                      pl.BlockSpec((1, 1, tm), lambda r, nv: (last(r + 1, nv), 0, 0),
                                   memory_space=pltpu.SMEM),
                      pl.BlockSpec(memory_space=pl.ANY),
                      pl.BlockSpec((1, d), lambda r, nv: (0, 0))],
            out_specs=pl.BlockSpec((tm, d), lambda r, nv: (r, 0)),
            scratch_shapes=[pltpu.VMEM((2 * tm * TOKEN_PITCH, LANES), F32),
                            pltpu.SemaphoreType.DMA((2,))],
        ),
        out_shape=jax.ShapeDtypeStruct((nb * tm, d), BF16),
        compiler_params=_params("arbitrary"),
        name="moe_gather_norm",
    )(nvalid, idx, idx, x3, g.reshape(1, d))


def _combine_kernel(idx_ref, idx_next_ref, x_ref, w_ref, y_hbm, *rest, tm, with_norm):
    if with_norm:
        g_ref, o_ref, h_ref, ys_scr, sems = rest
    else:
        o_ref, ys_scr, sems = rest
    r = pl.program_id(0)
    nr = pl.num_programs(0)
    slot = r % 2
    n_rows = TOP_K * tm

    @pl.when(r == 0)
    def _():
        _issue_row_gather(idx_ref, y_hbm, ys_scr.at[0], sems.at[0], n_rows)

    @pl.when(r + 1 < nr)
    def _():
        _issue_row_gather(idx_next_ref, y_hbm, ys_scr.at[1 - slot], sems.at[1 - slot], n_rows)

    _wait_row_gather(y_hbm, ys_scr.at[slot], sems.at[slot], n_rows)
    w0 = w_ref[:, 0:LANES]
    w1 = w_ref[:, LANES:2 * LANES]
    d = x_ref.shape[1]
    ssq = jnp.zeros((tm, LANES), F32)
    for c in range(d // LANES):
        cs = slice(c * LANES, (c + 1) * LANES)
        o = x_ref[:, cs] + (w0 * ys_scr[slot, 0:tm, cs] + w1 * ys_scr[slot, tm:2 * tm, cs])
        o_ref[:, cs] = o
        if with_norm:
            ssq = ssq + o * o
    if with_norm:
        inv = lax.rsqrt(jnp.sum(ssq, axis=-1, keepdims=True) / d + RMS_EPS)
        h_ref[...] = (o_ref[...] * inv * g_ref[...]).astype(h_ref.dtype)


def _combine(x, w_exp, yb, dest, next_norm_g, tm=128):
    n, d = x.shape
    nr = n // tm
    with_norm = next_norm_g is not None
    in_specs = [pl.BlockSpec((1, 1, TOP_K * tm), lambda r: (r, 0, 0), memory_space=pltpu.SMEM),
                pl.BlockSpec((1, 1, TOP_K * tm), lambda r: (jnp.minimum(r + 1, nr - 1), 0, 0),
                             memory_space=pltpu.SMEM),
                pl.BlockSpec((tm, d), lambda r: (r, 0)),
                pl.BlockSpec((tm, 2 * LANES), lambda r: (r, 0)),
                pl.BlockSpec(memory_space=pl.ANY)]
    args = [dest, dest, x, w_exp, yb]
    out_specs = [pl.BlockSpec((tm, d), lambda r: (r, 0))]
    out_shape = [jax.ShapeDtypeStruct((n, d), F32)]
    if with_norm:
        in_specs.append(pl.BlockSpec((1, d), lambda r: (0, 0)))
        args.append(next_norm_g.reshape(1, d))
        out_specs.append(pl.BlockSpec((tm, d), lambda r: (r, 0)))
        out_shape.append(jax.ShapeDtypeStruct((n, d), BF16))
    res = pl.pallas_call(
        functools.partial(_combine_kernel, tm=tm, with_norm=with_norm),
        grid=(nr,),
        in_specs=in_specs,
        out_specs=out_specs,
        out_shape=out_shape,
        scratch_shapes=[pltpu.VMEM((2, TOP_K * tm, d), F32), pltpu.SemaphoreType.DMA((2,))],
        compiler_params=_params("arbitrary"),
        name="moe_combine",
    )(*args)
    return (res[0], res[1]) if with_norm else (res[0], None)


def _ffn_up_kernel(be_ref, nv_ref, first_ref, run_ref, nxt_ref, nruns_ref, x_ref, wg_hbm, wu_hbm,
                   o_ref, stage, w_scr, sems, *, li, tf, nf):
    j = pl.program_id(0)
    r = pl.program_id(1)
    valid = r < nv_ref[0]
    cols = lambda jj: pl.ds(pl.multiple_of(jj * tf, LANES), tf)

    def copies(e, jj, s):
        return (pltpu.make_async_copy(wg_hbm.at[li, e, :, cols(jj)], stage.at[s, 0], sems.at[s]),
                pltpu.make_async_copy(wu_hbm.at[li, e, :, cols(jj)], stage.at[s, 1], sems.at[s]))

    @pl.when((j == 0) & (r == 0))
    def _():
        for cp in copies(be_ref[0], 0, 0):
            cp.start()

    @pl.when(valid & (first_ref[r] == 1))
    def _():
        s = (j * nruns_ref[0] + run_ref[r]) % 2
        nxt = nxt_ref[r]

        @pl.when(nxt >= 0)
        def _():
            for cp in copies(nxt, j, 1 - s):
                cp.start()

        @pl.when((nxt < 0) & (j + 1 < nf))
        def _():
            for cp in copies(be_ref[0], j + 1, 1 - s):
                cp.start()

        for cp in copies(be_ref[r], j, s):
            cp.wait()
        w_scr[:, 0:tf] = stage[s, 0].astype(BF16)
        w_scr[:, tf:2 * tf] = stage[s, 1].astype(BF16)

    @pl.when(valid)
    def _():
        gu = jnp.dot(x_ref[...], w_scr[...], preferred_element_type=F32)
        o_ref[...] = (jax.nn.silu(gu[:, 0:tf]) * gu[:, tf:2 * tf]).astype(o_ref.dtype)

    @pl.when(jnp.logical_not(valid))
    def _():
        o_ref[...] = jnp.zeros_like(o_ref)


def _ffn_down_kernel(be_ref, nv_ref, first_ref, run_ref, nxt_ref, a_ref, wd_hbm, o_ref,
                     stage, w_scr, sems, *, li):
    r = pl.program_id(0)
    valid = r < nv_ref[0]
    copy = lambda e, s: pltpu.make_async_copy(wd_hbm.at[li, e], stage.at[s], sems.at[s])

    @pl.when(r == 0)
    def _():
        copy(be_ref[0], 0).start()

    @pl.when(valid & (first_ref[r] == 1))
    def _():
        s = run_ref[r] % 2
        nxt = nxt_ref[r]

        @pl.when(nxt >= 0)
        def _():
            copy(nxt, 1 - s).start()

        copy(be_ref[r], s).wait()
        w_scr[...] = stage[s].astype(BF16)

    @pl.when(valid)
    def _():
        o_ref[...] = jnp.dot(a_ref[...], w_scr[...], preferred_element_type=F32)

    @pl.when(jnp.logical_not(valid))
    def _():
        o_ref[...] = jnp.zeros_like(o_ref)


def _expert_ffn(xg, blk_e, nvalid, w_gate, w_up, w_down, li, tm, tf=384):
    p, d = xg.shape
    f = w_gate.shape[3]
    nb = p // tm
    nf = f // tf
    tiles = jnp.arange(nb, dtype=jnp.int32)
    valid = tiles < nvalid[0]
    first = valid & ((tiles == 0) | (blk_e != jnp.roll(blk_e, 1)))
    run = (jnp.cumsum(first.astype(jnp.int32)) - 1).astype(jnp.int32)
    nruns = jnp.sum(first.astype(jnp.int32)).reshape(1)
    run_e = jnp.sum(jnp.where(first[None, :] & (run[None, :] == tiles[:, None]), blk_e[None, :], 0),
                    axis=1)
    nxt = jnp.sum(jnp.where(tiles[None, :] == run[:, None] + 1, run_e[None, :], 0), axis=1)
    nxt = jnp.where(run + 1 < nruns[0], nxt, -1).astype(jnp.int32)
    first = first.astype(jnp.int32)
    last = lambda r, nv: jnp.minimum(r, nv[0] - 1)
    act = pl.pallas_call(
        functools.partial(_ffn_up_kernel, li=li, tf=tf, nf=nf),
        grid_spec=pltpu.PrefetchScalarGridSpec(
            num_scalar_prefetch=6,
            grid=(nf, nb),
            in_specs=[pl.BlockSpec((tm, d), lambda j, r, be, nv, *_: (last(r, nv), 0)),
                      pl.BlockSpec(memory_space=pl.ANY),
                      pl.BlockSpec(memory_space=pl.ANY)],
            out_specs=pl.BlockSpec((tm, tf), lambda j, r, *_: (r, j)),
            scratch_shapes=[pltpu.VMEM((2, 2, d, tf), F32), pltpu.VMEM((d, 2 * tf), BF16),
                            pltpu.SemaphoreType.DMA((2,))],
        ),
        out_shape=jax.ShapeDtypeStruct((p, f), BF16),
        compiler_params=_params("arbitrary", "arbitrary"),
        name="moe_ffn_up",
    )(blk_e, nvalid, first, run, nxt, nruns, xg, w_gate, w_up)
    return pl.pallas_call(
        functools.partial(_ffn_down_kernel, li=li),
        grid_spec=pltpu.PrefetchScalarGridSpec(
            num_scalar_prefetch=5,
            grid=(nb,),
            in_specs=[pl.BlockSpec((tm, f), lambda r, be, nv, *_: (last(r, nv), 0)),
                      pl.BlockSpec(memory_space=pl.ANY)],
            out_specs=pl.BlockSpec((tm, d), lambda r, *_: (r, 0)),
            scratch_shapes=[pltpu.VMEM((2, f, d), F32), pltpu.VMEM((f, d), BF16),
                            pltpu.SemaphoreType.DMA((2,))],
        ),
        out_shape=jax.ShapeDtypeStruct((p, d), F32),
        compiler_params=_params("arbitrary"),
        name="moe_ffn_down",
    )(blk_e, nvalid, first, run, nxt, act, w_down)


def _hier_moe(x, norm_g, w_group, b_group, w_router, b_router, w_gate, w_up, w_down, li,
              next_norm_g, tm=MOE_TM, tc=128):
    n_tok, d = x.shape
    pad = LANES - N_GROUPS - N_EXPERTS
    w_rt = jnp.concatenate([w_group, w_router, jnp.zeros((d, pad), F32)], axis=1)
    logits = _router(x, norm_g, w_rt)

    glog = logits[:, :N_GROUPS] + b_group.astype(F32)
    gsel = jnp.argmax(glog, axis=-1)
    pg = jnp.take_along_axis(jax.nn.softmax(glog, axis=-1), gsel[:, None], axis=1)[:, 0]
    elog = (logits[:, N_GROUPS:N_GROUPS + N_EXPERTS] + b_router.astype(F32)).reshape(
        n_tok, N_GROUPS, EXPERTS_PER_GROUP)
    elog = jnp.take_along_axis(elog, gsel[:, None, None], axis=1)[:, 0]
    top_v, top_j = lax.top_k(elog, TOP_K)
    weights = pg[:, None] * jax.nn.softmax(top_v, axis=-1)
    eid = (gsel[:, None] * EXPERTS_PER_GROUP + top_j).astype(jnp.int32)

    n_as = n_tok * TOP_K
    flat_e = eid.reshape(n_as)
    ids = jnp.arange(n_as, dtype=jnp.int32)
    sorted_e, order = lax.sort((flat_e, ids), num_keys=1)
    _, inv = lax.sort((order, ids), num_keys=1)
    experts = jnp.arange(N_EXPERTS, dtype=jnp.int32)
    start = jnp.searchsorted(sorted_e, experts, side='left').astype(jnp.int32)
    end = jnp.searchsorted(sorted_e, experts, side='right').astype(jnp.int32)
    counts = end - start
    pcounts = (counts + tm - 1) // tm * tm
    pend = jnp.cumsum(pcounts)
    pstart = pend - pcounts
    nb = -(-n_as // tm) + N_EXPERTS
    p_rows = nb * tm
    blk_e = jnp.minimum(jnp.searchsorted(pend, jnp.arange(nb, dtype=jnp.int32) * tm, side='right'),
                        N_EXPERTS - 1).astype(jnp.int32)
    nvalid = (pend[-1:] // tm).astype(jnp.int32)
    prow = jnp.arange(p_rows, dtype=jnp.int32)
    row_e = jnp.repeat(blk_e, tm)
    within = prow - pstart[row_e]
    is_real = within < counts[row_e]
    src = jnp.where(is_real, order[jnp.clip(start[row_e] + within, 0, n_as - 1)], 0)
    buf_tok = src // TOP_K
    dest = (pstart[flat_e] + inv - start[flat_e]).reshape(n_tok, TOP_K)

    xg = _gather_norm(x, norm_g, buf_tok, nvalid, tm)
    yb = _expert_ffn(xg, blk_e, nvalid, w_gate, w_up, w_down, li, tm)
    dest_t = dest.reshape(n_tok // tc, tc, TOP_K).transpose(0, 2, 1).reshape(n_tok // tc, 1, TOP_K * tc)
    w_exp = jnp.repeat(weights, LANES, axis=1)
    return _combine(x, w_exp, yb, dest_t, next_norm_g, tm=tc)


def _ab_layer(x, h, w_in, li, conv_w, conv_b, wa, ba, wi, bi, lam, bf, qn, kn, w_out, bsz):
    n_tok, d = x.shape
    t = n_tok // bsz
    d_rnn = conv_w.shape[1]
    nh = bf.shape[0]
    fw = nh * HEAD_DIM
    zga = _proj_f32(h, w_in, li, 0, 2 * d_rnn).reshape(bsz, t, 2 * d_rnn)
    ya = _rglru(zga, bsz, conv_w, conv_b, wa, ba, wi, bi, lam)
    normw = jnp.concatenate([jnp.tile(qn, nh), jnp.tile(kn * FOX_QK_SCALE, nh),
                             jnp.ones((fw,), F32)]).reshape(1, 3 * fw)
    qkv = _proj_heads(h, w_in, li, 2 * d_rnn, 3 * fw, normw, 2 * fw, bsz, 1)
    b_f = jnp.pad(bf.astype(F32), (0, LANES - nh)).reshape(1, LANES)
    cum = _forget_cum(h, w_in, li, 2 * d_rnn + 3 * fw, nh, b_f, bsz)
    fox_tq = 512
    cum_row = cum[:, :, :nh].transpose(0, 2, 1).reshape(bsz, nh, t // fox_tq, fox_tq)
    yb = _fox_attention(qkv, cum, cum_row, tq=fox_tq)
    return _proj_res([ya.reshape(n_tok, d_rnn), yb.reshape(n_tok, fw)], w_out, li, x)


def _dilated_layer(x, h, w_in, li, qn, kn, table, w_out, bsz):
    n_tok, d = x.shape
    t = n_tok // bsz
    nh = table.shape[1] // len(DSW_PATTERNS)
    fw = nh * HEAD_DIM
    os_, lses = [], []
    for g, (window, dil) in enumerate(DSW_PATTERNS):
        normw = jnp.concatenate([jnp.tile(qn[g], nh), jnp.tile(kn[g], nh),
                                 jnp.ones((fw,), F32)]).reshape(1, 3 * fw)
        qkv = _proj_heads(h, w_in, li, g * 3 * fw, 3 * fw, normw, 2 * fw, bsz, dil)
        bias = _dsw_bias(table[:, g * nh:(g + 1) * nh], window, dil)
        o, lse = _dsw_group(qkv, bias, min(t // dil, 1024))
        os_.append(o)
        lses.append(lse)
    y = _dsw_merge(os_, lses)
    return _proj_res([y], w_out, li, x)


def kernel(x, mix_norm, ffn_norm, w_in_ab, conv_w, conv_b, lru_wa, lru_ba, lru_wi, lru_bi,
           lru_lambda, fox_bf, fox_qnorm, fox_knorm, w_out_ab, w_in_c, dsw_qnorm, dsw_knorm,
           w_out_c, rel_bias, moe_w_group, moe_b_group, moe_w_router, moe_b_router,
           moe_w_gate, moe_w_up, moe_w_down):
    bsz, t, d = x.shape
    depth = mix_norm.shape[0]
    xf = x.reshape(bsz * t, d)
    h = _rmsnorm(xf, mix_norm[0])
    for layer in range(depth):
        i = layer // 2
        if layer % 2 == 0:
            xf = _ab_layer(xf, h, w_in_ab, i, conv_w[i], conv_b[i], lru_wa[i],
                           lru_ba[i], lru_wi[i], lru_bi[i], lru_lambda[i], fox_bf[i],
                           fox_qnorm[i], fox_knorm[i], w_out_ab, bsz)
        else:
            xf = _dilated_layer(xf, h, w_in_c, i, dsw_qnorm[i], dsw_knorm[i],
                                rel_bias, w_out_c, bsz)
        next_g = mix_norm[layer + 1] if layer + 1 < depth else None
        xf, h = _hier_moe(xf, ffn_norm[layer], moe_w_group[layer], moe_b_group[layer],
                          moe_w_router[layer], moe_b_router[layer], moe_w_gate, moe_w_up,
                          moe_w_down, layer, next_g)
    return xf.reshape(bsz, t, d)
```

```python
import functools
import math

import jax
import jax.numpy as jnp
from jax import lax
from jax.experimental import pallas as pl
from jax.experimental.pallas import tpu as pltpu

F32 = jnp.float32
BF16 = jnp.bfloat16

LANES = 128
SUBLANES = 8
VMEM_LIMIT = 56 * 1024 * 1024

HEAD_DIM = 128
RMS_EPS = 1e-6
RNN_BLOCKS = 16
CONV_WIDTH = 4
LRU_C = 8.0
DSW_PATTERNS = ((128, 1), (512, 4), (2048, 16))
DSW_BLOCK = 128
DSW_CHAINS = 8
N_BUCKETS = 32
MAX_DISTANCE = 2048
N_GROUPS = 4
EXPERTS_PER_GROUP = 8
N_EXPERTS = N_GROUPS * EXPERTS_PER_GROUP
TOP_K = 2
LOG2E = math.log2(math.e)

MOE_TM = 256


def _params(*sem):
    return pltpu.CompilerParams(dimension_semantics=sem, vmem_limit_bytes=VMEM_LIMIT)


def _rmsnorm_kernel(x_ref, g_ref, o_ref):
    x = x_ref[...]
    ms = jnp.mean(x * x, axis=-1, keepdims=True)
    o_ref[...] = (x * lax.rsqrt(ms + RMS_EPS) * g_ref[...]).astype(o_ref.dtype)


def _rmsnorm(x, g, tm=256):
    m, d = x.shape
    return pl.pallas_call(
        _rmsnorm_kernel,
        grid=(m // tm,),
        in_specs=[pl.BlockSpec((tm, d), lambda i: (i, 0)),
                  pl.BlockSpec((1, d), lambda i: (0, 0))],
        out_specs=pl.BlockSpec((tm, d), lambda i: (i, 0)),
        out_shape=jax.ShapeDtypeStruct((m, d), BF16),
        compiler_params=_params("parallel"),
        name="rmsnorm",
    )(x, g.reshape(1, d))


def _mm(a, w_blk, wt):
    w = w_blk.astype(BF16)
    if wt:
        return lax.dot_general(a, w, (((1,), (1,)), ((), ())), preferred_element_type=F32)
    return jnp.dot(a, w, preferred_element_type=F32)


def _w_spec(k, tn, li, jb, wt, grid_rank):
    if wt:
        return pl.BlockSpec((None, tn, k), lambda *g: (li, jb + g[grid_rank - 1], 0))
    return pl.BlockSpec((None, k, tn), lambda *g: (li, 0, jb + g[grid_rank - 1]))


def _proj_f32_kernel(a_ref, w_ref, o_ref, *, wt):
    o_ref[...] = _mm(a_ref[...], w_ref[...], wt)


def _proj_f32(a, w, li, col0, ncols, wt=False, tm=1024, tn=512):
    m, k = a.shape
    jb = col0 // tn
    return pl.pallas_call(
        functools.partial(_proj_f32_kernel, wt=wt),
        grid=(m // tm, ncols // tn),
        in_specs=[pl.BlockSpec((tm, k), lambda i, j: (i, 0)),
                  _w_spec(k, tn, li, jb, wt, 2)],
        out_specs=pl.BlockSpec((tm, tn), lambda i, j: (i, j)),
        out_shape=jax.ShapeDtypeStruct((m, ncols), F32),
        compiler_params=_params("parallel", "arbitrary"),
        name="proj_f32",
    )(a, w)


def _proj_res_kernel(*refs, n_in):
    a_refs, w_refs = refs[:n_in], refs[n_in:2 * n_in]
    r_ref, o_ref = refs[2 * n_in], refs[2 * n_in + 1]
    acc = r_ref[...]
    for a_ref, w_ref in zip(a_refs, w_refs):
        acc = acc + jnp.dot(a_ref[...], w_ref[...].astype(BF16), preferred_element_type=F32)
    o_ref[...] = acc


def _proj_res(a_list, w, li, res, tm=1024, tn=512):
    m, k = a_list[0].shape
    n_in = len(a_list)
    n = w.shape[2]
    a_specs = [pl.BlockSpec((tm, k), lambda i, j: (i, 0)) for _ in a_list]
    w_specs = [pl.BlockSpec((None, k, tn), lambda i, j, p=p: (li, p, j)) for p in range(n_in)]
    return pl.pallas_call(
        functools.partial(_proj_res_kernel, n_in=n_in),
        grid=(m // tm, n // tn),
        in_specs=a_specs + w_specs + [pl.BlockSpec((tm, tn), lambda i, j: (i, j))],
        out_specs=pl.BlockSpec((tm, tn), lambda i, j: (i, j)),
        out_shape=jax.ShapeDtypeStruct((m, n), F32),
        compiler_params=_params("parallel", "arbitrary"),
        name="proj_res",
    )(*a_list, *([w] * n_in), res)


def _proj_heads_kernel(a_ref, w_ref, nw_ref, o_ref, y_scr, *, dil, n_norm_blocks, hpb, tm, wt):
    j = pl.program_id(2)
    acc = _mm(a_ref[...], w_ref[...], wt)
    for hh in range(hpb):
        y_scr[hh] = acc[:, hh * HEAD_DIM:(hh + 1) * HEAD_DIM]

    @pl.when(j < n_norm_blocks)
    def _():
        for hh in range(hpb):
            y = y_scr[hh]
            ms = jnp.mean(y * y, axis=-1, keepdims=True)
            y_scr[hh] = y * lax.rsqrt(ms + RMS_EPS) * nw_ref[:, hh * HEAD_DIM:(hh + 1) * HEAD_DIM]

    rows = tm // dil
    for z in range(dil):
        for hh in range(hpb):
            if dil == 1:
                o_ref[0, z, hh] = y_scr[hh].astype(o_ref.dtype)
            else:
                o_ref[0, z, hh] = y_scr[hh, pl.ds(z, rows, stride=dil), :].astype(o_ref.dtype)


def _proj_heads(a, w, li, col0, ncols, normw, n_norm_cols, bsz, dil, wt=False, tm=1024, tn=512):
    m, k = a.shape
    t = m // bsz
    nt = t // tm
    hpb = tn // HEAD_DIM
    jb = col0 // tn
    kern = functools.partial(_proj_heads_kernel, dil=dil, n_norm_blocks=n_norm_cols // tn,
                             hpb=hpb, tm=tm, wt=wt)
    return pl.pallas_call(
        kern,
        grid=(bsz, nt, ncols // tn),
        in_specs=[pl.BlockSpec((tm, k), lambda b, i, j: (b * nt + i, 0)),
                  _w_spec(k, tn, li, jb, wt, 3),
                  pl.BlockSpec((1, tn), lambda b, i, j: (0, j))],
        out_specs=pl.BlockSpec((1, dil, hpb, tm // dil, HEAD_DIM),
                               lambda b, i, j: (b, 0, j, i, 0)),
        out_shape=jax.ShapeDtypeStruct((bsz, dil, ncols // HEAD_DIM, t // dil, HEAD_DIM), BF16),
        scratch_shapes=[pltpu.VMEM((hpb, tm, HEAD_DIM), F32)],
        compiler_params=_params("parallel", "parallel", "arbitrary"),
        name=f"proj_heads_d{dil}",
    )(a, w, normw)


def _forget_cum_kernel(a_ref, w_ref, b_ref, o_ref, carry_ref, *, tm, nh, wt):
    i = pl.program_id(1)

    @pl.when(i == 0)
    def _():
        carry_ref[...] = jnp.zeros_like(carry_ref)

    lane = lax.broadcasted_iota(jnp.int32, (tm, LANES), 1)
    fz = _mm(a_ref[...], w_ref[...], wt)
    c = jax.nn.log_sigmoid(jnp.where(lane < nh, fz + b_ref[...], 0.0))
    row = lax.broadcasted_iota(jnp.int32, c.shape, 0)
    s = 1
    while s < tm:
        c = c + jnp.where(row >= s, pltpu.roll(c, s, 0), 0.0)
        s *= 2
    c = c + carry_ref[0:1, :]
    o_ref[0] = c
    carry_ref[...] = jnp.broadcast_to(c[tm - 1:tm, :], carry_ref.shape)


def _forget_cum(a, w, li, col0, nh, b_f, bsz, wt=False, tm=512):
    m, k = a.shape
    t = m // bsz
    nt = t // tm
    jb = col0 // LANES
    return pl.pallas_call(
        functools.partial(_forget_cum_kernel, tm=tm, nh=nh, wt=wt),
        grid=(bsz, nt),
        in_specs=[pl.BlockSpec((tm, k), lambda b, i: (b * nt + i, 0)),
                  (pl.BlockSpec((None, LANES, k), lambda b, i: (li, jb, 0)) if wt
                   else pl.BlockSpec((None, k, LANES), lambda b, i: (li, 0, jb))),
                  pl.BlockSpec((1, LANES), lambda b, i: (0, 0))],
        out_specs=pl.BlockSpec((1, tm, LANES), lambda b, i: (b, i, 0)),
        out_shape=jax.ShapeDtypeStruct((bsz, t, LANES), F32),
        scratch_shapes=[pltpu.VMEM((SUBLANES, LANES), F32)],
        compiler_params=_params("parallel", "arbitrary"),
        name="forget_cum",
    )(a, w, b_f)


def _gelu_tanh(x):
    return 0.5 * x * (1.0 + jnp.tanh(math.sqrt(2.0 / math.pi) * (x + 0.044715 * (x * x * x))))


def _rglru_kernel(g_ref, x_ref, cw_ref, cb_ref, wa_ref, ba_ref, wi_ref, bi_ref, lam_ref,
                  o_ref, xs_scr, a_scr, u_scr, h_scr, carry_scr, *, tc, nblk):
    ti = pl.program_id(2)
    halo = SUBLANES

    @pl.when(ti == 0)
    def _():
        xs_scr[0:halo, :] = jnp.zeros((halo, xs_scr.shape[1]), F32)
        carry_scr[...] = jnp.zeros_like(carry_scr)

    xs_scr[halo:halo + tc, :] = x_ref[0]
    xc = cb_ref[...] + cw_ref[CONV_WIDTH - 1:CONV_WIDTH, :] * xs_scr[halo:halo + tc, :]
    for j in range(CONV_WIDTH - 1):
        sh = CONV_WIDTH - 1 - j
        xc = xc + cw_ref[j:j + 1, :] * xs_scr[halo - sh:halo - sh + tc, :]
    xs_scr[0:halo, :] = xs_scr[tc:tc + halo, :]

    neg_sp = -LRU_C * jax.nn.softplus(-lam_ref[...])
    for n in range(nblk):
        cs = slice(n * HEAD_DIM, (n + 1) * HEAD_DIM)
        xb = xc[:, cs]
        xbb = xb.astype(BF16)
        r = jax.nn.sigmoid(jnp.dot(xbb, wa_ref[n].astype(BF16), preferred_element_type=F32)
                           + ba_ref[:, cs])
        ig = jax.nn.sigmoid(jnp.dot(xbb, wi_ref[n].astype(BF16), preferred_element_type=F32)
                            + bi_ref[:, cs])
        log_a = neg_sp[:, cs] * r
        a = jnp.exp(log_a)
        a_scr[:, cs] = a
        u_scr[:, cs] = jnp.sqrt(-jnp.tanh(log_a) * (a * a + 1.0)) * (ig * xb)

    row = lax.broadcasted_iota(jnp.int32, (SUBLANES, a_scr.shape[1]), 0)

    def body(g, carry):
        r0 = pl.multiple_of(g * SUBLANES, SUBLANES)
        a = a_scr[pl.ds(r0, SUBLANES), :]
        u = u_scr[pl.ds(r0, SUBLANES), :]
        s = 1
        while s < SUBLANES:
            keep = row >= s
            u = jnp.where(keep, a * pltpu.roll(u, s, 0) + u, u)
            a = jnp.where(keep, a * pltpu.roll(a, s, 0), a)
            s *= 2
        h = a * carry + u
        h_scr[pl.ds(r0, SUBLANES), :] = h
        return jnp.broadcast_to(h[SUBLANES - 1:SUBLANES, :], h.shape)

    carry = lax.fori_loop(0, tc // SUBLANES, body, carry_scr[...])
    carry_scr[...] = carry
    o_ref[0] = (h_scr[...] * _gelu_tanh(g_ref[0])).astype(o_ref.dtype)


def _rglru(zga, bsz, conv_w, conv_b, wa, ba, wi, bi, lam, tc=256, cw=512):
    _, t, c2 = zga.shape
    c = c2 // 2
    nblk = cw // HEAD_DIM
    ncb = c // cw
    vec = lambda v: v.reshape(1, c)
    vspec = pl.BlockSpec((1, cw), lambda b, j, i: (0, j))
    kern = functools.partial(_rglru_kernel, tc=tc, nblk=nblk)
    return pl.pallas_call(
        kern,
        grid=(bsz, ncb, t // tc),
        in_specs=[pl.BlockSpec((1, tc, cw), lambda b, j, i: (b, i, j)),
                  pl.BlockSpec((1, tc, cw), lambda b, j, i: (b, i, ncb + j)),
                  pl.BlockSpec((CONV_WIDTH, cw), lambda b, j, i: (0, j)),
                  vspec,
                  pl.BlockSpec((nblk, HEAD_DIM, HEAD_DIM), lambda b, j, i: (j, 0, 0)),
                  vspec,
                  pl.BlockSpec((nblk, HEAD_DIM, HEAD_DIM), lambda b, j, i: (j, 0, 0)),
                  vspec, vspec],
        out_specs=pl.BlockSpec((1, tc, cw), lambda b, j, i: (b, i, j)),
        out_shape=jax.ShapeDtypeStruct((bsz, t, c), BF16),
        scratch_shapes=[pltpu.VMEM((tc + SUBLANES, cw), F32),
                        pltpu.VMEM((tc, cw), F32),
                        pltpu.VMEM((tc, cw), F32),
                        pltpu.VMEM((tc, cw), F32),
                        pltpu.VMEM((SUBLANES, cw), F32)],
        compiler_params=_params("parallel", "parallel", "arbitrary"),
        name="rglru",
    )(zga, zga, conv_w, vec(conv_b), wa, vec(ba), wi, vec(bi), vec(lam))


FOX_STRIP = 128
FOX_QK_SCALE = LOG2E / math.sqrt(HEAD_DIM)


def _fox_kernel(q_ref, k_ref, v_ref, cq_ref, ck_ref, o_ref, m_scr, l_scr, acc_scr, cq_scr, *, tq, hp):
    g = pl.program_id(1)
    qi = pl.program_id(2)
    tk = tq
    sr = FOX_STRIP
    lane = lax.broadcasted_iota(jnp.int32, (tq, LANES), 1)
    for hh in range(hp):
        cq = jnp.sum(jnp.where(lane == g * hp + hh, cq_ref[0], 0.0), axis=-1, keepdims=True)
        cq_scr[hh] = jnp.broadcast_to(cq * LOG2E, (tq, LANES))
        m_scr[hh] = jnp.full(m_scr.shape[1:], -jnp.inf, F32)
        l_scr[hh] = jnp.zeros(l_scr.shape[1:], F32)
        acc_scr[hh] = jnp.zeros(acc_scr.shape[1:], F32)
    tri = (lax.broadcasted_iota(jnp.int32, (sr, LANES), 1)
           <= lax.broadcasted_iota(jnp.int32, (sr, LANES), 0))

    def scores(hh, r, j, diagonal):
        rs = slice(r * sr, (r + 1) * sr)
        k0 = pl.multiple_of(j * tk, tk)
        nk = (r + 1) * sr if diagonal else tk
        return lax.dot_general(q_ref[0, 0, hh, rs, :], k_ref[0, 0, hh, pl.ds(k0, nk), :],
                               (((1,), (1,)), ((), ())), preferred_element_type=F32)

    def softmax_pv(s, hh, r, j, ck, diagonal):
        rs = slice(r * sr, (r + 1) * sr)
        k0 = pl.multiple_of(j * tk, tk)
        nk = s.shape[1]
        cq = cq_scr[hh, rs, :]
        parts = []
        for c in range(nk // LANES):
            cs = slice(c * LANES, (c + 1) * LANES)
            part = s[:, cs] + (cq - ck[:, cs])
            if diagonal and c == r:
                part = jnp.where(tri, part, -jnp.inf)
            parts.append(part)
        m_prev = m_scr[hh, rs, :]
        m_new = jnp.maximum(m_prev, jnp.max(functools.reduce(jnp.maximum, parts),
                                            axis=-1, keepdims=True))
        alpha = jnp.exp2(m_prev - m_new)
        ps = [jnp.exp2(part - m_new) for part in parts]
        row_sum = jnp.sum(functools.reduce(lambda a, b: a + b, ps), axis=-1, keepdims=True)
        l_scr[hh, rs, :] = alpha * l_scr[hh, rs, :] + row_sum
        p = jnp.concatenate(ps, axis=1).astype(BF16)
        acc_scr[hh, rs, :] = alpha * acc_scr[hh, rs, :] + jnp.dot(
            p, v_ref[0, 0, hh, pl.ds(k0, nk), :], preferred_element_type=F32)
        m_scr[hh, rs, :] = m_new

    def kv_step(j, diagonal):
        cks = [ck_ref[0, hh, pl.ds(j, 1), :] * LOG2E for hh in range(hp)]
        strips = [(hh, r) for hh in range(hp) for r in range(tq // sr)]
        s_next = scores(*strips[0], j, diagonal)
        for n, (hh, r) in enumerate(strips):
            s_cur = s_next
            if n + 1 < len(strips):
                s_next = scores(*strips[n + 1], j, diagonal)
            softmax_pv(s_cur, hh, r, j, cks[hh], diagonal)

    def below_diagonal(j, carry):
        kv_step(j, False)
        return carry

    lax.fori_loop(0, qi, below_diagonal, 0)
    kv_step(qi, True)
    for hh in range(hp):
        o_ref[0, :, hh * HEAD_DIM:(hh + 1) * HEAD_DIM] = (acc_scr[hh] / l_scr[hh]).astype(o_ref.dtype)


def _fox_attention(qkv, cum_col, cum_row, tq=512, hp=2):
    bsz, _, h3, t, dh = qkv.shape
    nh = h3 // 3
    nq = t // tq
    ng = nh // hp
    kern = functools.partial(_fox_kernel, tq=tq, hp=hp)
    return pl.pallas_call(
        kern,
        grid=(bsz, ng, nq),
        in_specs=[pl.BlockSpec((1, 1, hp, tq, dh), lambda b, g, i: (b, 0, g, i, 0)),
                  pl.BlockSpec((1, 1, hp, t, dh), lambda b, g, i: (b, 0, ng + g, 0, 0)),
                  pl.BlockSpec((1, 1, hp, t, dh), lambda b, g, i: (b, 0, 2 * ng + g, 0, 0)),
                  pl.BlockSpec((1, tq, LANES), lambda b, g, i: (b, i, 0)),
                  pl.BlockSpec((1, hp, nq, tq), lambda b, g, i: (b, g, 0, 0))],
        out_specs=pl.BlockSpec((1, tq, hp * dh), lambda b, g, i: (b, i, g)),
        out_shape=jax.ShapeDtypeStruct((bsz, t, nh * dh), BF16),
        scratch_shapes=[pltpu.VMEM((hp, tq, LANES), F32), pltpu.VMEM((hp, tq, LANES), F32),
                        pltpu.VMEM((hp, tq, dh), F32), pltpu.VMEM((hp, tq, LANES), F32)],
        compiler_params=_params("parallel", "parallel", "arbitrary"),
        name="fox_attention",
    )(qkv, qkv, qkv, cum_col, cum_row)


def _dsw_kernel(q_ref, kc_ref, kp_ref, vc_ref, vp_ref, bias_ref, o_ref, lse_ref, *, lc, nh):
    c = pl.program_id(2)
    blk = DSW_BLOCK
    nb = lc // blk
    lane = lax.broadcasted_iota(jnp.int32, (blk, LANES), 1)
    col = lax.broadcasted_iota(jnp.int32, (blk, 2 * blk), 1)
    scale = 1.0 / math.sqrt(HEAD_DIM)
    lse_ref[...] = jnp.zeros_like(lse_ref)

    def scores(h, n):
        rs = slice(n * blk, (n + 1) * blk)
        if n == 0:
            kprev = kp_ref[0, 0, h]
        else:
            kprev = kc_ref[0, 0, h, (n - 1) * blk:n * blk, :]
        kk = jnp.concatenate([kprev, kc_ref[0, 0, h, rs, :]], axis=0)
        return lax.dot_general(q_ref[0, 0, h, rs, :], kk, (((1,), (1,)), ((), ())),
                               preferred_element_type=F32)

    def softmax_pv(s, h, n):
        rs = slice(n * blk, (n + 1) * blk)
        if n == 0:
            vprev = vp_ref[0, 0, h]
        else:
            vprev = vc_ref[0, 0, h, (n - 1) * blk:n * blk, :]
        vv = jnp.concatenate([vprev, vc_ref[0, 0, h, rs, :]], axis=0)
        s = s * scale + bias_ref[h]
        if n == 0:
            s = jnp.where((c > 0) | (col >= blk), s, -jnp.inf)
        m = jnp.max(s, axis=-1, keepdims=True)
        e = jnp.exp(s - m)
        den = jnp.sum(e, axis=-1, keepdims=True)
        o = jnp.dot(e.astype(BF16), vv, preferred_element_type=F32) / den
        o_ref[0, 0, rs, pl.ds(pl.multiple_of(h * HEAD_DIM, HEAD_DIM), HEAD_DIM)] = (
            o.astype(o_ref.dtype))
        lse = m + jnp.log(den)
        lse_ref[0, 0, rs, :] = jnp.where(lane == h, lse, lse_ref[0, 0, rs, :])

    hu = math.gcd(nh, max(1, DSW_CHAINS // nb))

    def heads(i, carry):
        blocks = [(i * hu + hh, n) for hh in range(hu) for n in range(nb)]
        s_next = scores(*blocks[0])
        for k, (h, n) in enumerate(blocks):
            s_cur = s_next
            if k + 1 < len(blocks):
                s_next = scores(*blocks[k + 1])
            softmax_pv(s_cur, h, n)
        return carry

    lax.fori_loop(0, nh // hu, heads, 0)


def _dsw_group(qkv, bias, lc):
    bsz, dil, h3, l, dh = qkv.shape
    nh = h3 // 3
    nc = l // lc
    bpc = lc // DSW_BLOCK
    kern = functools.partial(_dsw_kernel, lc=lc, nh=nh)
    cur = lambda off: pl.BlockSpec((1, 1, nh, lc, dh), lambda b, z, c: (b, z, off, c, 0))
    prev = lambda off: pl.BlockSpec((1, 1, nh, DSW_BLOCK, dh),
                                    lambda b, z, c: (b, z, off, jnp.maximum(c * bpc - 1, 0), 0))
    return pl.pallas_call(
        kern,
        grid=(bsz, dil, nc),
        in_specs=[cur(0), cur(1), prev(1), cur(2), prev(2),
                  pl.BlockSpec((nh, DSW_BLOCK, 2 * DSW_BLOCK), lambda b, z, c: (0, 0, 0))],
        out_specs=[pl.BlockSpec((1, 1, lc, nh * dh), lambda b, z, c: (b, z, c, 0)),
                   pl.BlockSpec((1, 1, lc, LANES), lambda b, z, c: (b, z, c, 0))],
        out_shape=[jax.ShapeDtypeStruct((bsz, dil, l, nh * dh), BF16),
                   jax.ShapeDtypeStruct((bsz, dil, l, LANES), F32)],
        compiler_params=_params("parallel", "parallel", "arbitrary"),
        name=f"dsw_attention_d{dil}",
    )(qkv, qkv, qkv, qkv, qkv, bias)


def _dsw_merge_kernel(*refs, dils, tm, nh):
    ng = len(dils)
    o_refs = refs[:ng]
    l_refs = refs[ng:2 * ng]
    out_ref = refs[2 * ng]
    o_scr = refs[2 * ng + 1:2 * ng + 1 + ng]
    l_scr = refs[2 * ng + 1 + ng:]
    for g, dil in enumerate(dils):
        rows = tm // dil
        for z in range(dil):
            rsel = slice(None) if dil == 1 else pl.ds(z, rows, stride=dil)
            l_scr[g][rsel, :] = l_refs[g][0, z]
            for h in range(nh):
                o_scr[g][h, rsel, :] = o_refs[g][0, z, :, h * HEAD_DIM:(h + 1) * HEAD_DIM].astype(F32)
    lses = [l_scr[g][...] for g in range(ng)]
    m = functools.reduce(jnp.maximum, lses)
    ws = [jnp.exp(l - m) for l in lses]
    inv = 1.0 / functools.reduce(lambda a, b: a + b, ws)
    for h in range(nh):
        acc = None
        for g in range(ng):
            alpha = (ws[g] * inv)[:, h:h + 1]
            term = alpha * o_scr[g][h]
            acc = term if acc is None else acc + term
        out_ref[:, h * HEAD_DIM:(h + 1) * HEAD_DIM] = acc.astype(out_ref.dtype)


def _dsw_merge(os_, lses, tm=512):
    bsz, _, t, w = os_[0].shape
    nh = w // HEAD_DIM
    dils = tuple(o.shape[1] for o in os_)
    t = t * dils[0]
    nt = t // tm
    kern = functools.partial(_dsw_merge_kernel, dils=dils, tm=tm, nh=nh)
    o_specs = [pl.BlockSpec((1, d, tm // d, w), lambda b, i: (b, 0, i, 0)) for d in dils]
    l_specs = [pl.BlockSpec((1, d, tm // d, LANES), lambda b, i: (b, 0, i, 0)) for d in dils]
    return pl.pallas_call(
        kern,
        grid=(bsz, nt),
        in_specs=o_specs + l_specs,
        out_specs=pl.BlockSpec((tm, w), lambda b, i: (b * nt + i, 0)),
        out_shape=jax.ShapeDtypeStruct((bsz * t, w), BF16),
        scratch_shapes=([pltpu.VMEM((nh, tm, HEAD_DIM), F32) for _ in dils]
                        + [pltpu.VMEM((tm, LANES), F32) for _ in dils]),
        compiler_params=_params("parallel", "parallel"),
        name="dsw_merge",
    )(*os_, *lses)


def _t5_bucket(dist):
    max_exact = N_BUCKETS // 2
    large = max_exact + (jnp.log(jnp.maximum(dist, 1).astype(F32) / max_exact)
                         / math.log(MAX_DISTANCE / max_exact) * (N_BUCKETS - max_exact)).astype(jnp.int32)
    return jnp.where(dist < max_exact, dist, jnp.minimum(large, N_BUCKETS - 1))


def _dsw_bias(table, window, dil):
    span = window // dil
    qi = jnp.arange(DSW_BLOCK)[:, None]
    kj = jnp.arange(2 * DSW_BLOCK)[None, :]
    delta = qi + DSW_BLOCK - kj
    band = (delta >= 0) & (delta <= span)
    bucket = _t5_bucket(jnp.maximum(delta, 0) * dil)
    onehot = (bucket[None, :, :] == jnp.arange(N_BUCKETS)[:, None, None])
    table_t = table.astype(F32).T
    bias = jnp.sum(jnp.where(onehot[None], table_t[:, :, None, None], 0.0), axis=1)
    return jnp.where(band[None], bias, -jnp.inf)


def _router_kernel(x_ref, g_ref, whi_ref, wlo_ref, b_ref, sel_ref, wts_ref):
    x = x_ref[...]
    tm = x.shape[0]
    ms = jnp.mean(x * x, axis=-1, keepdims=True)
    h = x * lax.rsqrt(ms + RMS_EPS) * g_ref[...]
    hi = h.astype(BF16)
    lo = (h - hi.astype(F32)).astype(BF16)
    logits = (jnp.dot(hi, whi_ref[...], preferred_element_type=F32)
              + (jnp.dot(hi, wlo_ref[...], preferred_element_type=F32)
                 + jnp.dot(lo, whi_ref[...], preferred_element_type=F32))) + b_ref[...]
    lane = lax.broadcasted_iota(jnp.int32, (tm, LANES), 1)
    first_max = lambda v, vmax: jnp.min(jnp.where(v == vmax, lane, LANES), axis=-1, keepdims=True)

    glog = jnp.where(lane < N_GROUPS, logits, -jnp.inf)
    gmax = jnp.max(glog, axis=-1, keepdims=True)
    gsel = first_max(glog, gmax)
    pg = 1.0 / jnp.sum(jnp.exp(glog - gmax), axis=-1, keepdims=True)

    ex = lane - N_GROUPS
    in_group = (ex >= 0) & (ex < N_EXPERTS) & ((ex // EXPERTS_PER_GROUP) == gsel)
    elog = jnp.where(in_group, logits, -jnp.inf)
    v1 = jnp.max(elog, axis=-1, keepdims=True)
    j1 = first_max(elog, v1)
    elog2 = jnp.where(lane == j1, -jnp.inf, elog)
    v2 = jnp.max(elog2, axis=-1, keepdims=True)
    j2 = first_max(elog2, v2)
    e2 = jnp.exp(v2 - v1)
    w1 = pg / (1.0 + e2)
    w2 = pg * e2 / (1.0 + e2)
    sel_ref[...] = jnp.where(lane == 0, j1 - N_GROUPS, jnp.where(lane == 1, j2 - N_GROUPS, 0))
    wts_ref[:, 0:LANES] = jnp.broadcast_to(w1, (tm, LANES))
    wts_ref[:, LANES:2 * LANES] = jnp.broadcast_to(w2, (tm, LANES))


def _router(x, g, w_group, b_group, w_router, b_router, tm=256):
    m, d = x.shape
    pad = LANES - N_GROUPS - N_EXPERTS
    w_rt = jnp.concatenate([w_group, w_router, jnp.zeros((d, pad), F32)], axis=1)
    w_hi = w_rt.astype(BF16)
    w_lo = (w_rt - w_hi.astype(F32)).astype(BF16)
    b_rt = jnp.concatenate([b_group.astype(F32), b_router.astype(F32), jnp.zeros((pad,), F32)])
    return pl.pallas_call(
        _router_kernel,
        grid=(m // tm,),
        in_specs=[pl.BlockSpec((tm, d), lambda i: (i, 0)),
                  pl.BlockSpec((1, d), lambda i: (0, 0)),
                  pl.BlockSpec((d, LANES), lambda i: (0, 0)),
                  pl.BlockSpec((d, LANES), lambda i: (0, 0)),
                  pl.BlockSpec((1, LANES), lambda i: (0, 0))],
        out_specs=[pl.BlockSpec((tm, LANES), lambda i: (i, 0)),
                   pl.BlockSpec((tm, 2 * LANES), lambda i: (i, 0))],
        out_shape=[jax.ShapeDtypeStruct((m, LANES), jnp.int32),
                   jax.ShapeDtypeStruct((m, 2 * LANES), F32)],
        compiler_params=_params("parallel"),
        name="moe_router",
    )(x, g.reshape(1, d), w_hi, w_lo, b_rt.reshape(1, LANES))


def _issue_row_gather(idx_ref, src_hbm, dst_ref, sem, n_rows):
    def issue(i, carry):
        row = idx_ref[0, 0, i]
        pltpu.make_async_copy(src_hbm.at[pl.ds(row, 1)], dst_ref.at[pl.ds(i, 1)], sem).start()
        return carry
    lax.fori_loop(0, n_rows, issue, 0, unroll=8)


def _wait_row_gather(src_hbm, dst_ref, sem, n_rows):
    pltpu.make_async_copy(src_hbm.at[pl.ds(0, n_rows)], dst_ref, sem).wait()


def _gather_norm_kernel(nv_ref, idx_ref, idx_next_ref, x_hbm, g_ref, o_ref, xs_scr, sems, *, tm, d):
    r = pl.program_id(0)
    nv = nv_ref[0]
    slot = r % 2
    nc = d // LANES

    def issue(idx, s):
        def one(i, carry):
            pltpu.make_async_copy(x_hbm.at[idx[0, 0, i]], xs_scr.at[s, :, i, :], sems.at[s]).start()
            return carry
        lax.fori_loop(0, tm, one, 0, unroll=8)

    @pl.when(r == 0)
    def _():
        issue(idx_ref, 0)

    @pl.when(r + 1 < nv)
    def _():
        issue(idx_next_ref, 1 - slot)

    @pl.when(r < nv)
    def _():
        pltpu.make_async_copy(xs_scr.at[slot], xs_scr.at[slot], sems.at[slot]).wait()
        ssq = jnp.zeros((tm, LANES), F32)
        for c in range(nc):
            xc = xs_scr[slot, c]
            ssq = ssq + xc * xc
        inv = lax.rsqrt(jnp.sum(ssq, axis=-1, keepdims=True) / d + RMS_EPS)
        for c in range(nc):
            cs = slice(c * LANES, (c + 1) * LANES)
            o_ref[:, cs] = (xs_scr[slot, c] * inv * g_ref[:, cs]).astype(o_ref.dtype)

    @pl.when(r >= nv)
    def _():
        o_ref[...] = jnp.zeros_like(o_ref)


def _gather_norm(x, g, buf_tok, nvalid, tm):
    n, d = x.shape
    nb = buf_tok.shape[0] // tm
    idx = buf_tok.reshape(nb, 1, tm)
    x3 = x.reshape(n, d // LANES, LANES)
    last = lambda r, nv: jnp.minimum(r, nv[0] - 1)
    return pl.pallas_call(
        functools.partial(_gather_norm_kernel, tm=tm, d=d),
        grid_spec=pltpu.PrefetchScalarGridSpec(
            num_scalar_prefetch=1,
            grid=(nb,),
            in_specs=[pl.BlockSpec((1, 1, tm), lambda r, nv: (last(r, nv), 0, 0),
                                   memory_space=pltpu.SMEM),
                      pl.BlockSpec((1, 1, tm), lambda r, nv: (last(r + 1, nv), 0, 0),
                                   memory_space=pltpu.SMEM),
                      pl.BlockSpec(memory_space=pl.ANY),
                      pl.BlockSpec((1, d), lambda r, nv: (0, 0))],
            out_specs=pl.BlockSpec((tm, d), lambda r, nv: (r, 0)),
            scratch_shapes=[pltpu.VMEM((2, d // LANES, tm, LANES), F32),
                            pltpu.SemaphoreType.DMA((2,))],
        ),
        out_shape=jax.ShapeDtypeStruct((nb * tm, d), BF16),
        compiler_params=_params("arbitrary"),
        name="moe_gather_norm",
    )(nvalid, idx, idx, x3, g.reshape(1, d))


def _combine_kernel(idx_ref, idx_next_ref, x_ref, w_ref, y_hbm, *rest, tm, with_norm):
    if with_norm:
        g_ref, o_ref, h_ref, ys_scr, sems = rest
    else:
        o_ref, ys_scr, sems = rest
    r = pl.program_id(0)
    nr = pl.num_programs(0)
    slot = r % 2
    n_rows = TOP_K * tm

    @pl.when(r == 0)
    def _():
        _issue_row_gather(idx_ref, y_hbm, ys_scr.at[0], sems.at[0], n_rows)

    @pl.when(r + 1 < nr)
    def _():
        _issue_row_gather(idx_next_ref, y_hbm, ys_scr.at[1 - slot], sems.at[1 - slot], n_rows)

    _wait_row_gather(y_hbm, ys_scr.at[slot], sems.at[slot], n_rows)
    w0 = w_ref[:, 0:LANES]
    w1 = w_ref[:, LANES:2 * LANES]
    d = x_ref.shape[1]
    ssq = jnp.zeros((tm, LANES), F32)
    for c in range(d // LANES):
        cs = slice(c * LANES, (c + 1) * LANES)
        o = x_ref[:, cs] + (w0 * ys_scr[slot, 0:tm, cs] + w1 * ys_scr[slot, tm:2 * tm, cs])
        o_ref[:, cs] = o
        if with_norm:
            ssq = ssq + o * o
    if with_norm:
        inv = lax.rsqrt(jnp.sum(ssq, axis=-1, keepdims=True) / d + RMS_EPS)
        h_ref[...] = (o_ref[...] * inv * g_ref[...]).astype(h_ref.dtype)


def _combine(x, w_exp, yb, dest, next_norm_g, tm=128):
    n, d = x.shape
    nr = n // tm
    with_norm = next_norm_g is not None
    in_specs = [pl.BlockSpec((1, 1, TOP_K * tm), lambda r: (r, 0, 0), memory_space=pltpu.SMEM),
                pl.BlockSpec((1, 1, TOP_K * tm), lambda r: (jnp.minimum(r + 1, nr - 1), 0, 0),
                             memory_space=pltpu.SMEM),
                pl.BlockSpec((tm, d), lambda r: (r, 0)),
                pl.BlockSpec((tm, 2 * LANES), lambda r: (r, 0)),
                pl.BlockSpec(memory_space=pl.ANY)]
    args = [dest, dest, x, w_exp, yb]
    out_specs = [pl.BlockSpec((tm, d), lambda r: (r, 0))]
    out_shape = [jax.ShapeDtypeStruct((n, d), F32)]
    if with_norm:
        in_specs.append(pl.BlockSpec((1, d), lambda r: (0, 0)))
        args.append(next_norm_g.reshape(1, d))
        out_specs.append(pl.BlockSpec((tm, d), lambda r: (r, 0)))
        out_shape.append(jax.ShapeDtypeStruct((n, d), BF16))
    res = pl.pallas_call(
        functools.partial(_combine_kernel, tm=tm, with_norm=with_norm),
        grid=(nr,),
        in_specs=in_specs,
        out_specs=out_specs,
        out_shape=out_shape,
        scratch_shapes=[pltpu.VMEM((2, TOP_K * tm, d), F32), pltpu.SemaphoreType.DMA((2,))],
        compiler_params=_params("arbitrary"),
        name="moe_combine",
    )(*args)
    return (res[0], res[1]) if with_norm else (res[0], None)


def _ffn_up_kernel(be_ref, nv_ref, first_ref, run_ref, nxt_ref, nruns_ref, x_ref, wg_hbm, wu_hbm,
                   o_ref, stage, w_scr, sems, *, li, tf, nf):
    j = pl.program_id(0)
    r = pl.program_id(1)
    valid = r < nv_ref[0]
    cols = lambda jj: pl.ds(pl.multiple_of(jj * tf, LANES), tf)

    def copies(e, jj, s):
        return (pltpu.make_async_copy(wg_hbm.at[li, e, :, cols(jj)], stage.at[s, 0], sems.at[s]),
                pltpu.make_async_copy(wu_hbm.at[li, e, :, cols(jj)], stage.at[s, 1], sems.at[s]))

    @pl.when((j == 0) & (r == 0))
    def _():
        for cp in copies(be_ref[0], 0, 0):
            cp.start()

    @pl.when(valid & (first_ref[r] == 1))
    def _():
        s = (j * nruns_ref[0] + run_ref[r]) % 2
        nxt = nxt_ref[r]

        @pl.when(nxt >= 0)
        def _():
            for cp in copies(nxt, j, 1 - s):
                cp.start()

        @pl.when((nxt < 0) & (j + 1 < nf))
        def _():
            for cp in copies(be_ref[0], j + 1, 1 - s):
                cp.start()

        for cp in copies(be_ref[r], j, s):
            cp.wait()
        w_scr[:, 0:tf] = stage[s, 0].astype(BF16)
        w_scr[:, tf:2 * tf] = stage[s, 1].astype(BF16)

    @pl.when(valid)
    def _():
        gu = jnp.dot(x_ref[...], w_scr[...], preferred_element_type=F32)
        o_ref[...] = (jax.nn.silu(gu[:, 0:tf]) * gu[:, tf:2 * tf]).astype(o_ref.dtype)

    @pl.when(jnp.logical_not(valid))
    def _():
        o_ref[...] = jnp.zeros_like(o_ref)


def _ffn_down_kernel(be_ref, nv_ref, first_ref, run_ref, nxt_ref, a_ref, wd_hbm, o_ref,
                     stage, w_scr, sems, *, li):
    r = pl.program_id(0)
    valid = r < nv_ref[0]
    copy = lambda e, s: pltpu.make_async_copy(wd_hbm.at[li, e], stage.at[s], sems.at[s])

    @pl.when(r == 0)
    def _():
        copy(be_ref[0], 0).start()

    @pl.when(valid & (first_ref[r] == 1))
    def _():
        s = run_ref[r] % 2
        nxt = nxt_ref[r]

        @pl.when(nxt >= 0)
        def _():
            copy(nxt, 1 - s).start()

        copy(be_ref[r], s).wait()
        w_scr[...] = stage[s].astype(BF16)

    @pl.when(valid)
    def _():
        o_ref[...] = jnp.dot(a_ref[...], w_scr[...], preferred_element_type=F32)

    @pl.when(jnp.logical_not(valid))
    def _():
        o_ref[...] = jnp.zeros_like(o_ref)


def _expert_ffn(xg, blk_e, nvalid, w_gate, w_up, w_down, li, tm, tf=384):
    p, d = xg.shape
    f = w_gate.shape[3]
    nb = p // tm
    nf = f // tf
    tiles = jnp.arange(nb, dtype=jnp.int32)
    valid = tiles < nvalid[0]
    first = valid & ((tiles == 0) | (blk_e != jnp.roll(blk_e, 1)))
    run = (jnp.cumsum(first.astype(jnp.int32)) - 1).astype(jnp.int32)
    nruns = jnp.sum(first.astype(jnp.int32)).reshape(1)
    run_e = jnp.sum(jnp.where(first[None, :] & (run[None, :] == tiles[:, None]), blk_e[None, :], 0),
                    axis=1)
    nxt = jnp.sum(jnp.where(tiles[None, :] == run[:, None] + 1, run_e[None, :], 0), axis=1)
    nxt = jnp.where(run + 1 < nruns[0], nxt, -1).astype(jnp.int32)
    first = first.astype(jnp.int32)
    last = lambda r, nv: jnp.minimum(r, nv[0] - 1)
    act = pl.pallas_call(
        functools.partial(_ffn_up_kernel, li=li, tf=tf, nf=nf),
        grid_spec=pltpu.PrefetchScalarGridSpec(
            num_scalar_prefetch=6,
            grid=(nf, nb),
            in_specs=[pl.BlockSpec((tm, d), lambda j, r, be, nv, *_: (last(r, nv), 0)),
                      pl.BlockSpec(memory_space=pl.ANY),
                      pl.BlockSpec(memory_space=pl.ANY)],
            out_specs=pl.BlockSpec((tm, tf), lambda j, r, *_: (r, j)),
            scratch_shapes=[pltpu.VMEM((2, 2, d, tf), F32), pltpu.VMEM((d, 2 * tf), BF16),
                            pltpu.SemaphoreType.DMA((2,))],
        ),
        out_shape=jax.ShapeDtypeStruct((p, f), BF16),
        compiler_params=_params("arbitrary", "arbitrary"),
        name="moe_ffn_up",
    )(blk_e, nvalid, first, run, nxt, nruns, xg, w_gate, w_up)
    return pl.pallas_call(
        functools.partial(_ffn_down_kernel, li=li),
        grid_spec=pltpu.PrefetchScalarGridSpec(
            num_scalar_prefetch=5,
            grid=(nb,),
            in_specs=[pl.BlockSpec((tm, f), lambda r, be, nv, *_: (last(r, nv), 0)),
                      pl.BlockSpec(memory_space=pl.ANY)],
            out_specs=pl.BlockSpec((tm, d), lambda r, *_: (r, 0)),
            scratch_shapes=[pltpu.VMEM((2, f, d), F32), pltpu.VMEM((f, d), BF16),
                            pltpu.SemaphoreType.DMA((2,))],
        ),
        out_shape=jax.ShapeDtypeStruct((p, d), F32),
        compiler_params=_params("arbitrary"),
        name="moe_ffn_down",
    )(blk_e, nvalid, first, run, nxt, act, w_down)


def _hier_moe(x, norm_g, w_group, b_group, w_router, b_router, w_gate, w_up, w_down, li,
              next_norm_g, tm=MOE_TM, tc=128):
    n_tok, d = x.shape
    sel, w_exp = _router(x, norm_g, w_group, b_group, w_router, b_router)

    n_as = n_tok * TOP_K
    nb = -(-n_as // tm) + N_EXPERTS
    p_rows = nb * tm
    n_pad = p_rows - n_as
    flat_e = sel[:, :TOP_K].reshape(n_as)
    experts = jnp.arange(N_EXPERTS, dtype=jnp.int32)
    counts = jnp.sum((flat_e[:, None] == experts[None, :]).astype(jnp.int32), axis=0)
    pcounts = (counts + tm - 1) // tm * tm
    nvalid = (jnp.sum(pcounts) // tm).astype(jnp.int32).reshape(1)
    pad_end = jnp.cumsum(pcounts - counts)
    pad_e = jnp.sum((jnp.arange(n_pad, dtype=jnp.int32)[:, None] >= pad_end[None, :]).astype(jnp.int32),
                    axis=1)
    keys = jnp.concatenate([2 * flat_e, 2 * pad_e + 1])
    vals = jnp.concatenate([jnp.arange(n_as, dtype=jnp.int32), jnp.full((n_pad,), -1, jnp.int32)])
    skeys, svals = lax.sort((keys, vals), num_keys=1)
    buf_tok = jnp.where(svals >= 0, svals // TOP_K, 0)
    _, pos = lax.sort((svals, jnp.arange(p_rows, dtype=jnp.int32)), num_keys=1)
    dest = pos[n_pad:].reshape(n_tok, TOP_K)
    blk_e = jnp.minimum(skeys.reshape(nb, tm)[:, 0] // 2, N_EXPERTS - 1).astype(jnp.int32)

    xg = _gather_norm(x, norm_g, buf_tok, nvalid, tm)
    yb = _expert_ffn(xg, blk_e, nvalid, w_gate, w_up, w_down, li, tm)
    dest_t = dest.reshape(n_tok // tc, tc, TOP_K).transpose(0, 2, 1).reshape(n_tok // tc, 1, TOP_K * tc)
    return _combine(x, w_exp, yb, dest_t, next_norm_g, tm=tc)


def _ab_layer(x, h, w_in, li, conv_w, conv_b, wa, ba, wi, bi, lam, bf, qn, kn, w_out, bsz):
    n_tok, d = x.shape
    t = n_tok // bsz
    d_rnn = conv_w.shape[1]
    nh = bf.shape[0]
    fw = nh * HEAD_DIM
    zga = _proj_f32(h, w_in, li, 0, 2 * d_rnn, wt=True).reshape(bsz, t, 2 * d_rnn)
    ya = _rglru(zga, bsz, conv_w, conv_b, wa, ba, wi, bi, lam)
    normw = jnp.concatenate([jnp.tile(qn, nh), jnp.tile(kn * FOX_QK_SCALE, nh),
                             jnp.ones((fw,), F32)]).reshape(1, 3 * fw)
    qkv = _proj_heads(h, w_in, li, 2 * d_rnn, 3 * fw, normw, 2 * fw, bsz, 1, wt=True)
    b_f = jnp.pad(bf.astype(F32), (0, LANES - nh)).reshape(1, LANES)
    cum = _forget_cum(h, w_in, li, 2 * d_rnn + 3 * fw, nh, b_f, bsz, wt=True)
    fox_tq = 512
    cum_row = cum[:, :, :nh].transpose(0, 2, 1).reshape(bsz, nh, t // fox_tq, fox_tq)
    yb = _fox_attention(qkv, cum, cum_row, tq=fox_tq)
    return _proj_res([ya.reshape(n_tok, d_rnn), yb.reshape(n_tok, fw)], w_out, li, x)


def _dilated_layer(x, h, w_in, li, qn, kn, table, w_out, bsz):
    n_tok, d = x.shape
    t = n_tok // bsz
    nh = table.shape[1] // len(DSW_PATTERNS)
    fw = nh * HEAD_DIM
    os_, lses = [], []
    for g, (window, dil) in enumerate(DSW_PATTERNS):
        normw = jnp.concatenate([jnp.tile(qn[g], nh), jnp.tile(kn[g], nh),
                                 jnp.ones((fw,), F32)]).reshape(1, 3 * fw)
        qkv = _proj_heads(h, w_in, li, g * 3 * fw, 3 * fw, normw, 2 * fw, bsz, dil)
        bias = _dsw_bias(table[:, g * nh:(g + 1) * nh], window, dil)
        o, lse = _dsw_group(qkv, bias, min(t // dil, 1024))
        os_.append(o)
        lses.append(lse)
    y = _dsw_merge(os_, lses)
    return _proj_res([y], w_out, li, x)


def kernel(x, mix_norm, ffn_norm, w_in_ab, conv_w, conv_b, lru_wa, lru_ba, lru_wi, lru_bi,
           lru_lambda, fox_bf, fox_qnorm, fox_knorm, w_out_ab, w_in_c, dsw_qnorm, dsw_knorm,
           w_out_c, rel_bias, moe_w_group, moe_b_group, moe_w_router, moe_b_router,
           moe_w_gate, moe_w_up, moe_w_down):
    bsz, t, d = x.shape
    depth = mix_norm.shape[0]
    xf = x.reshape(bsz * t, d)
    h = _rmsnorm(xf, mix_norm[0])
    for layer in range(depth):
        i = layer // 2
        if layer % 2 == 0:
            xf = _ab_layer(xf, h, jnp.swapaxes(w_in_ab, 1, 2), i, conv_w[i], conv_b[i], lru_wa[i],
                           lru_ba[i], lru_wi[i], lru_bi[i], lru_lambda[i], fox_bf[i],
                           fox_qnorm[i], fox_knorm[i], w_out_ab, bsz)
        else:
            xf = _dilated_layer(xf, h, w_in_c, i, dsw_qnorm[i], dsw_knorm[i],
                                rel_bias, w_out_c, bsz)
        next_g = mix_norm[layer + 1] if layer + 1 < depth else None
        xf, h = _hier_moe(xf, ffn_norm[layer], moe_w_group[layer], moe_b_group[layer],
                          moe_w_router[layer], moe_b_router[layer], moe_w_gate, moe_w_up,
                          moe_w_down, layer, next_g)
    return xf.reshape(bsz, t, d)
```

```python
import functools
import math

import jax
import jax.numpy as jnp
from jax import lax
from jax.experimental import pallas as pl
from jax.experimental.pallas import tpu as pltpu

F32 = jnp.float32
BF16 = jnp.bfloat16

LANES = 128
SUBLANES = 8
VMEM_LIMIT = 56 * 1024 * 1024

HEAD_DIM = 128
RMS_EPS = 1e-6
RNN_BLOCKS = 16
CONV_WIDTH = 4
LRU_C = 8.0
DSW_PATTERNS = ((128, 1), (512, 4), (2048, 16))
DSW_BLOCK = 128
DSW_CHAINS = 8
DSW_SKEW = 2
N_BUCKETS = 32
MAX_DISTANCE = 2048
N_GROUPS = 4
EXPERTS_PER_GROUP = 8
N_EXPERTS = N_GROUPS * EXPERTS_PER_GROUP
TOP_K = 2
LOG2E = math.log2(math.e)

MOE_TM = 256


def _params(*sem):
    return pltpu.CompilerParams(dimension_semantics=sem, vmem_limit_bytes=VMEM_LIMIT)


def _rmsnorm_kernel(x_ref, g_ref, o_ref):
    x = x_ref[...]
    ms = jnp.mean(x * x, axis=-1, keepdims=True)
    o_ref[...] = (x * lax.rsqrt(ms + RMS_EPS) * g_ref[...]).astype(o_ref.dtype)


def _rmsnorm(x, g, tm=256):
    m, d = x.shape
    return pl.pallas_call(
        _rmsnorm_kernel,
        grid=(m // tm,),
        in_specs=[pl.BlockSpec((tm, d), lambda i: (i, 0)),
                  pl.BlockSpec((1, d), lambda i: (0, 0))],
        out_specs=pl.BlockSpec((tm, d), lambda i: (i, 0)),
        out_shape=jax.ShapeDtypeStruct((m, d), BF16),
        compiler_params=_params("parallel"),
        name="rmsnorm",
    )(x, g.reshape(1, d))


def _mm(a, w_blk, wt):
    w = w_blk.astype(BF16)
    if wt:
        return lax.dot_general(a, w, (((1,), (1,)), ((), ())), preferred_element_type=F32)
    return jnp.dot(a, w, preferred_element_type=F32)


def _w_spec(k, tn, li, jb, wt, grid_rank):
    if wt:
        return pl.BlockSpec((None, tn, k), lambda *g: (li, jb + g[grid_rank - 1], 0))
    return pl.BlockSpec((None, k, tn), lambda *g: (li, 0, jb + g[grid_rank - 1]))


def _proj_f32_kernel(a_ref, w_ref, o_ref, *, wt):
    o_ref[...] = _mm(a_ref[...], w_ref[...], wt)


def _proj_f32(a, w, li, col0, ncols, wt=False, tm=1024, tn=512):
    m, k = a.shape
    jb = col0 // tn
    return pl.pallas_call(
        functools.partial(_proj_f32_kernel, wt=wt),
        grid=(m // tm, ncols // tn),
        in_specs=[pl.BlockSpec((tm, k), lambda i, j: (i, 0)),
                  _w_spec(k, tn, li, jb, wt, 2)],
        out_specs=pl.BlockSpec((tm, tn), lambda i, j: (i, j)),
        out_shape=jax.ShapeDtypeStruct((m, ncols), F32),
        compiler_params=_params("parallel", "arbitrary"),
        name="proj_f32",
    )(a, w)


def _proj_res_kernel(*refs, n_in):
    a_refs, w_refs = refs[:n_in], refs[n_in:2 * n_in]
    r_ref, o_ref = refs[2 * n_in], refs[2 * n_in + 1]
    acc = r_ref[...]
    for a_ref, w_ref in zip(a_refs, w_refs):
        acc = acc + jnp.dot(a_ref[...], w_ref[...].astype(BF16), preferred_element_type=F32)
    o_ref[...] = acc


def _proj_res(a_list, w, li, res, tm=1024, tn=512):
    m, k = a_list[0].shape
    n_in = len(a_list)
    n = w.shape[2]
    a_specs = [pl.BlockSpec((tm, k), lambda i, j: (i, 0)) for _ in a_list]
    w_specs = [pl.BlockSpec((None, k, tn), lambda i, j, p=p: (li, p, j)) for p in range(n_in)]
    return pl.pallas_call(
        functools.partial(_proj_res_kernel, n_in=n_in),
        grid=(m // tm, n // tn),
        in_specs=a_specs + w_specs + [pl.BlockSpec((tm, tn), lambda i, j: (i, j))],
        out_specs=pl.BlockSpec((tm, tn), lambda i, j: (i, j)),
        out_shape=jax.ShapeDtypeStruct((m, n), F32),
        compiler_params=_params("parallel", "arbitrary"),
        name="proj_res",
    )(*a_list, *([w] * n_in), res)


PROJ_SUB = 256
MAX_ROW_STRIDE = 4


def _proj_heads_kernel(a_ref, w_ref, nw_ref, o_ref, y_scr, t_scr, *, dil, n_norm_blocks, hpb, tm, wt):
    is_norm = pl.program_id(2) < n_norm_blocks
    rows = tm // dil
    hps = PROJ_SUB // HEAD_DIM
    for sb in range(hpb // hps):
        cs = slice(sb * PROJ_SUB, (sb + 1) * PROJ_SUB)
        acc = _mm(a_ref[...], w_ref[cs, :] if wt else w_ref[:, cs], wt)
        for h2 in range(hps):
            hh = sb * hps + h2
            y = acc[:, h2 * HEAD_DIM:(h2 + 1) * HEAD_DIM]
            ms = jnp.mean(y * y, axis=-1, keepdims=True)
            yn = y * lax.rsqrt(ms + RMS_EPS) * nw_ref[:, hh * HEAD_DIM:(hh + 1) * HEAD_DIM]
            y = jnp.where(is_norm, yn, y)
            if dil == 1:
                o_ref[0, 0, hh] = y.astype(o_ref.dtype)
            else:
                y_scr[hh] = y
        if dil > 1:
            s1 = min(dil, MAX_ROW_STRIDE)
            s2 = dil // s1
            for hh in range(sb * hps, (sb + 1) * hps):
                for z1 in range(s1):
                    if s2 == 1:
                        o_ref[0, z1, hh] = y_scr[hh, pl.ds(z1, rows, stride=s1), :].astype(o_ref.dtype)
                        continue
                    t_scr[z1] = y_scr[hh, pl.ds(z1, tm // s1, stride=s1), :]
                    for z2 in range(s2):
                        o_ref[0, z1 + s1 * z2, hh] = (
                            t_scr[z1, pl.ds(z2, rows, stride=s2), :].astype(o_ref.dtype))


def _proj_heads(a, w, li, col0, ncols, normw, n_norm_cols, bsz, dil, wt=False, tm=1024, tn=512):
    m, k = a.shape
    t = m // bsz
    nt = t // tm
    hpb = tn // HEAD_DIM
    jb = col0 // tn
    kern = functools.partial(_proj_heads_kernel, dil=dil, n_norm_blocks=n_norm_cols // tn,
                             hpb=hpb, tm=tm, wt=wt)
    return pl.pallas_call(
        kern,
        grid=(bsz, nt, ncols // tn),
        in_specs=[pl.BlockSpec((tm, k), lambda b, i, j: (b * nt + i, 0)),
                  _w_spec(k, tn, li, jb, wt, 3),
                  pl.BlockSpec((1, tn), lambda b, i, j: (0, j))],
        out_specs=pl.BlockSpec((1, dil, hpb, tm // dil, HEAD_DIM),
                               lambda b, i, j: (b, 0, j, i, 0)),
        out_shape=jax.ShapeDtypeStruct((bsz, dil, ncols // HEAD_DIM, t // dil, HEAD_DIM), BF16),
        scratch_shapes=[pltpu.VMEM((hpb, tm, HEAD_DIM), F32),
                        pltpu.VMEM((MAX_ROW_STRIDE, tm // MAX_ROW_STRIDE, HEAD_DIM), F32)],
        compiler_params=_params("parallel", "parallel", "arbitrary"),
        name=f"proj_heads_d{dil}",
    )(a, w, normw)


def _forget_cum_kernel(a_ref, w_ref, b_ref, o_ref, carry_ref, *, tm, nh, wt):
    i = pl.program_id(1)

    @pl.when(i == 0)
    def _():
        carry_ref[...] = jnp.zeros_like(carry_ref)

    lane = lax.broadcasted_iota(jnp.int32, (tm, LANES), 1)
    fz = _mm(a_ref[...], w_ref[...], wt)
    c = jax.nn.log_sigmoid(jnp.where(lane < nh, fz + b_ref[...], 0.0))
    row = lax.broadcasted_iota(jnp.int32, c.shape, 0)
    s = 1
    while s < tm:
        c = c + jnp.where(row >= s, pltpu.roll(c, s, 0), 0.0)
        s *= 2
    c = c + carry_ref[0:1, :]
    o_ref[0] = c
    carry_ref[...] = jnp.broadcast_to(c[tm - 1:tm, :], carry_ref.shape)


def _forget_cum(a, w, li, col0, nh, b_f, bsz, wt=False, tm=512):
    m, k = a.shape
    t = m // bsz
    nt = t // tm
    jb = col0 // LANES
    return pl.pallas_call(
        functools.partial(_forget_cum_kernel, tm=tm, nh=nh, wt=wt),
        grid=(bsz, nt),
        in_specs=[pl.BlockSpec((tm, k), lambda b, i: (b * nt + i, 0)),
                  (pl.BlockSpec((None, LANES, k), lambda b, i: (li, jb, 0)) if wt
                   else pl.BlockSpec((None, k, LANES), lambda b, i: (li, 0, jb))),
                  pl.BlockSpec((1, LANES), lambda b, i: (0, 0))],
        out_specs=pl.BlockSpec((1, tm, LANES), lambda b, i: (b, i, 0)),
        out_shape=jax.ShapeDtypeStruct((bsz, t, LANES), F32),
        scratch_shapes=[pltpu.VMEM((SUBLANES, LANES), F32)],
        compiler_params=_params("parallel", "arbitrary"),
        name="forget_cum",
    )(a, w, b_f)


def _gelu_tanh(x):
    return 0.5 * x * (1.0 + jnp.tanh(math.sqrt(2.0 / math.pi) * (x + 0.044715 * (x * x * x))))


def _rglru_kernel(g_ref, x_ref, cw_ref, cb_ref, wa_ref, ba_ref, wi_ref, bi_ref, lam_ref,
                  o_ref, xs_scr, a_scr, u_scr, h_scr, carry_scr, *, tc, nblk):
    ti = pl.program_id(2)
    halo = SUBLANES

    @pl.when(ti == 0)
    def _():
        xs_scr[0:halo, :] = jnp.zeros((halo, xs_scr.shape[1]), F32)
        carry_scr[...] = jnp.zeros_like(carry_scr)

    xs_scr[halo:halo + tc, :] = x_ref[0]
    xc = cb_ref[...] + cw_ref[CONV_WIDTH - 1:CONV_WIDTH, :] * xs_scr[halo:halo + tc, :]
    for j in range(CONV_WIDTH - 1):
        sh = CONV_WIDTH - 1 - j
        xc = xc + cw_ref[j:j + 1, :] * xs_scr[halo - sh:halo - sh + tc, :]
    xs_scr[0:halo, :] = xs_scr[tc:tc + halo, :]

    neg_sp = -LRU_C * jax.nn.softplus(-lam_ref[...])
    for n in range(nblk):
        cs = slice(n * HEAD_DIM, (n + 1) * HEAD_DIM)
        xb = xc[:, cs]
        xbb = xb.astype(BF16)
        r = jax.nn.sigmoid(jnp.dot(xbb, wa_ref[n].astype(BF16), preferred_element_type=F32)
                           + ba_ref[:, cs])
        ig = jax.nn.sigmoid(jnp.dot(xbb, wi_ref[n].astype(BF16), preferred_element_type=F32)
                            + bi_ref[:, cs])
        log_a = neg_sp[:, cs] * r
        a = jnp.exp(log_a)
        a_scr[:, cs] = a
        u_scr[:, cs] = jnp.sqrt(-jnp.tanh(log_a) * (a * a + 1.0)) * (ig * xb)

    row = lax.broadcasted_iota(jnp.int32, (SUBLANES, a_scr.shape[1]), 0)

    def body(g, carry):
        r0 = pl.multiple_of(g * SUBLANES, SUBLANES)
        a = a_scr[pl.ds(r0, SUBLANES), :]
        u = u_scr[pl.ds(r0, SUBLANES), :]
        s = 1
        while s < SUBLANES:
            keep = row >= s
            u = jnp.where(keep, a * pltpu.roll(u, s, 0) + u, u)
            a = jnp.where(keep, a * pltpu.roll(a, s, 0), a)
            s *= 2
        h = a * carry + u
        h_scr[pl.ds(r0, SUBLANES), :] = h
        return jnp.broadcast_to(h[SUBLANES - 1:SUBLANES, :], h.shape)

    carry = lax.fori_loop(0, tc // SUBLANES, body, carry_scr[...], unroll=4)
    carry_scr[...] = carry
    o_ref[0] = (h_scr[...] * _gelu_tanh(g_ref[0])).astype(o_ref.dtype)


def _rglru(zga, bsz, conv_w, conv_b, wa, ba, wi, bi, lam, tc=256, cw=512):
    _, t, c2 = zga.shape
    c = c2 // 2
    nblk = cw // HEAD_DIM
    ncb = c // cw
    vec = lambda v: v.reshape(1, c)
    vspec = pl.BlockSpec((1, cw), lambda b, j, i: (0, j))
    kern = functools.partial(_rglru_kernel, tc=tc, nblk=nblk)
    return pl.pallas_call(
        kern,
        grid=(bsz, ncb, t // tc),
        in_specs=[pl.BlockSpec((1, tc, cw), lambda b, j, i: (b, i, j)),
                  pl.BlockSpec((1, tc, cw), lambda b, j, i: (b, i, ncb + j)),
                  pl.BlockSpec((CONV_WIDTH, cw), lambda b, j, i: (0, j)),
                  vspec,
                  pl.BlockSpec((nblk, HEAD_DIM, HEAD_DIM), lambda b, j, i: (j, 0, 0)),
                  vspec,
                  pl.BlockSpec((nblk, HEAD_DIM, HEAD_DIM), lambda b, j, i: (j, 0, 0)),
                  vspec, vspec],
        out_specs=pl.BlockSpec((1, tc, cw), lambda b, j, i: (b, i, j)),
        out_shape=jax.ShapeDtypeStruct((bsz, t, c), BF16),
        scratch_shapes=[pltpu.VMEM((tc + SUBLANES, cw), F32),
                        pltpu.VMEM((tc, cw), F32),
                        pltpu.VMEM((tc, cw), F32),
                        pltpu.VMEM((tc, cw), F32),
                        pltpu.VMEM((SUBLANES, cw), F32)],
        compiler_params=_params("parallel", "parallel", "arbitrary"),
        name="rglru",
    )(zga, zga, conv_w, vec(conv_b), wa, vec(ba), wi, vec(bi), vec(lam))


FOX_STRIP = 128
FOX_QK_SCALE = LOG2E / math.sqrt(HEAD_DIM)
FOX_SKEW = 3


def _fox_kernel(q_ref, k_ref, v_ref, cq_ref, ck_ref, o_ref, m_scr, l_scr, acc_scr, cq_scr, *, tq, hp):
    g = pl.program_id(1)
    qi = pl.program_id(2)
    tk = tq
    sr = FOX_STRIP
    lane = lax.broadcasted_iota(jnp.int32, (tq, LANES), 1)
    for hh in range(hp):
        cq = jnp.sum(jnp.where(lane == g * hp + hh, cq_ref[0], 0.0), axis=-1, keepdims=True)
        cq_scr[hh] = jnp.broadcast_to(cq * LOG2E, (tq, LANES))
        m_scr[hh] = jnp.full(m_scr.shape[1:], -jnp.inf, F32)
        l_scr[hh] = jnp.zeros(l_scr.shape[1:], F32)
        acc_scr[hh] = jnp.zeros(acc_scr.shape[1:], F32)
    tri = (lax.broadcasted_iota(jnp.int32, (sr, LANES), 1)
           <= lax.broadcasted_iota(jnp.int32, (sr, LANES), 0))

    def scores(hh, r, j, diagonal):
        rs = slice(r * sr, (r + 1) * sr)
        k0 = pl.multiple_of(j * tk, tk)
        nk = (r + 1) * sr if diagonal else tk
        return lax.dot_general(q_ref[0, 0, hh, rs, :], k_ref[0, 0, hh, pl.ds(k0, nk), :],
                               (((1,), (1,)), ((), ())), preferred_element_type=F32)

    def softmax_pv(s, hh, r, j, ck, diagonal):
        rs = slice(r * sr, (r + 1) * sr)
        k0 = pl.multiple_of(j * tk, tk)
        nk = s.shape[1]
        cq = cq_scr[hh, rs, :]
        parts = []
        for c in range(nk // LANES):
            cs = slice(c * LANES, (c + 1) * LANES)
            part = s[:, cs] + (cq - ck[:, cs])
            if diagonal and c == r:
                part = jnp.where(tri, part, -jnp.inf)
            parts.append(part)
        m_prev = m_scr[hh, rs, :]
        m_new = jnp.maximum(m_prev, jnp.max(functools.reduce(jnp.maximum, parts),
                                            axis=-1, keepdims=True))
        alpha = jnp.exp2(m_prev - m_new)
        ps = [jnp.exp2(part - m_new) for part in parts]
        row_sum = jnp.sum(functools.reduce(lambda a, b: a + b, ps), axis=-1, keepdims=True)
        l_scr[hh, rs, :] = alpha * l_scr[hh, rs, :] + row_sum
        p = jnp.concatenate(ps, axis=1).astype(BF16)
        acc_scr[hh, rs, :] = alpha * acc_scr[hh, rs, :] + jnp.dot(
            p, v_ref[0, 0, hh, pl.ds(k0, nk), :], preferred_element_type=F32)
        m_scr[hh, rs, :] = m_new

    def kv_step(j, diagonal):
        cks = [ck_ref[0, hh, pl.ds(j, 1), :] * LOG2E for hh in range(hp)]
        strips = [(hh, r) for hh in range(hp) for r in range(tq // sr)]
        pending = [scores(*strips[n], j, diagonal) for n in range(FOX_SKEW)]
        for n, (hh, r) in enumerate(strips):
            if n + FOX_SKEW < len(strips):
                pending.append(scores(*strips[n + FOX_SKEW], j, diagonal))
            softmax_pv(pending.pop(0), hh, r, j, cks[hh], diagonal)

    def below_diagonal(j, carry):
        kv_step(j, False)
        return carry

    lax.fori_loop(0, qi, below_diagonal, 0)
    kv_step(qi, True)
    for hh in range(hp):
        o_ref[0, :, hh * HEAD_DIM:(hh + 1) * HEAD_DIM] = (acc_scr[hh] / l_scr[hh]).astype(o_ref.dtype)


def _fox_attention(qkv, cum_col, cum_row, tq=512, hp=2):
    bsz, _, h3, t, dh = qkv.shape
    nh = h3 // 3
    nq = t // tq
    ng = nh // hp
    kern = functools.partial(_fox_kernel, tq=tq, hp=hp)
    return pl.pallas_call(
        kern,
        grid=(bsz, ng, nq),
        in_specs=[pl.BlockSpec((1, 1, hp, tq, dh), lambda b, g, i: (b, 0, g, i, 0)),
                  pl.BlockSpec((1, 1, hp, t, dh), lambda b, g, i: (b, 0, ng + g, 0, 0)),
                  pl.BlockSpec((1, 1, hp, t, dh), lambda b, g, i: (b, 0, 2 * ng + g, 0, 0)),
                  pl.BlockSpec((1, tq, LANES), lambda b, g, i: (b, i, 0)),
                  pl.BlockSpec((1, hp, nq, tq), lambda b, g, i: (b, g, 0, 0))],
        out_specs=pl.BlockSpec((1, tq, hp * dh), lambda b, g, i: (b, i, g)),
        out_shape=jax.ShapeDtypeStruct((bsz, t, nh * dh), BF16),
        scratch_shapes=[pltpu.VMEM((hp, tq, LANES), F32), pltpu.VMEM((hp, tq, LANES), F32),
                        pltpu.VMEM((hp, tq, dh), F32), pltpu.VMEM((hp, tq, LANES), F32)],
        compiler_params=_params("parallel", "parallel", "arbitrary"),
        name="fox_attention",
    )(qkv, qkv, qkv, cum_col, cum_row)


def _dsw_kernel(q_ref, kc_ref, kp_ref, vc_ref, vp_ref, bias_ref, o_ref, lse_ref, *, lc, nh):
    c = pl.program_id(2)
    blk = DSW_BLOCK
    nb = lc // blk
    lane = lax.broadcasted_iota(jnp.int32, (blk, LANES), 1)
    col = lax.broadcasted_iota(jnp.int32, (blk, 2 * blk), 1)
    scale = 1.0 / math.sqrt(HEAD_DIM)
    lse_ref[...] = jnp.zeros_like(lse_ref)

    def scores(h, n):
        rs = slice(n * blk, (n + 1) * blk)
        if n == 0:
            kprev = kp_ref[0, 0, h]
        else:
            kprev = kc_ref[0, 0, h, (n - 1) * blk:n * blk, :]
        kk = jnp.concatenate([kprev, kc_ref[0, 0, h, rs, :]], axis=0)
        return lax.dot_general(q_ref[0, 0, h, rs, :], kk, (((1,), (1,)), ((), ())),
                               preferred_element_type=F32)

    def softmax_pv(s, h, n):
        rs = slice(n * blk, (n + 1) * blk)
        if n == 0:
            vprev = vp_ref[0, 0, h]
        else:
            vprev = vc_ref[0, 0, h, (n - 1) * blk:n * blk, :]
        vv = jnp.concatenate([vprev, vc_ref[0, 0, h, rs, :]], axis=0)
        s = s * scale + bias_ref[h]
        if n == 0:
            s = jnp.where((c > 0) | (col >= blk), s, -jnp.inf)
        m = jnp.max(s, axis=-1, keepdims=True)
        e = jnp.exp(s - m)
        den = jnp.sum(e, axis=-1, keepdims=True)
        o = jnp.dot(e.astype(BF16), vv, preferred_element_type=F32) / den
        o_ref[0, 0, rs, pl.ds(pl.multiple_of(h * HEAD_DIM, HEAD_DIM), HEAD_DIM)] = (
            o.astype(o_ref.dtype))
        lse = m + jnp.log(den)
        lse_ref[0, 0, rs, :] = jnp.where(lane == h, lse, lse_ref[0, 0, rs, :])

    hu = math.gcd(nh, max(1, DSW_CHAINS // nb))

    def heads(i, carry):
        blocks = [(i * hu + hh, n) for hh in range(hu) for n in range(nb)]
        pending = [scores(*blocks[k]) for k in range(DSW_SKEW)]
        for k, (h, n) in enumerate(blocks):
            if k + DSW_SKEW < len(blocks):
                pending.append(scores(*blocks[k + DSW_SKEW]))
            softmax_pv(pending.pop(0), h, n)
        return carry

    lax.fori_loop(0, nh // hu, heads, 0)


def _dsw_group(qkv, bias, lc):
    bsz, dil, h3, l, dh = qkv.shape
    nh = h3 // 3
    nc = l // lc
    bpc = lc // DSW_BLOCK
    kern = functools.partial(_dsw_kernel, lc=lc, nh=nh)
    cur = lambda off: pl.BlockSpec((1, 1, nh, lc, dh), lambda b, z, c: (b, z, off, c, 0))
    prev = lambda off: pl.BlockSpec((1, 1, nh, DSW_BLOCK, dh),
                                    lambda b, z, c: (b, z, off, jnp.maximum(c * bpc - 1, 0), 0))
    return pl.pallas_call(
        kern,
        grid=(bsz, dil, nc),
        in_specs=[cur(0), cur(1), prev(1), cur(2), prev(2),
                  pl.BlockSpec((nh, DSW_BLOCK, 2 * DSW_BLOCK), lambda b, z, c: (0, 0, 0))],
        out_specs=[pl.BlockSpec((1, 1, lc, nh * dh), lambda b, z, c: (b, z, c, 0)),
                   pl.BlockSpec((1, 1, lc, LANES), lambda b, z, c: (b, z, c, 0))],
        out_shape=[jax.ShapeDtypeStruct((bsz, dil, l, nh * dh), BF16),
                   jax.ShapeDtypeStruct((bsz, dil, l, LANES), F32)],
        compiler_params=_params("parallel", "parallel", "arbitrary"),
        name=f"dsw_attention_d{dil}",
    )(qkv, qkv, qkv, qkv, qkv, bias)


def _dsw_merge_kernel(*refs, dils, tm, nh):
    ng = len(dils)
    o_refs = refs[:ng]
    l_refs = refs[ng:2 * ng]
    out_ref = refs[2 * ng]
    o_scr = refs[2 * ng + 1:2 * ng + 1 + ng]
    l_scr = refs[2 * ng + 1 + ng:]
    for g, dil in enumerate(dils):
        rows = tm // dil
        for z in range(dil):
            rsel = slice(None) if dil == 1 else pl.ds(z, rows, stride=dil)
            l_scr[g][rsel, :] = l_refs[g][0, z]
            for h in range(nh):
                o_scr[g][h, rsel, :] = o_refs[g][0, z, :, h * HEAD_DIM:(h + 1) * HEAD_DIM].astype(F32)
    lses = [l_scr[g][...] for g in range(ng)]
    m = functools.reduce(jnp.maximum, lses)
    ws = [jnp.exp(l - m) for l in lses]
    inv = 1.0 / functools.reduce(lambda a, b: a + b, ws)
    for h in range(nh):
        acc = None
        for g in range(ng):
            alpha = (ws[g] * inv)[:, h:h + 1]
            term = alpha * o_scr[g][h]
            acc = term if acc is None else acc + term
        out_ref[:, h * HEAD_DIM:(h + 1) * HEAD_DIM] = acc.astype(out_ref.dtype)


def _dsw_merge(os_, lses, tm=512):
    bsz, _, t, w = os_[0].shape
    nh = w // HEAD_DIM
    dils = tuple(o.shape[1] for o in os_)
    t = t * dils[0]
    nt = t // tm
    kern = functools.partial(_dsw_merge_kernel, dils=dils, tm=tm, nh=nh)
    o_specs = [pl.BlockSpec((1, d, tm // d, w), lambda b, i: (b, 0, i, 0)) for d in dils]
    l_specs = [pl.BlockSpec((1, d, tm // d, LANES), lambda b, i: (b, 0, i, 0)) for d in dils]
    return pl.pallas_call(
        kern,
        grid=(bsz, nt),
        in_specs=o_specs + l_specs,
        out_specs=pl.BlockSpec((tm, w), lambda b, i: (b * nt + i, 0)),
        out_shape=jax.ShapeDtypeStruct((bsz * t, w), BF16),
        scratch_shapes=([pltpu.VMEM((nh, tm, HEAD_DIM), F32) for _ in dils]
                        + [pltpu.VMEM((tm, LANES), F32) for _ in dils]),
        compiler_params=_params("parallel", "parallel"),
        name="dsw_merge",
    )(*os_, *lses)


def _t5_bucket(dist):
    max_exact = N_BUCKETS // 2
    large = max_exact + (jnp.log(jnp.maximum(dist, 1).astype(F32) / max_exact)
                         / math.log(MAX_DISTANCE / max_exact) * (N_BUCKETS - max_exact)).astype(jnp.int32)
    return jnp.where(dist < max_exact, dist, jnp.minimum(large, N_BUCKETS - 1))


def _dsw_bias(table, window, dil):
    span = window // dil
    qi = jnp.arange(DSW_BLOCK)[:, None]
    kj = jnp.arange(2 * DSW_BLOCK)[None, :]
    delta = qi + DSW_BLOCK - kj
    band = (delta >= 0) & (delta <= span)
    bucket = _t5_bucket(jnp.maximum(delta, 0) * dil)
    onehot = (bucket[None, :, :] == jnp.arange(N_BUCKETS)[:, None, None])
    table_t = table.astype(F32).T
    bias = jnp.sum(jnp.where(onehot[None], table_t[:, :, None, None], 0.0), axis=1)
    return jnp.where(band[None], bias, -jnp.inf)


def _router_kernel(x_ref, g_ref, whi_ref, wlo_ref, b_ref, sel_ref, wts_ref):
    x = x_ref[...]
    tm = x.shape[0]
    ms = jnp.mean(x * x, axis=-1, keepdims=True)
    h = x * lax.rsqrt(ms + RMS_EPS) * g_ref[...]
    hi = h.astype(BF16)
    lo = (h - hi.astype(F32)).astype(BF16)
    logits = (jnp.dot(hi, whi_ref[...], preferred_element_type=F32)
              + (jnp.dot(hi, wlo_ref[...], preferred_element_type=F32)
                 + jnp.dot(lo, whi_ref[...], preferred_element_type=F32))) + b_ref[...]
    lane = lax.broadcasted_iota(jnp.int32, (tm, LANES), 1)
    first_max = lambda v, vmax: jnp.min(jnp.where(v == vmax, lane, LANES), axis=-1, keepdims=True)

    glog = jnp.where(lane < N_GROUPS, logits, -jnp.inf)
    gmax = jnp.max(glog, axis=-1, keepdims=True)
    gsel = first_max(glog, gmax)
    pg = 1.0 / jnp.sum(jnp.exp(glog - gmax), axis=-1, keepdims=True)

    ex = lane - N_GROUPS
    in_group = (ex >= 0) & (ex < N_EXPERTS) & ((ex // EXPERTS_PER_GROUP) == gsel)
    elog = jnp.where(in_group, logits, -jnp.inf)
    v1 = jnp.max(elog, axis=-1, keepdims=True)
    j1 = first_max(elog, v1)
    elog2 = jnp.where(lane == j1, -jnp.inf, elog)
    v2 = jnp.max(elog2, axis=-1, keepdims=True)
    j2 = first_max(elog2, v2)
    e2 = jnp.exp(v2 - v1)
    w1 = pg / (1.0 + e2)
    w2 = pg * e2 / (1.0 + e2)
    sel_ref[...] = jnp.where(lane == 0, j1 - N_GROUPS, jnp.where(lane == 1, j2 - N_GROUPS, 0))
    wts_ref[:, 0:LANES] = jnp.broadcast_to(w1, (tm, LANES))
    wts_ref[:, LANES:2 * LANES] = jnp.broadcast_to(w2, (tm, LANES))


def _router(x, g, w_group, b_group, w_router, b_router, tm=256):
    m, d = x.shape
    pad = LANES - N_GROUPS - N_EXPERTS
    w_rt = jnp.concatenate([w_group, w_router, jnp.zeros((d, pad), F32)], axis=1)
    w_hi = w_rt.astype(BF16)
    w_lo = (w_rt - w_hi.astype(F32)).astype(BF16)
    b_rt = jnp.concatenate([b_group.astype(F32), b_router.astype(F32), jnp.zeros((pad,), F32)])
    return pl.pallas_call(
        _router_kernel,
        grid=(m // tm,),
        in_specs=[pl.BlockSpec((tm, d), lambda i: (i, 0)),
                  pl.BlockSpec((1, d), lambda i: (0, 0)),
                  pl.BlockSpec((d, LANES), lambda i: (0, 0)),
                  pl.BlockSpec((d, LANES), lambda i: (0, 0)),
                  pl.BlockSpec((1, LANES), lambda i: (0, 0))],
        out_specs=[pl.BlockSpec((tm, LANES), lambda i: (i, 0)),
                   pl.BlockSpec((tm, 2 * LANES), lambda i: (i, 0))],
        out_shape=[jax.ShapeDtypeStruct((m, LANES), jnp.int32),
                   jax.ShapeDtypeStruct((m, 2 * LANES), F32)],
        compiler_params=_params("parallel"),
        name="moe_router",
    )(x, g.reshape(1, d), w_hi, w_lo, b_rt.reshape(1, LANES))


GATHER_UNROLL = 8


def _issue_row_gather(idx_ref, src_hbm, dst_ref, sem, n_rows):
    def issue(i, carry):
        row = idx_ref[0, 0, i]
        pltpu.make_async_copy(src_hbm.at[pl.ds(row, 1)], dst_ref.at[pl.ds(i, 1)], sem).start()
        return carry
    lax.fori_loop(0, n_rows, issue, 0, unroll=8)


def _wait_row_gather(src_hbm, dst_ref, sem, n_rows):
    pltpu.make_async_copy(src_hbm.at[pl.ds(0, n_rows)], dst_ref, sem).wait()


def _gather_norm_kernel(nv_ref, idx_ref, idx_next_ref, x_hbm, g_ref, o_ref, xs_scr, sems, *, tm, d):
    r = pl.program_id(0)
    nv = nv_ref[0]
    slot = r % 2
    nc = d // LANES

    def issue(idx, s):
        def group(i8, carry):
            for u in range(GATHER_UNROLL):
                i = i8 * GATHER_UNROLL + u
                pltpu.make_async_copy(x_hbm.at[idx[0, 0, i]], xs_scr.at[s, :, i, :],
                                      sems.at[s]).start(priority=u % 2)
            return carry
        lax.fori_loop(0, tm // GATHER_UNROLL, group, 0)

    @pl.when(r == 0)
    def _():
        issue(idx_ref, 0)

    @pl.when(r + 1 < nv)
    def _():
        issue(idx_next_ref, 1 - slot)

    @pl.when(r < nv)
    def _():
        done = xs_scr.at[slot, :, 0:tm, :]
        pltpu.make_async_copy(done, done, sems.at[slot]).wait()
        ssq = jnp.zeros((tm, LANES), F32)
        for c in range(nc):
            xc = xs_scr[slot, c, 0:tm, :]
            ssq = ssq + xc * xc
        inv = lax.rsqrt(jnp.sum(ssq, axis=-1, keepdims=True) / d + RMS_EPS)
        for c in range(nc):
            cs = slice(c * LANES, (c + 1) * LANES)
            o_ref[:, cs] = (xs_scr[slot, c, 0:tm, :] * inv * g_ref[:, cs]).astype(o_ref.dtype)

    @pl.when(r >= nv)
    def _():
        o_ref[...] = jnp.zeros_like(o_ref)


def _gather_norm(x, g, buf_tok, nvalid, tm):
    n, d = x.shape
    nb = buf_tok.shape[0] // tm
    idx = buf_tok.reshape(nb, 1, tm)
    x3 = x.reshape(n, d // LANES, LANES)
    last = lambda r, nv: jnp.minimum(r, nv[0] - 1)
    return pl.pallas_call(
        functools.partial(_gather_norm_kernel, tm=tm, d=d),
        grid_spec=pltpu.PrefetchScalarGridSpec(
            num_scalar_prefetch=1,
            grid=(nb,),
            in_specs=[pl.BlockSpec((1, 1, tm), lambda r, nv: (last(r, nv), 0, 0),
                                   memory_space=pltpu.SMEM),
                      pl.BlockSpec((1, 1, tm), lambda r, nv: (last(r + 1, nv), 0, 0),
                                   memory_space=pltpu.SMEM),
                      pl.BlockSpec(memory_space=pl.ANY),
                      pl.BlockSpec((1, d), lambda r, nv: (0, 0))],
            out_specs=pl.BlockSpec((tm, d), lambda r, nv: (r, 0)),
            scratch_shapes=[pltpu.VMEM((2, d // LANES, tm + SUBLANES, LANES), F32),
                            pltpu.SemaphoreType.DMA((2,))],
        ),
        out_shape=jax.ShapeDtypeStruct((nb * tm, d), BF16),
        compiler_params=_params("arbitrary"),
        name="moe_gather_norm",
    )(nvalid, idx, idx, x3, g.reshape(1, d))


def _combine_kernel(idx_ref, idx_next_ref, x_ref, w_ref, y_hbm, *rest, tm, with_norm):
    if with_norm:
        g_ref, o_ref, h_ref, ys_scr, sems = rest
    else:
        o_ref, ys_scr, sems = rest
    r = pl.program_id(0)
    nr = pl.num_programs(0)
    slot = r % 2
    n_rows = TOP_K * tm

    @pl.when(r == 0)
    def _():
        _issue_row_gather(idx_ref, y_hbm, ys_scr.at[0], sems.at[0], n_rows)

    @pl.when(r + 1 < nr)
    def _():
        _issue_row_gather(idx_next_ref, y_hbm, ys_scr.at[1 - slot], sems.at[1 - slot], n_rows)

    _wait_row_gather(y_hbm, ys_scr.at[slot], sems.at[slot], n_rows)
    w0 = w_ref[:, 0:LANES]
    w1 = w_ref[:, LANES:2 * LANES]
    d = x_ref.shape[1]
    ssq = jnp.zeros((tm, LANES), F32)
    for c in range(d // LANES):
        cs = slice(c * LANES, (c + 1) * LANES)
        o = x_ref[:, cs] + (w0 * ys_scr[slot, 0:tm, cs] + w1 * ys_scr[slot, tm:2 * tm, cs])
        o_ref[:, cs] = o
        if with_norm:
            ssq = ssq + o * o
    if with_norm:
        inv = lax.rsqrt(jnp.sum(ssq, axis=-1, keepdims=True) / d + RMS_EPS)
        h_ref[...] = (o_ref[...] * inv * g_ref[...]).astype(h_ref.dtype)


def _combine(x, w_exp, yb, dest, next_norm_g, tm=128):
    n, d = x.shape
    nr = n // tm
    with_norm = next_norm_g is not None
    in_specs = [pl.BlockSpec((1, 1, TOP_K * tm), lambda r: (r, 0, 0), memory_space=pltpu.SMEM),
                pl.BlockSpec((1, 1, TOP_K * tm), lambda r: (jnp.minimum(r + 1, nr - 1), 0, 0),
                             memory_space=pltpu.SMEM),
                pl.BlockSpec((tm, d), lambda r: (r, 0)),
                pl.BlockSpec((tm, 2 * LANES), lambda r: (r, 0)),
                pl.BlockSpec(memory_space=pl.ANY)]
    args = [dest, dest, x, w_exp, yb]
    out_specs = [pl.BlockSpec((tm, d), lambda r: (r, 0))]
    out_shape = [jax.ShapeDtypeStruct((n, d), F32)]
    if with_norm:
        in_specs.append(pl.BlockSpec((1, d), lambda r: (0, 0)))
        args.append(next_norm_g.reshape(1, d))
        out_specs.append(pl.BlockSpec((tm, d), lambda r: (r, 0)))
        out_shape.append(jax.ShapeDtypeStruct((n, d), BF16))
    res = pl.pallas_call(
        functools.partial(_combine_kernel, tm=tm, with_norm=with_norm),
        grid=(nr,),
        in_specs=in_specs,
        out_specs=out_specs,
        out_shape=out_shape,
        scratch_shapes=[pltpu.VMEM((2, TOP_K * tm, d), F32), pltpu.SemaphoreType.DMA((2,))],
        compiler_params=_params("arbitrary"),
        name="moe_combine",
    )(*args)
    return (res[0], res[1]) if with_norm else (res[0], None)


def _ffn_up_kernel(be_ref, nv_ref, first_ref, run_ref, nxt_ref, nruns_ref, x_ref, wg_hbm, wu_hbm,
                   o_ref, stage, w_scr, sems, *, li, tf, nf):
    j = pl.program_id(0)
    r = pl.program_id(1)
    valid = r < nv_ref[0]
    cols = lambda jj: pl.ds(pl.multiple_of(jj * tf, LANES), tf)

    def copies(e, jj, s):
        return (pltpu.make_async_copy(wg_hbm.at[li, e, :, cols(jj)], stage.at[s, 0], sems.at[s]),
                pltpu.make_async_copy(wu_hbm.at[li, e, :, cols(jj)], stage.at[s, 1], sems.at[s]))

    @pl.when((j == 0) & (r == 0))
    def _():
        for cp in copies(be_ref[0], 0, 0):
            cp.start()

    @pl.when(valid & (first_ref[r] == 1))
    def _():
        s = (j * nruns_ref[0] + run_ref[r]) % 2
        nxt = nxt_ref[r]

        @pl.when(nxt >= 0)
        def _():
            for cp in copies(nxt, j, 1 - s):
                cp.start()

        @pl.when((nxt < 0) & (j + 1 < nf))
        def _():
            for cp in copies(be_ref[0], j + 1, 1 - s):
                cp.start()

        for cp in copies(be_ref[r], j, s):
            cp.wait()
        w_scr[:, 0:tf] = stage[s, 0].astype(BF16)
        w_scr[:, tf:2 * tf] = stage[s, 1].astype(BF16)

    @pl.when(valid)
    def _():
        gu = jnp.dot(x_ref[...], w_scr[...], preferred_element_type=F32)
        o_ref[...] = (jax.nn.silu(gu[:, 0:tf]) * gu[:, tf:2 * tf]).astype(o_ref.dtype)

    @pl.when(jnp.logical_not(valid))
    def _():
        o_ref[...] = jnp.zeros_like(o_ref)


def _ffn_down_kernel(be_ref, nv_ref, first_ref, run_ref, nxt_ref, a_ref, wd_hbm, o_ref,
                     stage, w_scr, sems, *, li):
    r = pl.program_id(0)
    valid = r < nv_ref[0]
    copy = lambda e, s: pltpu.make_async_copy(wd_hbm.at[li, e], stage.at[s], sems.at[s])

    @pl.when(r == 0)
    def _():
        copy(be_ref[0], 0).start()

    @pl.when(valid & (first_ref[r] == 1))
    def _():
        s = run_ref[r] % 2
        nxt = nxt_ref[r]

        @pl.when(nxt >= 0)
        def _():
            copy(nxt, 1 - s).start()

        copy(be_ref[r], s).wait()
        w_scr[...] = stage[s].astype(BF16)

    @pl.when(valid)
    def _():
        o_ref[...] = jnp.dot(a_ref[...], w_scr[...], preferred_element_type=F32)

    @pl.when(jnp.logical_not(valid))
    def _():
        o_ref[...] = jnp.zeros_like(o_ref)


def _expert_ffn(xg, blk_e, nvalid, w_gate, w_up, w_down, li, tm, tf=384):
    p, d = xg.shape
    f = w_gate.shape[3]
    nb = p // tm
    nf = f // tf
    tiles = jnp.arange(nb, dtype=jnp.int32)
    valid = tiles < nvalid[0]
    first = valid & ((tiles == 0) | (blk_e != jnp.roll(blk_e, 1)))
    run = (jnp.cumsum(first.astype(jnp.int32)) - 1).astype(jnp.int32)
    nruns = jnp.sum(first.astype(jnp.int32)).reshape(1)
    run_e = jnp.sum(jnp.where(first[None, :] & (run[None, :] == tiles[:, None]), blk_e[None, :], 0),
                    axis=1)
    nxt = jnp.sum(jnp.where(tiles[None, :] == run[:, None] + 1, run_e[None, :], 0), axis=1)
    nxt = jnp.where(run + 1 < nruns[0], nxt, -1).astype(jnp.int32)
    first = first.astype(jnp.int32)
    last = lambda r, nv: jnp.minimum(r, nv[0] - 1)
    act = pl.pallas_call(
        functools.partial(_ffn_up_kernel, li=li, tf=tf, nf=nf),
        grid_spec=pltpu.PrefetchScalarGridSpec(
            num_scalar_prefetch=6,
            grid=(nf, nb),
            in_specs=[pl.BlockSpec((tm, d), lambda j, r, be, nv, *_: (last(r, nv), 0)),
                      pl.BlockSpec(memory_space=pl.ANY),
                      pl.BlockSpec(memory_space=pl.ANY)],
            out_specs=pl.BlockSpec((tm, tf), lambda j, r, *_: (r, j)),
            scratch_shapes=[pltpu.VMEM((2, 2, d, tf), F32), pltpu.VMEM((d, 2 * tf), BF16),
                            pltpu.SemaphoreType.DMA((2,))],
        ),
        out_shape=jax.ShapeDtypeStruct((p, f), BF16),
        compiler_params=_params("arbitrary", "arbitrary"),
        name="moe_ffn_up",
    )(blk_e, nvalid, first, run, nxt, nruns, xg, w_gate, w_up)
    return pl.pallas_call(
        functools.partial(_ffn_down_kernel, li=li),
        grid_spec=pltpu.PrefetchScalarGridSpec(
            num_scalar_prefetch=5,
            grid=(nb,),
            in_specs=[pl.BlockSpec((tm, f), lambda r, be, nv, *_: (last(r, nv), 0)),
                      pl.BlockSpec(memory_space=pl.ANY)],
            out_specs=pl.BlockSpec((tm, d), lambda r, *_: (r, 0)),
            scratch_shapes=[pltpu.VMEM((2, f, d), F32), pltpu.VMEM((f, d), BF16),
                            pltpu.SemaphoreType.DMA((2,))],
        ),
        out_shape=jax.ShapeDtypeStruct((p, d), F32),
        compiler_params=_params("arbitrary"),
        name="moe_ffn_down",
    )(blk_e, nvalid, first, run, nxt, act, w_down)


def _hier_moe(x, norm_g, w_group, b_group, w_router, b_router, w_gate, w_up, w_down, li,
              next_norm_g, tm=MOE_TM, tc=128):
    n_tok, d = x.shape
    sel, w_exp = _router(x, norm_g, w_group, b_group, w_router, b_router)

    n_as = n_tok * TOP_K
    nb = -(-n_as // tm) + N_EXPERTS
    p_rows = nb * tm
    n_pad = p_rows - n_as
    flat_e = sel[:, :TOP_K].reshape(n_as)
    experts = jnp.arange(N_EXPERTS, dtype=jnp.int32)
    counts = jnp.sum((flat_e[:, None] == experts[None, :]).astype(jnp.int32), axis=0)
    pcounts = (counts + tm - 1) // tm * tm
    nvalid = (jnp.sum(pcounts) // tm).astype(jnp.int32).reshape(1)
    pad_end = jnp.cumsum(pcounts - counts)
    pad_e = jnp.sum((jnp.arange(n_pad, dtype=jnp.int32)[:, None] >= pad_end[None, :]).astype(jnp.int32),
                    axis=1)
    keys = jnp.concatenate([2 * flat_e, 2 * pad_e + 1])
    vals = jnp.concatenate([jnp.arange(n_as, dtype=jnp.int32), jnp.full((n_pad,), -1, jnp.int32)])
    skeys, svals = lax.sort((keys, vals), num_keys=1)
    buf_tok = jnp.where(svals >= 0, svals // TOP_K, 0)
    _, pos = lax.sort((svals, jnp.arange(p_rows, dtype=jnp.int32)), num_keys=1)
    dest = pos[n_pad:].reshape(n_tok, TOP_K)
    blk_e = jnp.minimum(skeys.reshape(nb, tm)[:, 0] // 2, N_EXPERTS - 1).astype(jnp.int32)

    xg = _gather_norm(x, norm_g, buf_tok, nvalid, tm)
    yb = _expert_ffn(xg, blk_e, nvalid, w_gate, w_up, w_down, li, tm)
    dest_t = dest.reshape(n_tok // tc, tc, TOP_K).transpose(0, 2, 1).reshape(n_tok // tc, 1, TOP_K * tc)
    return _combine(x, w_exp, yb, dest_t, next_norm_g, tm=tc)


def _ab_layer(x, h, w_in, li, conv_w, conv_b, wa, ba, wi, bi, lam, bf, qn, kn, w_out, bsz):
    n_tok, d = x.shape
    t = n_tok // bsz
    d_rnn = conv_w.shape[1]
    nh = bf.shape[0]
    fw = nh * HEAD_DIM
    zga = _proj_f32(h, w_in, li, 0, 2 * d_rnn, wt=True).reshape(bsz, t, 2 * d_rnn)
    ya = _rglru(zga, bsz, conv_w, conv_b, wa, ba, wi, bi, lam)
    normw = jnp.concatenate([jnp.tile(qn, nh), jnp.tile(kn * FOX_QK_SCALE, nh),
                             jnp.ones((fw,), F32)]).reshape(1, 3 * fw)
    qkv = _proj_heads(h, w_in, li, 2 * d_rnn, 3 * fw, normw, 2 * fw, bsz, 1, wt=True)
    b_f = jnp.pad(bf.astype(F32), (0, LANES - nh)).reshape(1, LANES)
    cum = _forget_cum(h, w_in, li, 2 * d_rnn + 3 * fw, nh, b_f, bsz, wt=True)
    fox_tq = 512
    cum_row = cum[:, :, :nh].transpose(0, 2, 1).reshape(bsz, nh, t // fox_tq, fox_tq)
    yb = _fox_attention(qkv, cum, cum_row, tq=fox_tq)
    return _proj_res([ya.reshape(n_tok, d_rnn), yb.reshape(n_tok, fw)], w_out, li, x)


def _dilated_layer(x, h, w_in, li, qn, kn, table, w_out, bsz):
    n_tok, d = x.shape
    t = n_tok // bsz
    nh = table.shape[1] // len(DSW_PATTERNS)
    fw = nh * HEAD_DIM
    os_, lses = [], []
    for g, (window, dil) in enumerate(DSW_PATTERNS):
        normw = jnp.concatenate([jnp.tile(qn[g], nh), jnp.tile(kn[g], nh),
                                 jnp.ones((fw,), F32)]).reshape(1, 3 * fw)
        qkv = _proj_heads(h, w_in, li, g * 3 * fw, 3 * fw, normw, 2 * fw, bsz, dil)
        bias = _dsw_bias(table[:, g * nh:(g + 1) * nh], window, dil)
        o, lse = _dsw_group(qkv, bias, min(t // dil, 1024))
        os_.append(o)
        lses.append(lse)
    y = _dsw_merge(os_, lses)
    return _proj_res([y], w_out, li, x)


def kernel(x, mix_norm, ffn_norm, w_in_ab, conv_w, conv_b, lru_wa, lru_ba, lru_wi, lru_bi,
           lru_lambda, fox_bf, fox_qnorm, fox_knorm, w_out_ab, w_in_c, dsw_qnorm, dsw_knorm,
           w_out_c, rel_bias, moe_w_group, moe_b_group, moe_w_router, moe_b_router,
           moe_w_gate, moe_w_up, moe_w_down):
    bsz, t, d = x.shape
    depth = mix_norm.shape[0]
    xf = x.reshape(bsz * t, d)
    h = _rmsnorm(xf, mix_norm[0])
    for layer in range(depth):
        i = layer // 2
        if layer % 2 == 0:
            xf = _ab_layer(xf, h, jnp.swapaxes(w_in_ab, 1, 2), i, conv_w[i], conv_b[i], lru_wa[i],
                           lru_ba[i], lru_wi[i], lru_bi[i], lru_lambda[i], fox_bf[i],
                           fox_qnorm[i], fox_knorm[i], w_out_ab, bsz)
        else:
            xf = _dilated_layer(xf, h, w_in_c, i, dsw_qnorm[i], dsw_knorm[i],
                                rel_bias, w_out_c, bsz)
        next_g = mix_norm[layer + 1] if layer + 1 < depth else None
        xf, h = _hier_moe(xf, ffn_norm[layer], moe_w_group[layer], moe_b_group[layer],
                          moe_w_router[layer], moe_b_router[layer], moe_w_gate, moe_w_up,
                          moe_w_down, layer, next_g)
    return xf.reshape(bsz, t, d)
```

```python
import functools
import math

import jax
import jax.numpy as jnp
from jax import lax
from jax.experimental import pallas as pl
from jax.experimental.pallas import tpu as pltpu

F32 = jnp.float32
BF16 = jnp.bfloat16

LANES = 128
SUBLANES = 8
VMEM_LIMIT = 56 * 1024 * 1024

HEAD_DIM = 128
RMS_EPS = 1e-6
RNN_BLOCKS = 16
CONV_WIDTH = 4
LRU_C = 8.0
DSW_PATTERNS = ((128, 1), (512, 4), (2048, 16))
DSW_BLOCK = 128
DSW_CHAINS = 8
DSW_SKEW = 2
N_BUCKETS = 32
MAX_DISTANCE = 2048
N_GROUPS = 4
EXPERTS_PER_GROUP = 8
N_EXPERTS = N_GROUPS * EXPERTS_PER_GROUP
TOP_K = 2
LOG2E = math.log2(math.e)

MOE_TM = 256
PROJ_SUB = 256
MAX_ROW_STRIDE = 4


def _params(*sem):
    return pltpu.CompilerParams(dimension_semantics=sem, vmem_limit_bytes=VMEM_LIMIT)


def _rmsnorm_kernel(x_ref, g_ref, o_ref):
    x = x_ref[...]
    ms = jnp.mean(x * x, axis=-1, keepdims=True)
    o_ref[...] = (x * lax.rsqrt(ms + RMS_EPS) * g_ref[...]).astype(o_ref.dtype)


def _rmsnorm(x, g, tm=256):
    m, d = x.shape
    return pl.pallas_call(
        _rmsnorm_kernel,
        grid=(m // tm,),
        in_specs=[pl.BlockSpec((tm, d), lambda i: (i, 0)),
                  pl.BlockSpec((1, d), lambda i: (0, 0))],
        out_specs=pl.BlockSpec((tm, d), lambda i: (i, 0)),
        out_shape=jax.ShapeDtypeStruct((m, d), BF16),
        compiler_params=_params("parallel"),
        name="rmsnorm",
    )(x, g.reshape(1, d))


def _mm(a, w_blk, wt):
    w = w_blk.astype(BF16)
    if wt:
        return lax.dot_general(a, w, (((1,), (1,)), ((), ())), preferred_element_type=F32)
    return jnp.dot(a, w, preferred_element_type=F32)


def _w_spec(k, tn, li, jb, wt, grid_rank):
    if wt:
        return pl.BlockSpec((None, tn, k), lambda *g: (li, jb + g[grid_rank - 1], 0))
    return pl.BlockSpec((None, k, tn), lambda *g: (li, 0, jb + g[grid_rank - 1]))


def _proj_f32_kernel(a_ref, w_ref, o_ref, *, wt):
    for sb in range(o_ref.shape[1] // PROJ_SUB):
        cs = slice(sb * PROJ_SUB, (sb + 1) * PROJ_SUB)
        o_ref[:, cs] = _mm(a_ref[...], w_ref[cs, :] if wt else w_ref[:, cs], wt)


def _proj_f32(a, w, li, col0, ncols, wt=False, tm=1024, tn=512):
    m, k = a.shape
    jb = col0 // tn
    return pl.pallas_call(
        functools.partial(_proj_f32_kernel, wt=wt),
        grid=(m // tm, ncols // tn),
        in_specs=[pl.BlockSpec((tm, k), lambda i, j: (i, 0)),
                  _w_spec(k, tn, li, jb, wt, 2)],
        out_specs=pl.BlockSpec((tm, tn), lambda i, j: (i, j)),
        out_shape=jax.ShapeDtypeStruct((m, ncols), F32),
        compiler_params=_params("parallel", "arbitrary"),
        name="proj_f32",
    )(a, w)


def _proj_res_kernel(*refs, n_in):
    a_refs, w_refs = refs[:n_in], refs[n_in:2 * n_in]
    r_ref, o_ref = refs[2 * n_in], refs[2 * n_in + 1]
    for sb in range(o_ref.shape[1] // PROJ_SUB):
        cs = slice(sb * PROJ_SUB, (sb + 1) * PROJ_SUB)
        acc = r_ref[:, cs]
        for a_ref, w_ref in zip(a_refs, w_refs):
            acc = acc + jnp.dot(a_ref[...], w_ref[:, cs].astype(BF16), preferred_element_type=F32)
        o_ref[:, cs] = acc


def _proj_res(a_list, w, li, res, tm=1024, tn=512):
    m, k = a_list[0].shape
    n_in = len(a_list)
    n = w.shape[2]
    a_specs = [pl.BlockSpec((tm, k), lambda i, j: (i, 0)) for _ in a_list]
    w_specs = [pl.BlockSpec((None, k, tn), lambda i, j, p=p: (li, p, j)) for p in range(n_in)]
    return pl.pallas_call(
        functools.partial(_proj_res_kernel, n_in=n_in),
        grid=(m // tm, n // tn),
        in_specs=a_specs + w_specs + [pl.BlockSpec((tm, tn), lambda i, j: (i, j))],
        out_specs=pl.BlockSpec((tm, tn), lambda i, j: (i, j)),
        out_shape=jax.ShapeDtypeStruct((m, n), F32),
        compiler_params=_params("parallel", "arbitrary"),
        name="proj_res",
    )(*a_list, *([w] * n_in), res)


def _proj_heads_kernel(a_ref, w_ref, nw_ref, o_ref, y_scr, t_scr, *, dil, n_norm_blocks, hpb, tm, wt):
    is_norm = pl.program_id(2) < n_norm_blocks
    rows = tm // dil
    hps = PROJ_SUB // HEAD_DIM
    for sb in range(hpb // hps):
        cs = slice(sb * PROJ_SUB, (sb + 1) * PROJ_SUB)
        acc = _mm(a_ref[...], w_ref[cs, :] if wt else w_ref[:, cs], wt)
        for h2 in range(hps):
            hh = sb * hps + h2
            y = acc[:, h2 * HEAD_DIM:(h2 + 1) * HEAD_DIM]
            ms = jnp.mean(y * y, axis=-1, keepdims=True)
            yn = y * lax.rsqrt(ms + RMS_EPS) * nw_ref[:, hh * HEAD_DIM:(hh + 1) * HEAD_DIM]
            y = jnp.where(is_norm, yn, y)
            if dil == 1:
                o_ref[0, 0, hh] = y.astype(o_ref.dtype)
            else:
                y_scr[hh] = y
        if dil > 1:
            s1 = min(dil, MAX_ROW_STRIDE)
            s2 = dil // s1
            for hh in range(sb * hps, (sb + 1) * hps):
                for z1 in range(s1):
                    if s2 == 1:
                        o_ref[0, z1, hh] = y_scr[hh, pl.ds(z1, rows, stride=s1), :].astype(o_ref.dtype)
                        continue
                    t_scr[z1] = y_scr[hh, pl.ds(z1, tm // s1, stride=s1), :]
                    for z2 in range(s2):
                        o_ref[0, z1 + s1 * z2, hh] = (
                            t_scr[z1, pl.ds(z2, rows, stride=s2), :].astype(o_ref.dtype))


def _proj_heads(a, w, li, col0, ncols, normw, n_norm_cols, bsz, dil, wt=False, tm=1024, tn=512):
    m, k = a.shape
    t = m // bsz
    nt = t // tm
    hpb = tn // HEAD_DIM
    jb = col0 // tn
    kern = functools.partial(_proj_heads_kernel, dil=dil, n_norm_blocks=n_norm_cols // tn,
                             hpb=hpb, tm=tm, wt=wt)
    return pl.pallas_call(
        kern,
        grid=(bsz, nt, ncols // tn),
        in_specs=[pl.BlockSpec((tm, k), lambda b, i, j: (b * nt + i, 0)),
                  _w_spec(k, tn, li, jb, wt, 3),
                  pl.BlockSpec((1, tn), lambda b, i, j: (0, j))],
        out_specs=pl.BlockSpec((1, dil, hpb, tm // dil, HEAD_DIM),
                               lambda b, i, j: (b, 0, j, i, 0)),
        out_shape=jax.ShapeDtypeStruct((bsz, dil, ncols // HEAD_DIM, t // dil, HEAD_DIM), BF16),
        scratch_shapes=[pltpu.VMEM((hpb, tm, HEAD_DIM), F32),
                        pltpu.VMEM((MAX_ROW_STRIDE, tm // MAX_ROW_STRIDE, HEAD_DIM), F32)],
        compiler_params=_params("parallel", "parallel", "arbitrary"),
        name=f"proj_heads_d{dil}",
    )(a, w, normw)


def _forget_cum_kernel(a_ref, w_ref, b_ref, o_ref, carry_ref, *, tm, nh, wt):
    i = pl.program_id(1)

    @pl.when(i == 0)
    def _():
        carry_ref[...] = jnp.zeros_like(carry_ref)

    lane = lax.broadcasted_iota(jnp.int32, (tm, LANES), 1)
    fz = _mm(a_ref[...], w_ref[...], wt)
    c = jax.nn.log_sigmoid(jnp.where(lane < nh, fz + b_ref[...], 0.0))
    row = lax.broadcasted_iota(jnp.int32, c.shape, 0)
    s = 1
    while s < tm:
        c = c + jnp.where(row >= s, pltpu.roll(c, s, 0), 0.0)
        s *= 2
    c = c + carry_ref[0:1, :]
    o_ref[0] = c
    carry_ref[...] = jnp.broadcast_to(c[tm - 1:tm, :], carry_ref.shape)


def _forget_cum(a, w, li, col0, nh, b_f, bsz, wt=False, tm=512):
    m, k = a.shape
    t = m // bsz
    nt = t // tm
    jb = col0 // LANES
    return pl.pallas_call(
        functools.partial(_forget_cum_kernel, tm=tm, nh=nh, wt=wt),
        grid=(bsz, nt),
        in_specs=[pl.BlockSpec((tm, k), lambda b, i: (b * nt + i, 0)),
                  (pl.BlockSpec((None, LANES, k), lambda b, i: (li, jb, 0)) if wt
                   else pl.BlockSpec((None, k, LANES), lambda b, i: (li, 0, jb))),
                  pl.BlockSpec((1, LANES), lambda b, i: (0, 0))],
        out_specs=pl.BlockSpec((1, tm, LANES), lambda b, i: (b, i, 0)),
        out_shape=jax.ShapeDtypeStruct((bsz, t, LANES), F32),
        scratch_shapes=[pltpu.VMEM((SUBLANES, LANES), F32)],
        compiler_params=_params("parallel", "arbitrary"),
        name="forget_cum",
    )(a, w, b_f)


def _gelu_tanh(x):
    return 0.5 * x * (1.0 + jnp.tanh(math.sqrt(2.0 / math.pi) * (x + 0.044715 * (x * x * x))))


def _rglru_kernel(g_ref, x_ref, cw_ref, cb_ref, wa_ref, ba_ref, wi_ref, bi_ref, lam_ref,
                  o_ref, xs_scr, a_scr, u_scr, h_scr, carry_scr, *, tc, nblk):
    ti = pl.program_id(2)
    halo = SUBLANES

    @pl.when(ti == 0)
    def _():
        xs_scr[0:halo, :] = jnp.zeros((halo, xs_scr.shape[1]), F32)
        carry_scr[...] = jnp.zeros_like(carry_scr)

    xs_scr[halo:halo + tc, :] = x_ref[0]
    xc = cb_ref[...] + cw_ref[CONV_WIDTH - 1:CONV_WIDTH, :] * xs_scr[halo:halo + tc, :]
    for j in range(CONV_WIDTH - 1):
        sh = CONV_WIDTH - 1 - j
        xc = xc + cw_ref[j:j + 1, :] * xs_scr[halo - sh:halo - sh + tc, :]
    xs_scr[0:halo, :] = xs_scr[tc:tc + halo, :]

    neg_sp = -LRU_C * jax.nn.softplus(-lam_ref[...])
    for n in range(nblk):
        cs = slice(n * HEAD_DIM, (n + 1) * HEAD_DIM)
        xb = xc[:, cs]
        xbb = xb.astype(BF16)
        r = jax.nn.sigmoid(jnp.dot(xbb, wa_ref[n].astype(BF16), preferred_element_type=F32)
                           + ba_ref[:, cs])
        ig = jax.nn.sigmoid(jnp.dot(xbb, wi_ref[n].astype(BF16), preferred_element_type=F32)
                            + bi_ref[:, cs])
        log_a = neg_sp[:, cs] * r
        a = jnp.exp(log_a)
        a_scr[:, cs] = a
        u_scr[:, cs] = jnp.sqrt(-jnp.tanh(log_a) * (a * a + 1.0)) * (ig * xb)

    row = lax.broadcasted_iota(jnp.int32, (SUBLANES, a_scr.shape[1]), 0)

    def body(g, carry):
        r0 = pl.multiple_of(g * SUBLANES, SUBLANES)
        a = a_scr[pl.ds(r0, SUBLANES), :]
        u = u_scr[pl.ds(r0, SUBLANES), :]
        s = 1
        while s < SUBLANES:
            keep = row >= s
            u = jnp.where(keep, a * pltpu.roll(u, s, 0) + u, u)
            a = jnp.where(keep, a * pltpu.roll(a, s, 0), a)
            s *= 2
        h = a * carry + u
        h_scr[pl.ds(r0, SUBLANES), :] = h
        return jnp.broadcast_to(h[SUBLANES - 1:SUBLANES, :], h.shape)

    carry = lax.fori_loop(0, tc // SUBLANES, body, carry_scr[...], unroll=4)
    carry_scr[...] = carry
    o_ref[0] = (h_scr[...] * _gelu_tanh(g_ref[0])).astype(o_ref.dtype)


def _rglru(zga, bsz, conv_w, conv_b, wa, ba, wi, bi, lam, tc=256, cw=512):
    _, t, c2 = zga.shape
    c = c2 // 2
    nblk = cw // HEAD_DIM
    ncb = c // cw
    vec = lambda v: v.reshape(1, c)
    vspec = pl.BlockSpec((1, cw), lambda b, j, i: (0, j))
    kern = functools.partial(_rglru_kernel, tc=tc, nblk=nblk)
    return pl.pallas_call(
        kern,
        grid=(bsz, ncb, t // tc),
        in_specs=[pl.BlockSpec((1, tc, cw), lambda b, j, i: (b, i, j)),
                  pl.BlockSpec((1, tc, cw), lambda b, j, i: (b, i, ncb + j)),
                  pl.BlockSpec((CONV_WIDTH, cw), lambda b, j, i: (0, j)),
                  vspec,
                  pl.BlockSpec((nblk, HEAD_DIM, HEAD_DIM), lambda b, j, i: (j, 0, 0)),
                  vspec,
                  pl.BlockSpec((nblk, HEAD_DIM, HEAD_DIM), lambda b, j, i: (j, 0, 0)),
                  vspec, vspec],
        out_specs=pl.BlockSpec((1, tc, cw), lambda b, j, i: (b, i, j)),
        out_shape=jax.ShapeDtypeStruct((bsz, t, c), BF16),
        scratch_shapes=[pltpu.VMEM((tc + SUBLANES, cw), F32),
                        pltpu.VMEM((tc, cw), F32),
                        pltpu.VMEM((tc, cw), F32),
                        pltpu.VMEM((tc, cw), F32),
                        pltpu.VMEM((SUBLANES, cw), F32)],
        compiler_params=_params("parallel", "parallel", "arbitrary"),
        name="rglru",
    )(zga, zga, conv_w, vec(conv_b), wa, vec(ba), wi, vec(bi), vec(lam))


FOX_STRIP = 128
FOX_QK_SCALE = LOG2E / math.sqrt(HEAD_DIM)
FOX_SKEW = 3


def _fox_kernel(q_ref, k_ref, v_ref, cq_ref, ck_ref, o_ref, m_scr, l_scr, acc_scr, cq_scr, *, tq, hp):
    g = pl.program_id(1)
    qi = pl.program_id(2)
    tk = tq
    sr = FOX_STRIP
    lane = lax.broadcasted_iota(jnp.int32, (tq, LANES), 1)
    for hh in range(hp):
        cq = jnp.sum(jnp.where(lane == g * hp + hh, cq_ref[0], 0.0), axis=-1, keepdims=True)
        cq_scr[hh] = jnp.broadcast_to(cq * LOG2E, (tq, LANES))
        m_scr[hh] = jnp.full(m_scr.shape[1:], -jnp.inf, F32)
        l_scr[hh] = jnp.zeros(l_scr.shape[1:], F32)
        acc_scr[hh] = jnp.zeros(acc_scr.shape[1:], F32)
    tri = (lax.broadcasted_iota(jnp.int32, (sr, LANES), 1)
           <= lax.broadcasted_iota(jnp.int32, (sr, LANES), 0))

    def scores(hh, r, j, diagonal):
        rs = slice(r * sr, (r + 1) * sr)
        k0 = pl.multiple_of(j * tk, tk)
        nk = (r + 1) * sr if diagonal else tk
        return lax.dot_general(q_ref[0, 0, hh, rs, :], k_ref[0, 0, hh, pl.ds(k0, nk), :],
                               (((1,), (1,)), ((), ())), preferred_element_type=F32)

    def softmax_pv(s, hh, r, j, ck, diagonal):
        rs = slice(r * sr, (r + 1) * sr)
        k0 = pl.multiple_of(j * tk, tk)
        nk = s.shape[1]
        cq = cq_scr[hh, rs, :]
        parts = []
        for c in range(nk // LANES):
            cs = slice(c * LANES, (c + 1) * LANES)
            part = s[:, cs] + (cq - ck[:, cs])
            if diagonal and c == r:
                part = jnp.where(tri, part, -jnp.inf)
            parts.append(part)
        m_prev = m_scr[hh, rs, :]
        m_new = jnp.maximum(m_prev, jnp.max(functools.reduce(jnp.maximum, parts),
                                            axis=-1, keepdims=True))
        alpha = jnp.exp2(m_prev - m_new)
        ps = [jnp.exp2(part - m_new) for part in parts]
        row_sum = jnp.sum(functools.reduce(lambda a, b: a + b, ps), axis=-1, keepdims=True)
        l_scr[hh, rs, :] = alpha * l_scr[hh, rs, :] + row_sum
        p = jnp.concatenate(ps, axis=1).astype(BF16)
        acc_scr[hh, rs, :] = alpha * acc_scr[hh, rs, :] + jnp.dot(
            p, v_ref[0, 0, hh, pl.ds(k0, nk), :], preferred_element_type=F32)
        m_scr[hh, rs, :] = m_new

    def kv_step(j, diagonal):
        cks = [ck_ref[0, hh, pl.ds(j, 1), :] * LOG2E for hh in range(hp)]
        strips = [(hh, r) for hh in range(hp) for r in range(tq // sr)]
        pending = [scores(*strips[n], j, diagonal) for n in range(FOX_SKEW)]
        for n, (hh, r) in enumerate(strips):
            if n + FOX_SKEW < len(strips):
                pending.append(scores(*strips[n + FOX_SKEW], j, diagonal))
            softmax_pv(pending.pop(0), hh, r, j, cks[hh], diagonal)

    def below_diagonal(j, carry):
        kv_step(j, False)
        return carry

    lax.fori_loop(0, qi, below_diagonal, 0)
    kv_step(qi, True)
    for hh in range(hp):
        o_ref[0, :, hh * HEAD_DIM:(hh + 1) * HEAD_DIM] = (acc_scr[hh] / l_scr[hh]).astype(o_ref.dtype)


def _fox_attention(qkv, cum_col, cum_row, tq=512, hp=2):
    bsz, _, h3, t, dh = qkv.shape
    nh = h3 // 3
    nq = t // tq
    ng = nh // hp
    kern = functools.partial(_fox_kernel, tq=tq, hp=hp)
    return pl.pallas_call(
        kern,
        grid=(bsz, ng, nq),
        in_specs=[pl.BlockSpec((1, 1, hp, tq, dh), lambda b, g, i: (b, 0, g, i, 0)),
                  pl.BlockSpec((1, 1, hp, t, dh), lambda b, g, i: (b, 0, ng + g, 0, 0)),
                  pl.BlockSpec((1, 1, hp, t, dh), lambda b, g, i: (b, 0, 2 * ng + g, 0, 0)),
                  pl.BlockSpec((1, tq, LANES), lambda b, g, i: (b, i, 0)),
                  pl.BlockSpec((1, hp, nq, tq), lambda b, g, i: (b, g, 0, 0))],
        out_specs=pl.BlockSpec((1, tq, hp * dh), lambda b, g, i: (b, i, g)),
        out_shape=jax.ShapeDtypeStruct((bsz, t, nh * dh), BF16),
        scratch_shapes=[pltpu.VMEM((hp, tq, LANES), F32), pltpu.VMEM((hp, tq, LANES), F32),
                        pltpu.VMEM((hp, tq, dh), F32), pltpu.VMEM((hp, tq, LANES), F32)],
        compiler_params=_params("parallel", "parallel", "arbitrary"),
        name="fox_attention",
    )(qkv, qkv, qkv, cum_col, cum_row)


def _dsw_kernel(q_ref, kc_ref, kp_ref, vc_ref, vp_ref, bias_ref, o_ref, lse_ref, *, lc, nh):
    c = pl.program_id(2)
    blk = DSW_BLOCK
    nb = lc // blk
    lane = lax.broadcasted_iota(jnp.int32, (blk, LANES), 1)
    col = lax.broadcasted_iota(jnp.int32, (blk, 2 * blk), 1)
    scale = 1.0 / math.sqrt(HEAD_DIM)
    lse_ref[...] = jnp.zeros_like(lse_ref)

    def scores(h, n):
        rs = slice(n * blk, (n + 1) * blk)
        if n == 0:
            kprev = kp_ref[0, 0, h]
        else:
            kprev = kc_ref[0, 0, h, (n - 1) * blk:n * blk, :]
        kk = jnp.concatenate([kprev, kc_ref[0, 0, h, rs, :]], axis=0)
        return lax.dot_general(q_ref[0, 0, h, rs, :], kk, (((1,), (1,)), ((), ())),
                               preferred_element_type=F32)

    def softmax_pv(s, h, n):
        rs = slice(n * blk, (n + 1) * blk)
        if n == 0:
            vprev = vp_ref[0, 0, h]
        else:
            vprev = vc_ref[0, 0, h, (n - 1) * blk:n * blk, :]
        vv = jnp.concatenate([vprev, vc_ref[0, 0, h, rs, :]], axis=0)
        s = s * scale + bias_ref[h]
        if n == 0:
            s = jnp.where((c > 0) | (col >= blk), s, -jnp.inf)
        m = jnp.max(s, axis=-1, keepdims=True)
        e = jnp.exp(s - m)
        den = jnp.sum(e, axis=-1, keepdims=True)
        o = jnp.dot(e.astype(BF16), vv, preferred_element_type=F32) / den
        o_ref[0, 0, rs, pl.ds(pl.multiple_of(h * HEAD_DIM, HEAD_DIM), HEAD_DIM)] = (
            o.astype(o_ref.dtype))
        lse = m + jnp.log(den)
        lse_ref[0, 0, rs, :] = jnp.where(lane == h, lse, lse_ref[0, 0, rs, :])

    hu = math.gcd(nh, max(1, DSW_CHAINS // nb))

    def heads(i, carry):
        blocks = [(i * hu + hh, n) for hh in range(hu) for n in range(nb)]
        pending = [scores(*blocks[k]) for k in range(DSW_SKEW)]
        for k, (h, n) in enumerate(blocks):
            if k + DSW_SKEW < len(blocks):
                pending.append(scores(*blocks[k + DSW_SKEW]))
            softmax_pv(pending.pop(0), h, n)
        return carry

    lax.fori_loop(0, nh // hu, heads, 0)


def _dsw_group(qkv, bias, lc):
    bsz, dil, h3, l, dh = qkv.shape
    nh = h3 // 3
    nc = l // lc
    bpc = lc // DSW_BLOCK
    kern = functools.partial(_dsw_kernel, lc=lc, nh=nh)
    cur = lambda off: pl.BlockSpec((1, 1, nh, lc, dh), lambda b, z, c: (b, z, off, c, 0))
    prev = lambda off: pl.BlockSpec((1, 1, nh, DSW_BLOCK, dh),
                                    lambda b, z, c: (b, z, off, jnp.maximum(c * bpc - 1, 0), 0))
    return pl.pallas_call(
        kern,
        grid=(bsz, dil, nc),
        in_specs=[cur(0), cur(1), prev(1), cur(2), prev(2),
                  pl.BlockSpec((nh, DSW_BLOCK, 2 * DSW_BLOCK), lambda b, z, c: (0, 0, 0))],
        out_specs=[pl.BlockSpec((1, 1, lc, nh * dh), lambda b, z, c: (b, z, c, 0)),
                   pl.BlockSpec((1, 1, lc, LANES), lambda b, z, c: (b, z, c, 0))],
        out_shape=[jax.ShapeDtypeStruct((bsz, dil, l, nh * dh), BF16),
                   jax.ShapeDtypeStruct((bsz, dil, l, LANES), F32)],
        compiler_params=_params("parallel", "parallel", "arbitrary"),
        name=f"dsw_attention_d{dil}",
    )(qkv, qkv, qkv, qkv, qkv, bias)


def _dsw_merge_kernel(*refs, dils, tm, nh):
    ng = len(dils)
    o_refs = refs[:ng]
    l_refs = refs[ng:2 * ng]
    out_ref = refs[2 * ng]
    o_scr = refs[2 * ng + 1:2 * ng + 1 + ng]
    l_scr = refs[2 * ng + 1 + ng:]
    for g, dil in enumerate(dils):
        rows = tm // dil
        for z in range(dil):
            rsel = slice(None) if dil == 1 else pl.ds(z, rows, stride=dil)
            l_scr[g][rsel, :] = l_refs[g][0, z]
            for h in range(nh):
                o_scr[g][h, rsel, :] = o_refs[g][0, z, :, h * HEAD_DIM:(h + 1) * HEAD_DIM].astype(F32)
    lses = [l_scr[g][...] for g in range(ng)]
    m = functools.reduce(jnp.maximum, lses)
    ws = [jnp.exp(l - m) for l in lses]
    inv = 1.0 / functools.reduce(lambda a, b: a + b, ws)
    for h in range(nh):
        acc = None
        for g in range(ng):
            alpha = (ws[g] * inv)[:, h:h + 1]
            term = alpha * o_scr[g][h]
            acc = term if acc is None else acc + term
        out_ref[:, h * HEAD_DIM:(h + 1) * HEAD_DIM] = acc.astype(out_ref.dtype)


def _dsw_merge(os_, lses, tm=512):
    bsz, _, t, w = os_[0].shape
    nh = w // HEAD_DIM
    dils = tuple(o.shape[1] for o in os_)
    t = t * dils[0]
    nt = t // tm
    kern = functools.partial(_dsw_merge_kernel, dils=dils, tm=tm, nh=nh)
    o_specs = [pl.BlockSpec((1, d, tm // d, w), lambda b, i: (b, 0, i, 0)) for d in dils]
    l_specs = [pl.BlockSpec((1, d, tm // d, LANES), lambda b, i: (b, 0, i, 0)) for d in dils]
    return pl.pallas_call(
        kern,
        grid=(bsz, nt),
        in_specs=o_specs + l_specs,
        out_specs=pl.BlockSpec((tm, w), lambda b, i: (b * nt + i, 0)),
        out_shape=jax.ShapeDtypeStruct((bsz * t, w), BF16),
        scratch_shapes=([pltpu.VMEM((nh, tm, HEAD_DIM), F32) for _ in dils]
                        + [pltpu.VMEM((tm, LANES), F32) for _ in dils]),
        compiler_params=_params("parallel", "parallel"),
        name="dsw_merge",
    )(*os_, *lses)


def _t5_bucket(dist):
    max_exact = N_BUCKETS // 2
    large = max_exact + (jnp.log(jnp.maximum(dist, 1).astype(F32) / max_exact)
                         / math.log(MAX_DISTANCE / max_exact) * (N_BUCKETS - max_exact)).astype(jnp.int32)
    return jnp.where(dist < max_exact, dist, jnp.minimum(large, N_BUCKETS - 1))


def _dsw_bias(table, window, dil):
    span = window // dil
    qi = jnp.arange(DSW_BLOCK)[:, None]
    kj = jnp.arange(2 * DSW_BLOCK)[None, :]
    delta = qi + DSW_BLOCK - kj
    band = (delta >= 0) & (delta <= span)
    bucket = _t5_bucket(jnp.maximum(delta, 0) * dil)
    onehot = (bucket[None, :, :] == jnp.arange(N_BUCKETS)[:, None, None])
    table_t = table.astype(F32).T
    bias = jnp.sum(jnp.where(onehot[None], table_t[:, :, None, None], 0.0), axis=1)
    return jnp.where(band[None], bias, -jnp.inf)


def _router_kernel(x_ref, g_ref, whi_ref, wlo_ref, b_ref, sel_ref, wts_ref):
    x = x_ref[...]
    tm = x.shape[0]
    ms = jnp.mean(x * x, axis=-1, keepdims=True)
    h = x * lax.rsqrt(ms + RMS_EPS) * g_ref[...]
    hi = h.astype(BF16)
    lo = (h - hi.astype(F32)).astype(BF16)
    logits = (jnp.dot(hi, whi_ref[...], preferred_element_type=F32)
              + (jnp.dot(hi, wlo_ref[...], preferred_element_type=F32)
                 + jnp.dot(lo, whi_ref[...], preferred_element_type=F32))) + b_ref[...]
    lane = lax.broadcasted_iota(jnp.int32, (tm, LANES), 1)
    first_max = lambda v, vmax: jnp.min(jnp.where(v == vmax, lane, LANES), axis=-1, keepdims=True)

    glog = jnp.where(lane < N_GROUPS, logits, -jnp.inf)
    gmax = jnp.max(glog, axis=-1, keepdims=True)
    gsel = first_max(glog, gmax)
    pg = 1.0 / jnp.sum(jnp.exp(glog - gmax), axis=-1, keepdims=True)

    ex = lane - N_GROUPS
    in_group = (ex >= 0) & (ex < N_EXPERTS) & ((ex // EXPERTS_PER_GROUP) == gsel)
    elog = jnp.where(in_group, logits, -jnp.inf)
    v1 = jnp.max(elog, axis=-1, keepdims=True)
    j1 = first_max(elog, v1)
    elog2 = jnp.where(lane == j1, -jnp.inf, elog)
    v2 = jnp.max(elog2, axis=-1, keepdims=True)
    j2 = first_max(elog2, v2)
    e2 = jnp.exp(v2 - v1)
    w1 = pg / (1.0 + e2)
    w2 = pg * e2 / (1.0 + e2)
    sel_ref[...] = jnp.where(lane == 0, j1 - N_GROUPS, jnp.where(lane == 1, j2 - N_GROUPS, 0))
    wts_ref[:, 0:LANES] = jnp.broadcast_to(w1, (tm, LANES))
    wts_ref[:, LANES:2 * LANES] = jnp.broadcast_to(w2, (tm, LANES))


def _router(x, g, w_group, b_group, w_router, b_router, tm=256):
    m, d = x.shape
    pad = LANES - N_GROUPS - N_EXPERTS
    w_rt = jnp.concatenate([w_group, w_router, jnp.zeros((d, pad), F32)], axis=1)
    w_hi = w_rt.astype(BF16)
    w_lo = (w_rt - w_hi.astype(F32)).astype(BF16)
    b_rt = jnp.concatenate([b_group.astype(F32), b_router.astype(F32), jnp.zeros((pad,), F32)])
    return pl.pallas_call(
        _router_kernel,
        grid=(m // tm,),
        in_specs=[pl.BlockSpec((tm, d), lambda i: (i, 0)),
                  pl.BlockSpec((1, d), lambda i: (0, 0)),
                  pl.BlockSpec((d, LANES), lambda i: (0, 0)),
                  pl.BlockSpec((d, LANES), lambda i: (0, 0)),
                  pl.BlockSpec((1, LANES), lambda i: (0, 0))],
        out_specs=[pl.BlockSpec((tm, LANES), lambda i: (i, 0)),
                   pl.BlockSpec((tm, 2 * LANES), lambda i: (i, 0))],
        out_shape=[jax.ShapeDtypeStruct((m, LANES), jnp.int32),
                   jax.ShapeDtypeStruct((m, 2 * LANES), F32)],
        compiler_params=_params("parallel"),
        name="moe_router",
    )(x, g.reshape(1, d), w_hi, w_lo, b_rt.reshape(1, LANES))


GATHER_UNROLL = 8


def _issue_row_gather(idx_ref, src_hbm, dst_ref, sem, n_rows):
    def issue(i, carry):
        row = idx_ref[0, 0, i]
        pltpu.make_async_copy(src_hbm.at[pl.ds(row, 1)], dst_ref.at[pl.ds(i, 1)], sem).start()
        return carry
    lax.fori_loop(0, n_rows, issue, 0, unroll=8)


def _wait_row_gather(src_hbm, dst_ref, sem, n_rows):
    pltpu.make_async_copy(src_hbm.at[pl.ds(0, n_rows)], dst_ref, sem).wait()


def _gather_norm_kernel(nv_ref, idx_ref, idx_next_ref, x_hbm, g_ref, o_ref, xs_scr, sems, *, tm, d):
    r = pl.program_id(0)
    nv = nv_ref[0]
    slot = r % 2
    nc = d // LANES

    def issue(idx, s):
        def group(i8, carry):
            for u in range(GATHER_UNROLL):
                i = i8 * GATHER_UNROLL + u
                pltpu.make_async_copy(x_hbm.at[idx[0, 0, i]], xs_scr.at[s, :, i, :],
                                      sems.at[s]).start(priority=u % 2)
            return carry
        lax.fori_loop(0, tm // GATHER_UNROLL, group, 0)

    @pl.when(r == 0)
    def _():
        issue(idx_ref, 0)

    @pl.when(r + 1 < nv)
    def _():
        issue(idx_next_ref, 1 - slot)

    @pl.when(r < nv)
    def _():
        done = xs_scr.at[slot, :, 0:tm, :]
        pltpu.make_async_copy(done, done, sems.at[slot]).wait()
        ssq = jnp.zeros((tm, LANES), F32)
        for c in range(nc):
            xc = xs_scr[slot, c, 0:tm, :]
            ssq = ssq + xc * xc
        inv = lax.rsqrt(jnp.sum(ssq, axis=-1, keepdims=True) / d + RMS_EPS)
        for c in range(nc):
            cs = slice(c * LANES, (c + 1) * LANES)
            o_ref[:, cs] = (xs_scr[slot, c, 0:tm, :] * inv * g_ref[:, cs]).astype(o_ref.dtype)

    @pl.when(r >= nv)
    def _():
        o_ref[...] = jnp.zeros_like(o_ref)


def _gather_norm(x, g, buf_tok, nvalid, tm):
    n, d = x.shape
    nb = buf_tok.shape[0] // tm
    idx = buf_tok.reshape(nb, 1, tm)
    x3 = x.reshape(n, d // LANES, LANES)
    last = lambda r, nv: jnp.minimum(r, nv[0] - 1)
    return pl.pallas_call(
        functools.partial(_gather_norm_kernel, tm=tm, d=d),
        grid_spec=pltpu.PrefetchScalarGridSpec(
            num_scalar_prefetch=1,
            grid=(nb,),
            in_specs=[pl.BlockSpec((1, 1, tm), lambda r, nv: (last(r, nv), 0, 0),
                                   memory_space=pltpu.SMEM),
                      pl.BlockSpec((1, 1, tm), lambda r, nv: (last(r + 1, nv), 0, 0),
                                   memory_space=pltpu.SMEM),
                      pl.BlockSpec(memory_space=pl.ANY),
                      pl.BlockSpec((1, d), lambda r, nv: (0, 0))],
            out_specs=pl.BlockSpec((tm, d), lambda r, nv: (r, 0)),
            scratch_shapes=[pltpu.VMEM((2, d // LANES, tm + SUBLANES, LANES), F32),
                            pltpu.SemaphoreType.DMA((2,))],
        ),
        out_shape=jax.ShapeDtypeStruct((nb * tm, d), BF16),
        compiler_params=_params("arbitrary"),
        name="moe_gather_norm",
    )(nvalid, idx, idx, x3, g.reshape(1, d))


def _combine_kernel(idx_ref, idx_next_ref, x_ref, w_ref, y_hbm, *rest, tm, with_norm):
    if with_norm:
        g_ref, o_ref, h_ref, ys_scr, sems = rest
    else:
        o_ref, ys_scr, sems = rest
    r = pl.program_id(0)
    nr = pl.num_programs(0)
    slot = r % 2
    n_rows = TOP_K * tm

    @pl.when(r == 0)
    def _():
        _issue_row_gather(idx_ref, y_hbm, ys_scr.at[0], sems.at[0], n_rows)

    @pl.when(r + 1 < nr)
    def _():
        _issue_row_gather(idx_next_ref, y_hbm, ys_scr.at[1 - slot], sems.at[1 - slot], n_rows)

    _wait_row_gather(y_hbm, ys_scr.at[slot], sems.at[slot], n_rows)
    w0 = w_ref[:, 0:LANES]
    w1 = w_ref[:, LANES:2 * LANES]
    d = x_ref.shape[1]
    ssq = jnp.zeros((tm, LANES), F32)
    for c in range(d // LANES):
        cs = slice(c * LANES, (c + 1) * LANES)
        o = x_ref[:, cs] + (w0 * ys_scr[slot, 0:tm, cs] + w1 * ys_scr[slot, tm:2 * tm, cs])
        o_ref[:, cs] = o
        if with_norm:
            ssq = ssq + o * o
    if with_norm:
        inv = lax.rsqrt(jnp.sum(ssq, axis=-1, keepdims=True) / d + RMS_EPS)
        h_ref[...] = (o_ref[...] * inv * g_ref[...]).astype(h_ref.dtype)


def _combine(x, w_exp, yb, dest, next_norm_g, tm=128):
    n, d = x.shape
    nr = n // tm
    with_norm = next_norm_g is not None
    in_specs = [pl.BlockSpec((1, 1, TOP_K * tm), lambda r: (r, 0, 0), memory_space=pltpu.SMEM),
                pl.BlockSpec((1, 1, TOP_K * tm), lambda r: (jnp.minimum(r + 1, nr - 1), 0, 0),
                             memory_space=pltpu.SMEM),
                pl.BlockSpec((tm, d), lambda r: (r, 0)),
                pl.BlockSpec((tm, 2 * LANES), lambda r: (r, 0)),
                pl.BlockSpec(memory_space=pl.ANY)]
    args = [dest, dest, x, w_exp, yb]
    out_specs = [pl.BlockSpec((tm, d), lambda r: (r, 0))]
    out_shape = [jax.ShapeDtypeStruct((n, d), F32)]
    if with_norm:
        in_specs.append(pl.BlockSpec((1, d), lambda r: (0, 0)))
        args.append(next_norm_g.reshape(1, d))
        out_specs.append(pl.BlockSpec((tm, d), lambda r: (r, 0)))
        out_shape.append(jax.ShapeDtypeStruct((n, d), BF16))
    res = pl.pallas_call(
        functools.partial(_combine_kernel, tm=tm, with_norm=with_norm),
        grid=(nr,),
        in_specs=in_specs,
        out_specs=out_specs,
        out_shape=out_shape,
        scratch_shapes=[pltpu.VMEM((2, TOP_K * tm, d), F32), pltpu.SemaphoreType.DMA((2,))],
        compiler_params=_params("arbitrary"),
        name="moe_combine",
    )(*args)
    return (res[0], res[1]) if with_norm else (res[0], None)


def _ffn_up_kernel(be_ref, nv_ref, first_ref, run_ref, nxt_ref, nruns_ref, x_ref, wg_hbm, wu_hbm,
                   o_ref, stage, w_scr, sems, *, li, tf, nf):
    j = pl.program_id(0)
    r = pl.program_id(1)
    valid = r < nv_ref[0]
    cols = lambda jj: pl.ds(pl.multiple_of(jj * tf, LANES), tf)

    def copies(e, jj, s):
        return (pltpu.make_async_copy(wg_hbm.at[li, e, :, cols(jj)], stage.at[s, 0], sems.at[s]),
                pltpu.make_async_copy(wu_hbm.at[li, e, :, cols(jj)], stage.at[s, 1], sems.at[s]))

    @pl.when((j == 0) & (r == 0))
    def _():
        for cp in copies(be_ref[0], 0, 0):
            cp.start()

    @pl.when(valid & (first_ref[r] == 1))
    def _():
        s = (j * nruns_ref[0] + run_ref[r]) % 2
        nxt = nxt_ref[r]

        @pl.when(nxt >= 0)
        def _():
            for cp in copies(nxt, j, 1 - s):
                cp.start()

        @pl.when((nxt < 0) & (j + 1 < nf))
        def _():
            for cp in copies(be_ref[0], j + 1, 1 - s):
                cp.start()

        for cp in copies(be_ref[r], j, s):
            cp.wait()
        w_scr[:, 0:tf] = stage[s, 0].astype(BF16)
        w_scr[:, tf:2 * tf] = stage[s, 1].astype(BF16)

    @pl.when(valid)
    def _():
        gu = jnp.dot(x_ref[...], w_scr[...], preferred_element_type=F32)
        o_ref[...] = (jax.nn.silu(gu[:, 0:tf]) * gu[:, tf:2 * tf]).astype(o_ref.dtype)

    @pl.when(jnp.logical_not(valid))
    def _():
        o_ref[...] = jnp.zeros_like(o_ref)


def _ffn_down_kernel(be_ref, nv_ref, first_ref, run_ref, nxt_ref, a_ref, wd_hbm, o_ref,
                     stage, w_scr, sems, *, li):
    r = pl.program_id(0)
    valid = r < nv_ref[0]
    copy = lambda e, s: pltpu.make_async_copy(wd_hbm.at[li, e], stage.at[s], sems.at[s])

    @pl.when(r == 0)
    def _():
        copy(be_ref[0], 0).start()

    @pl.when(valid & (first_ref[r] == 1))
    def _():
        s = run_ref[r] % 2
        nxt = nxt_ref[r]

        @pl.when(nxt >= 0)
        def _():
            copy(nxt, 1 - s).start()

        copy(be_ref[r], s).wait()
        w_scr[...] = stage[s].astype(BF16)

    @pl.when(valid)
    def _():
        o_ref[...] = jnp.dot(a_ref[...], w_scr[...], preferred_element_type=F32)

    @pl.when(jnp.logical_not(valid))
    def _():
        o_ref[...] = jnp.zeros_like(o_ref)


def _expert_ffn(xg, blk_e, nvalid, w_gate, w_up, w_down, li, tm, tf=384):
    p, d = xg.shape
    f = w_gate.shape[3]
    nb = p // tm
    nf = f // tf
    tiles = jnp.arange(nb, dtype=jnp.int32)
    valid = tiles < nvalid[0]
    first = valid & ((tiles == 0) | (blk_e != jnp.roll(blk_e, 1)))
    run = (jnp.cumsum(first.astype(jnp.int32)) - 1).astype(jnp.int32)
    nruns = jnp.sum(first.astype(jnp.int32)).reshape(1)
    run_e = jnp.sum(jnp.where(first[None, :] & (run[None, :] == tiles[:, None]), blk_e[None, :], 0),
                    axis=1)
    nxt = jnp.sum(jnp.where(tiles[None, :] == run[:, None] + 1, run_e[None, :], 0), axis=1)
    nxt = jnp.where(run + 1 < nruns[0], nxt, -1).astype(jnp.int32)
    first = first.astype(jnp.int32)
    last = lambda r, nv: jnp.minimum(r, nv[0] - 1)
    act = pl.pallas_call(
        functools.partial(_ffn_up_kernel, li=li, tf=tf, nf=nf),
        grid_spec=pltpu.PrefetchScalarGridSpec(
            num_scalar_prefetch=6,
            grid=(nf, nb),
            in_specs=[pl.BlockSpec((tm, d), lambda j, r, be, nv, *_: (last(r, nv), 0)),
                      pl.BlockSpec(memory_space=pl.ANY),
                      pl.BlockSpec(memory_space=pl.ANY)],
            out_specs=pl.BlockSpec((tm, tf), lambda j, r, *_: (r, j)),
            scratch_shapes=[pltpu.VMEM((2, 2, d, tf), F32), pltpu.VMEM((d, 2 * tf), BF16),
                            pltpu.SemaphoreType.DMA((2,))],
        ),
        out_shape=jax.ShapeDtypeStruct((p, f), BF16),
        compiler_params=_params("arbitrary", "arbitrary"),
        name="moe_ffn_up",
    )(blk_e, nvalid, first, run, nxt, nruns, xg, w_gate, w_up)
    return pl.pallas_call(
        functools.partial(_ffn_down_kernel, li=li),
        grid_spec=pltpu.PrefetchScalarGridSpec(
            num_scalar_prefetch=5,
            grid=(nb,),
            in_specs=[pl.BlockSpec((tm, f), lambda r, be, nv, *_: (last(r, nv), 0)),
                      pl.BlockSpec(memory_space=pl.ANY)],
            out_specs=pl.BlockSpec((tm, d), lambda r, *_: (r, 0)),
            scratch_shapes=[pltpu.VMEM((2, f, d), F32), pltpu.VMEM((f, d), BF16),
                            pltpu.SemaphoreType.DMA((2,))],
        ),
        out_shape=jax.ShapeDtypeStruct((p, d), F32),
        compiler_params=_params("arbitrary"),
        name="moe_ffn_down",
    )(blk_e, nvalid, first, run, nxt, act, w_down)


def _hier_moe(x, norm_g, w_group, b_group, w_router, b_router, w_gate, w_up, w_down, li,
              next_norm_g, tm=MOE_TM, tc=128):
    n_tok, d = x.shape
    sel, w_exp = _router(x, norm_g, w_group, b_group, w_router, b_router)

    n_as = n_tok * TOP_K
    nb = -(-n_as // tm) + N_EXPERTS
    p_rows = nb * tm
    n_pad = p_rows - n_as
    flat_e = sel[:, :TOP_K].reshape(n_as)
    experts = jnp.arange(N_EXPERTS, dtype=jnp.int32)
    counts = jnp.sum((flat_e[:, None] == experts[None, :]).astype(jnp.int32), axis=0)
    pcounts = (counts + tm - 1) // tm * tm
    nvalid = (jnp.sum(pcounts) // tm).astype(jnp.int32).reshape(1)
    pad_end = jnp.cumsum(pcounts - counts)
    pad_e = jnp.sum((jnp.arange(n_pad, dtype=jnp.int32)[:, None] >= pad_end[None, :]).astype(jnp.int32),
                    axis=1)
    keys = jnp.concatenate([2 * flat_e, 2 * pad_e + 1])
    vals = jnp.concatenate([jnp.arange(n_as, dtype=jnp.int32), jnp.full((n_pad,), -1, jnp.int32)])
    skeys, svals = lax.sort((keys, vals), num_keys=1)
    buf_tok = jnp.where(svals >= 0, svals // TOP_K, jnp.arange(p_rows, dtype=jnp.int32) % n_tok)
    _, pos = lax.sort((svals, jnp.arange(p_rows, dtype=jnp.int32)), num_keys=1)
    dest = pos[n_pad:].reshape(n_tok, TOP_K)
    blk_e = jnp.minimum(skeys.reshape(nb, tm)[:, 0] // 2, N_EXPERTS - 1).astype(jnp.int32)

    xg = _gather_norm(x, norm_g, buf_tok, nvalid, tm)
    yb = _expert_ffn(xg, blk_e, nvalid, w_gate, w_up, w_down, li, tm)
    dest_t = dest.reshape(n_tok // tc, tc, TOP_K).transpose(0, 2, 1).reshape(n_tok // tc, 1, TOP_K * tc)
    return _combine(x, w_exp, yb, dest_t, next_norm_g, tm=tc)


def _ab_layer(x, h, w_in, li, conv_w, conv_b, wa, ba, wi, bi, lam, bf, qn, kn, w_out, bsz):
    n_tok, d = x.shape
    t = n_tok // bsz
    d_rnn = conv_w.shape[1]
    nh = bf.shape[0]
    fw = nh * HEAD_DIM
    zga = _proj_f32(h, w_in, li, 0, 2 * d_rnn, wt=True).reshape(bsz, t, 2 * d_rnn)
    ya = _rglru(zga, bsz, conv_w, conv_b, wa, ba, wi, bi, lam)
    normw = jnp.concatenate([jnp.tile(qn, nh), jnp.tile(kn * FOX_QK_SCALE, nh),
                             jnp.ones((fw,), F32)]).reshape(1, 3 * fw)
    qkv = _proj_heads(h, w_in, li, 2 * d_rnn, 3 * fw, normw, 2 * fw, bsz, 1, wt=True)
    b_f = jnp.pad(bf.astype(F32), (0, LANES - nh)).reshape(1, LANES)
    cum = _forget_cum(h, w_in, li, 2 * d_rnn + 3 * fw, nh, b_f, bsz, wt=True)
    fox_tq = 512
    cum_row = cum[:, :, :nh].transpose(0, 2, 1).reshape(bsz, nh, t // fox_tq, fox_tq)
    yb = _fox_attention(qkv, cum, cum_row, tq=fox_tq)
    return _proj_res([ya.reshape(n_tok, d_rnn), yb.reshape(n_tok, fw)], w_out, li, x)


def _dilated_layer(x, h, w_in, li, qn, kn, table, w_out, bsz):
    n_tok, d = x.shape
    t = n_tok // bsz
    nh = table.shape[1] // len(DSW_PATTERNS)
    fw = nh * HEAD_DIM
    os_, lses = [], []
    for g, (window, dil) in enumerate(DSW_PATTERNS):
        normw = jnp.concatenate([jnp.tile(qn[g], nh), jnp.tile(kn[g], nh),
                                 jnp.ones((fw,), F32)]).reshape(1, 3 * fw)
        qkv = _proj_heads(h, w_in, li, g * 3 * fw, 3 * fw, normw, 2 * fw, bsz, dil)
        bias = _dsw_bias(table[:, g * nh:(g + 1) * nh], window, dil)
        o, lse = _dsw_group(qkv, bias, min(t // dil, 1024))
        os_.append(o)
        lses.append(lse)
    y = _dsw_merge(os_, lses)
    return _proj_res([y], w_out, li, x)


def kernel(x, mix_norm, ffn_norm, w_in_ab, conv_w, conv_b, lru_wa, lru_ba, lru_wi, lru_bi,
           lru_lambda, fox_bf, fox_qnorm, fox_knorm, w_out_ab, w_in_c, dsw_qnorm, dsw_knorm,
           w_out_c, rel_bias, moe_w_group, moe_b_group, moe_w_router, moe_b_router,
           moe_w_gate, moe_w_up, moe_w_down):
    bsz, t, d = x.shape
    depth = mix_norm.shape[0]
    xf = x.reshape(bsz * t, d)
    h = _rmsnorm(xf, mix_norm[0])
    for layer in range(depth):
        i = layer // 2
        if layer % 2 == 0:
            xf = _ab_layer(xf, h, jnp.swapaxes(w_in_ab, 1, 2), i, conv_w[i], conv_b[i], lru_wa[i],
                           lru_ba[i], lru_wi[i], lru_bi[i], lru_lambda[i], fox_bf[i],
                           fox_qnorm[i], fox_knorm[i], w_out_ab, bsz)
        else:
            xf = _dilated_layer(xf, h, w_in_c, i, dsw_qnorm[i], dsw_knorm[i],
                                rel_bias, w_out_c, bsz)
        next_g = mix_norm[layer + 1] if layer + 1 < depth else None
        xf, h = _hier_moe(xf, ffn_norm[layer], moe_w_group[layer], moe_b_group[layer],
                          moe_w_router[layer], moe_b_router[layer], moe_w_gate, moe_w_up,
                          moe_w_down, layer, next_g)
    return xf.reshape(bsz, t, d)
```

```python
import functools
import math

import jax
import jax.numpy as jnp
from jax import lax
from jax.experimental import pallas as pl
from jax.experimental.pallas import tpu as pltpu

F32 = jnp.float32
BF16 = jnp.bfloat16

LANES = 128
SUBLANES = 8
VMEM_LIMIT = 56 * 1024 * 1024

HEAD_DIM = 128
RMS_EPS = 1e-6
RNN_BLOCKS = 16
CONV_WIDTH = 4
LRU_C = 8.0
DSW_PATTERNS = ((128, 1), (512, 4), (2048, 16))
DSW_BLOCK = 128
DSW_CHAINS = 8
DSW_SKEW = 2
N_BUCKETS = 32
MAX_DISTANCE = 2048
N_GROUPS = 4
EXPERTS_PER_GROUP = 8
N_EXPERTS = N_GROUPS * EXPERTS_PER_GROUP
TOP_K = 2
LOG2E = math.log2(math.e)

MOE_TM = 256
PROJ_SUB = 256
MAX_ROW_STRIDE = 4


def _params(*sem):
    return pltpu.CompilerParams(dimension_semantics=sem, vmem_limit_bytes=VMEM_LIMIT)


def _rmsnorm_kernel(x_ref, g_ref, o_ref):
    x = x_ref[...]
    ms = jnp.mean(x * x, axis=-1, keepdims=True)
    o_ref[...] = (x * lax.rsqrt(ms + RMS_EPS) * g_ref[...]).astype(o_ref.dtype)


def _rmsnorm(x, g, tm=256):
    m, d = x.shape
    return pl.pallas_call(
        _rmsnorm_kernel,
        grid=(m // tm,),
        in_specs=[pl.BlockSpec((tm, d), lambda i: (i, 0)),
                  pl.BlockSpec((1, d), lambda i: (0, 0))],
        out_specs=pl.BlockSpec((tm, d), lambda i: (i, 0)),
        out_shape=jax.ShapeDtypeStruct((m, d), BF16),
        compiler_params=_params("parallel"),
        name="rmsnorm",
    )(x, g.reshape(1, d))


def _mm(a, w_blk, wt):
    w = w_blk.astype(BF16)
    if wt:
        return lax.dot_general(a, w, (((1,), (1,)), ((), ())), preferred_element_type=F32)
    return jnp.dot(a, w, preferred_element_type=F32)


def _w_spec(k, tn, li, jb, wt, grid_rank):
    if wt:
        return pl.BlockSpec((None, tn, k), lambda *g: (li, jb + g[grid_rank - 1], 0))
    return pl.BlockSpec((None, k, tn), lambda *g: (li, 0, jb + g[grid_rank - 1]))


def _proj_f32_kernel(a_ref, w_ref, o_ref, *, wt):
    for sb in range(o_ref.shape[1] // PROJ_SUB):
        cs = slice(sb * PROJ_SUB, (sb + 1) * PROJ_SUB)
        o_ref[:, cs] = _mm(a_ref[...], w_ref[cs, :] if wt else w_ref[:, cs], wt)


def _proj_f32(a, w, li, col0, ncols, wt=False, tm=1024, tn=512):
    m, k = a.shape
    jb = col0 // tn
    return pl.pallas_call(
        functools.partial(_proj_f32_kernel, wt=wt),
        grid=(m // tm, ncols // tn),
        in_specs=[pl.BlockSpec((tm, k), lambda i, j: (i, 0)),
                  _w_spec(k, tn, li, jb, wt, 2)],
        out_specs=pl.BlockSpec((tm, tn), lambda i, j: (i, j)),
        out_shape=jax.ShapeDtypeStruct((m, ncols), F32),
        compiler_params=_params("parallel", "arbitrary"),
        name="proj_f32",
    )(a, w)


def _proj_res_kernel(*refs, n_in):
    a_refs, w_refs = refs[:n_in], refs[n_in:2 * n_in]
    r_ref, o_ref = refs[2 * n_in], refs[2 * n_in + 1]
    for sb in range(o_ref.shape[1] // PROJ_SUB):
        cs = slice(sb * PROJ_SUB, (sb + 1) * PROJ_SUB)
        acc = r_ref[:, cs]
        for a_ref, w_ref in zip(a_refs, w_refs):
            acc = acc + jnp.dot(a_ref[...], w_ref[:, cs].astype(BF16), preferred_element_type=F32)
        o_ref[:, cs] = acc


def _proj_res(a_list, w, li, res, tm=1024, tn=512):
    m, k = a_list[0].shape
    n_in = len(a_list)
    n = w.shape[2]
    a_specs = [pl.BlockSpec((tm, k), lambda i, j: (i, 0)) for _ in a_list]
    w_specs = [pl.BlockSpec((None, k, tn), lambda i, j, p=p: (li, p, j)) for p in range(n_in)]
    return pl.pallas_call(
        functools.partial(_proj_res_kernel, n_in=n_in),
        grid=(m // tm, n // tn),
        in_specs=a_specs + w_specs + [pl.BlockSpec((tm, tn), lambda i, j: (i, j))],
        out_specs=pl.BlockSpec((tm, tn), lambda i, j: (i, j)),
        out_shape=jax.ShapeDtypeStruct((m, n), F32),
        compiler_params=_params("parallel", "arbitrary"),
        name="proj_res",
    )(*a_list, *([w] * n_in), res)


def _proj_heads_kernel(a_ref, w_ref, nw_ref, o_ref, acc_scr, y_scr, t_scr, *,
                       dil, n_norm_steps, nj, nsteps, tm, wt):
    s = pl.program_id(0)
    hps = acc_scr.shape[2] // HEAD_DIM
    rows = tm // dil

    @pl.when(s == 0)
    def _():
        acc_scr[1] = jnp.zeros(acc_scr.shape[1:], F32)

    prev = jnp.maximum(s - 1, 0)
    is_norm = (prev % nj) < n_norm_steps

    def step(slot):
        acc_scr[slot] = _mm(a_ref[...], w_ref[...], wt)
        acc = acc_scr[1 - slot]
        for hh in range(hps):
            y = acc[:, hh * HEAD_DIM:(hh + 1) * HEAD_DIM]
            ms = jnp.mean(y * y, axis=-1, keepdims=True)
            yn = y * lax.rsqrt(ms + RMS_EPS) * nw_ref[:, hh * HEAD_DIM:(hh + 1) * HEAD_DIM]
            y = jnp.where(is_norm, yn, y)
            if dil == 1:
                o_ref[0, 0, hh] = y.astype(o_ref.dtype)
            else:
                y_scr[hh] = y
        if dil > 1:
            s1 = min(dil, MAX_ROW_STRIDE)
            s2 = dil // s1
            for hh in range(hps):
                for z1 in range(s1):
                    if s2 == 1:
                        o_ref[0, z1, hh] = (
                            y_scr[hh, pl.ds(z1, rows, stride=s1), :].astype(o_ref.dtype))
                        continue
                    t_scr[z1] = y_scr[hh, pl.ds(z1, tm // s1, stride=s1), :]
                    for z2 in range(s2):
                        o_ref[0, z1 + s1 * z2, hh] = (
                            t_scr[z1, pl.ds(z2, rows, stride=s2), :].astype(o_ref.dtype))

    @pl.when(s % 2 == 0)
    def _():
        step(0)

    @pl.when(s % 2 == 1)
    def _():
        step(1)


def _proj_heads(a, w, li, col0, ncols, normw, n_norm_cols, bsz, dil, wt=False, tm=1024, tn=512):
    m, k = a.shape
    t = m // bsz
    nt = t // tm
    hps = tn // HEAD_DIM
    nj = ncols // tn
    jb = col0 // tn
    nsteps = bsz * nt * nj

    def mm_step(s):
        s = jnp.minimum(s, nsteps - 1)
        return s // nj, s % nj

    def out_step(s):
        s = jnp.maximum(s - 1, 0)
        return s // (nj * nt), (s // nj) % nt, s % nj

    if wt:
        w_spec = pl.BlockSpec((None, tn, k), lambda s: (li, jb + mm_step(s)[1], 0))
    else:
        w_spec = pl.BlockSpec((None, k, tn), lambda s: (li, 0, jb + mm_step(s)[1]))
    kern = functools.partial(_proj_heads_kernel, dil=dil, n_norm_steps=n_norm_cols // tn, nj=nj,
                             nsteps=nsteps, tm=tm, wt=wt)
    return pl.pallas_call(
        kern,
        grid=(nsteps + 1,),
        in_specs=[pl.BlockSpec((tm, k), lambda s: (mm_step(s)[0], 0)),
                  w_spec,
                  pl.BlockSpec((1, tn), lambda s: (0, out_step(s)[2]))],
        out_specs=pl.BlockSpec((1, dil, hps, tm // dil, HEAD_DIM),
                               lambda s: (out_step(s)[0], 0, out_step(s)[2], out_step(s)[1], 0)),
        out_shape=jax.ShapeDtypeStruct((bsz, dil, ncols // HEAD_DIM, t // dil, HEAD_DIM), BF16),
        scratch_shapes=[pltpu.VMEM((2, tm, tn), F32),
                        pltpu.VMEM((hps, tm, HEAD_DIM), F32),
                        pltpu.VMEM((MAX_ROW_STRIDE, tm // MAX_ROW_STRIDE, HEAD_DIM), F32)],
        compiler_params=_params("arbitrary"),
        name=f"proj_heads_d{dil}",
    )(a, w, normw)


def _forget_cum_kernel(a_ref, w_ref, b_ref, o_ref, carry_ref, *, tm, nh, wt):
    i = pl.program_id(1)

    @pl.when(i == 0)
    def _():
        carry_ref[...] = jnp.zeros_like(carry_ref)

    lane = lax.broadcasted_iota(jnp.int32, (tm, LANES), 1)
    fz = _mm(a_ref[...], w_ref[...], wt)
    c = jax.nn.log_sigmoid(jnp.where(lane < nh, fz + b_ref[...], 0.0))
    row = lax.broadcasted_iota(jnp.int32, c.shape, 0)
    s = 1
    while s < tm:
        c = c + jnp.where(row >= s, pltpu.roll(c, s, 0), 0.0)
        s *= 2
    c = c + carry_ref[0:1, :]
    o_ref[0] = c
    carry_ref[...] = jnp.broadcast_to(c[tm - 1:tm, :], carry_ref.shape)


def _forget_cum(a, w, li, col0, nh, b_f, bsz, wt=False, tm=512):
    m, k = a.shape
    t = m // bsz
    nt = t // tm
    jb = col0 // LANES
    return pl.pallas_call(
        functools.partial(_forget_cum_kernel, tm=tm, nh=nh, wt=wt),
        grid=(bsz, nt),
        in_specs=[pl.BlockSpec((tm, k), lambda b, i: (b * nt + i, 0)),
                  (pl.BlockSpec((None, LANES, k), lambda b, i: (li, jb, 0)) if wt
                   else pl.BlockSpec((None, k, LANES), lambda b, i: (li, 0, jb))),
                  pl.BlockSpec((1, LANES), lambda b, i: (0, 0))],
        out_specs=pl.BlockSpec((1, tm, LANES), lambda b, i: (b, i, 0)),
        out_shape=jax.ShapeDtypeStruct((bsz, t, LANES), F32),
        scratch_shapes=[pltpu.VMEM((SUBLANES, LANES), F32)],
        compiler_params=_params("parallel", "arbitrary"),
        name="forget_cum",
    )(a, w, b_f)


def _gelu_tanh(x):
    return 0.5 * x * (1.0 + jnp.tanh(math.sqrt(2.0 / math.pi) * (x + 0.044715 * (x * x * x))))


def _rglru_kernel(g_ref, x_ref, cw_ref, cb_ref, wa_ref, ba_ref, wi_ref, bi_ref, lam_ref,
                  o_ref, xs_scr, a_scr, u_scr, h_scr, carry_scr, *, tc, nblk):
    ti = pl.program_id(2)
    halo = SUBLANES

    @pl.when(ti == 0)
    def _():
        xs_scr[0:halo, :] = jnp.zeros((halo, xs_scr.shape[1]), F32)
        carry_scr[...] = jnp.zeros_like(carry_scr)

    xs_scr[halo:halo + tc, :] = x_ref[0]
    xc = cb_ref[...] + cw_ref[CONV_WIDTH - 1:CONV_WIDTH, :] * xs_scr[halo:halo + tc, :]
    for j in range(CONV_WIDTH - 1):
        sh = CONV_WIDTH - 1 - j
        xc = xc + cw_ref[j:j + 1, :] * xs_scr[halo - sh:halo - sh + tc, :]
    xs_scr[0:halo, :] = xs_scr[tc:tc + halo, :]

    neg_sp = -LRU_C * jax.nn.softplus(-lam_ref[...])
    for n in range(nblk):
        cs = slice(n * HEAD_DIM, (n + 1) * HEAD_DIM)
        xb = xc[:, cs]
        xbb = xb.astype(BF16)
        r = jax.nn.sigmoid(jnp.dot(xbb, wa_ref[n].astype(BF16), preferred_element_type=F32)
                           + ba_ref[:, cs])
        ig = jax.nn.sigmoid(jnp.dot(xbb, wi_ref[n].astype(BF16), preferred_element_type=F32)
                            + bi_ref[:, cs])
        log_a = neg_sp[:, cs] * r
        a = jnp.exp(log_a)
        a_scr[:, cs] = a
        u_scr[:, cs] = jnp.sqrt(-jnp.tanh(log_a) * (a * a + 1.0)) * (ig * xb)

    row = lax.broadcasted_iota(jnp.int32, (SUBLANES, a_scr.shape[1]), 0)

    def body(g, carry):
        r0 = pl.multiple_of(g * SUBLANES, SUBLANES)
        a = a_scr[pl.ds(r0, SUBLANES), :]
        u = u_scr[pl.ds(r0, SUBLANES), :]
        s = 1
        while s < SUBLANES:
            keep = row >= s
            u = jnp.where(keep, a * pltpu.roll(u, s, 0) + u, u)
            a = jnp.where(keep, a * pltpu.roll(a, s, 0), a)
            s *= 2
        h = a * carry + u
        h_scr[pl.ds(r0, SUBLANES), :] = h
        return jnp.broadcast_to(h[SUBLANES - 1:SUBLANES, :], h.shape)

    carry = lax.fori_loop(0, tc // SUBLANES, body, carry_scr[...], unroll=4)
    carry_scr[...] = carry
    o_ref[0] = (h_scr[...] * _gelu_tanh(g_ref[0])).astype(o_ref.dtype)


def _rglru(zga, bsz, conv_w, conv_b, wa, ba, wi, bi, lam, tc=256, cw=512):
    _, t, c2 = zga.shape
    c = c2 // 2
    nblk = cw // HEAD_DIM
    ncb = c // cw
    vec = lambda v: v.reshape(1, c)
    vspec = pl.BlockSpec((1, cw), lambda b, j, i: (0, j))
    kern = functools.partial(_rglru_kernel, tc=tc, nblk=nblk)
    return pl.pallas_call(
        kern,
        grid=(bsz, ncb, t // tc),
        in_specs=[pl.BlockSpec((1, tc, cw), lambda b, j, i: (b, i, j)),
                  pl.BlockSpec((1, tc, cw), lambda b, j, i: (b, i, ncb + j)),
                  pl.BlockSpec((CONV_WIDTH, cw), lambda b, j, i: (0, j)),
                  vspec,
                  pl.BlockSpec((nblk, HEAD_DIM, HEAD_DIM), lambda b, j, i: (j, 0, 0)),
                  vspec,
                  pl.BlockSpec((nblk, HEAD_DIM, HEAD_DIM), lambda b, j, i: (j, 0, 0)),
                  vspec, vspec],
        out_specs=pl.BlockSpec((1, tc, cw), lambda b, j, i: (b, i, j)),
        out_shape=jax.ShapeDtypeStruct((bsz, t, c), BF16),
        scratch_shapes=[pltpu.VMEM((tc + SUBLANES, cw), F32),
                        pltpu.VMEM((tc, cw), F32),
                        pltpu.VMEM((tc, cw), F32),
                        pltpu.VMEM((tc, cw), F32),
                        pltpu.VMEM((SUBLANES, cw), F32)],
        compiler_params=_params("parallel", "parallel", "arbitrary"),
        name="rglru",
    )(zga, zga, conv_w, vec(conv_b), wa, vec(ba), wi, vec(bi), vec(lam))


FOX_STRIP = 128
FOX_QK_SCALE = LOG2E / math.sqrt(HEAD_DIM)
FOX_SKEW = 3


def _fox_kernel(q_ref, k_ref, v_ref, cq_ref, ck_ref, o_ref, m_scr, l_scr, acc_scr, cq_scr, *, tq, hp):
    g = pl.program_id(1)
    qi = pl.program_id(2)
    tk = tq
    sr = FOX_STRIP
    lane = lax.broadcasted_iota(jnp.int32, (tq, LANES), 1)
    for hh in range(hp):
        cq = jnp.sum(jnp.where(lane == g * hp + hh, cq_ref[0], 0.0), axis=-1, keepdims=True)
        cq_scr[hh] = jnp.broadcast_to(cq * LOG2E, (tq, LANES))
        m_scr[hh] = jnp.full(m_scr.shape[1:], -jnp.inf, F32)
        l_scr[hh] = jnp.zeros(l_scr.shape[1:], F32)
        acc_scr[hh] = jnp.zeros(acc_scr.shape[1:], F32)
    tri = (lax.broadcasted_iota(jnp.int32, (sr, LANES), 1)
           <= lax.broadcasted_iota(jnp.int32, (sr, LANES), 0))

    def scores(hh, r, j, diagonal):
        rs = slice(r * sr, (r + 1) * sr)
        k0 = pl.multiple_of(j * tk, tk)
        nk = (r + 1) * sr if diagonal else tk
        return lax.dot_general(q_ref[0, 0, hh, rs, :], k_ref[0, 0, hh, pl.ds(k0, nk), :],
                               (((1,), (1,)), ((), ())), preferred_element_type=F32)

    def softmax_pv(s, hh, r, j, ck, diagonal):
        rs = slice(r * sr, (r + 1) * sr)
        k0 = pl.multiple_of(j * tk, tk)
        nk = s.shape[1]
        cq = cq_scr[hh, rs, :]
        parts = []
        for c in range(nk // LANES):
            cs = slice(c * LANES, (c + 1) * LANES)
            part = s[:, cs] + (cq - ck[:, cs])
            if diagonal and c == r:
                part = jnp.where(tri, part, -jnp.inf)
            parts.append(part)
        m_prev = m_scr[hh, rs, :]
        m_new = jnp.maximum(m_prev, jnp.max(functools.reduce(jnp.maximum, parts),
                                            axis=-1, keepdims=True))
        alpha = jnp.exp2(m_prev - m_new)
        ps = [jnp.exp2(part - m_new) for part in parts]
        row_sum = jnp.sum(functools.reduce(lambda a, b: a + b, ps), axis=-1, keepdims=True)
        l_scr[hh, rs, :] = alpha * l_scr[hh, rs, :] + row_sum
        p = jnp.concatenate(ps, axis=1).astype(BF16)
        acc_scr[hh, rs, :] = alpha * acc_scr[hh, rs, :] + jnp.dot(
            p, v_ref[0, 0, hh, pl.ds(k0, nk), :], preferred_element_type=F32)
        m_scr[hh, rs, :] = m_new

    def kv_step(j, diagonal):
        cks = [ck_ref[0, hh, pl.ds(j, 1), :] * LOG2E for hh in range(hp)]
        strips = [(hh, r) for hh in range(hp) for r in range(tq // sr)]
        pending = [scores(*strips[n], j, diagonal) for n in range(FOX_SKEW)]
        for n, (hh, r) in enumerate(strips):
            if n + FOX_SKEW < len(strips):
                pending.append(scores(*strips[n + FOX_SKEW], j, diagonal))
            softmax_pv(pending.pop(0), hh, r, j, cks[hh], diagonal)

    def below_diagonal(j, carry):
        kv_step(j, False)
        return carry

    lax.fori_loop(0, qi, below_diagonal, 0)
    kv_step(qi, True)
    for hh in range(hp):
        o_ref[0, :, hh * HEAD_DIM:(hh + 1) * HEAD_DIM] = (acc_scr[hh] / l_scr[hh]).astype(o_ref.dtype)


def _fox_attention(qkv, cum_col, cum_row, tq=512, hp=2):
    bsz, _, h3, t, dh = qkv.shape
    nh = h3 // 3
    nq = t // tq
    ng = nh // hp
    kern = functools.partial(_fox_kernel, tq=tq, hp=hp)
    return pl.pallas_call(
        kern,
        grid=(bsz, ng, nq),
        in_specs=[pl.BlockSpec((1, 1, hp, tq, dh), lambda b, g, i: (b, 0, g, i, 0)),
                  pl.BlockSpec((1, 1, hp, t, dh), lambda b, g, i: (b, 0, ng + g, 0, 0)),
                  pl.BlockSpec((1, 1, hp, t, dh), lambda b, g, i: (b, 0, 2 * ng + g, 0, 0)),
                  pl.BlockSpec((1, tq, LANES), lambda b, g, i: (b, i, 0)),
                  pl.BlockSpec((1, hp, nq, tq), lambda b, g, i: (b, g, 0, 0))],
        out_specs=pl.BlockSpec((1, tq, hp * dh), lambda b, g, i: (b, i, g)),
        out_shape=jax.ShapeDtypeStruct((bsz, t, nh * dh), BF16),
        scratch_shapes=[pltpu.VMEM((hp, tq, LANES), F32), pltpu.VMEM((hp, tq, LANES), F32),
                        pltpu.VMEM((hp, tq, dh), F32), pltpu.VMEM((hp, tq, LANES), F32)],
        compiler_params=_params("parallel", "parallel", "arbitrary"),
        name="fox_attention",
    )(qkv, qkv, qkv, cum_col, cum_row)


def _dsw_kernel(q_ref, kc_ref, kp_ref, vc_ref, vp_ref, bias_ref, o_ref, lse_ref, *, lc, nh):
    c = pl.program_id(2)
    blk = DSW_BLOCK
    nb = lc // blk
    lane = lax.broadcasted_iota(jnp.int32, (blk, LANES), 1)
    col = lax.broadcasted_iota(jnp.int32, (blk, 2 * blk), 1)
    scale = 1.0 / math.sqrt(HEAD_DIM)
    lse_ref[...] = jnp.zeros_like(lse_ref)

    def scores(h, n):
        rs = slice(n * blk, (n + 1) * blk)
        if n == 0:
            kprev = kp_ref[0, 0, h]
        else:
            kprev = kc_ref[0, 0, h, (n - 1) * blk:n * blk, :]
        kk = jnp.concatenate([kprev, kc_ref[0, 0, h, rs, :]], axis=0)
        return lax.dot_general(q_ref[0, 0, h, rs, :], kk, (((1,), (1,)), ((), ())),
                               preferred_element_type=F32)

    def softmax_pv(s, h, n):
        rs = slice(n * blk, (n + 1) * blk)
        if n == 0:
            vprev = vp_ref[0, 0, h]
        else:
            vprev = vc_ref[0, 0, h, (n - 1) * blk:n * blk, :]
        vv = jnp.concatenate([vprev, vc_ref[0, 0, h, rs, :]], axis=0)
        s = s * scale + bias_ref[h]
        if n == 0:
            s = jnp.where((c > 0) | (col >= blk), s, -jnp.inf)
        m = jnp.max(s, axis=-1, keepdims=True)
        e = jnp.exp(s - m)
        den = jnp.sum(e, axis=-1, keepdims=True)
        o = jnp.dot(e.astype(BF16), vv, preferred_element_type=F32) / den
        o_ref[0, 0, rs, pl.ds(pl.multiple_of(h * HEAD_DIM, HEAD_DIM), HEAD_DIM)] = (
            o.astype(o_ref.dtype))
        lse = m + jnp.log(den)
        lse_ref[0, 0, rs, :] = jnp.where(lane == h, lse, lse_ref[0, 0, rs, :])

    hu = math.gcd(nh, max(1, DSW_CHAINS // nb))

    def heads(i, carry):
        blocks = [(i * hu + hh, n) for hh in range(hu) for n in range(nb)]
        pending = [scores(*blocks[k]) for k in range(DSW_SKEW)]
        for k, (h, n) in enumerate(blocks):
            if k + DSW_SKEW < len(blocks):
                pending.append(scores(*blocks[k + DSW_SKEW]))
            softmax_pv(pending.pop(0), h, n)
        return carry

    lax.fori_loop(0, nh // hu, heads, 0)


def _dsw_group(qkv, bias, lc):
    bsz, dil, h3, l, dh = qkv.shape
    nh = h3 // 3
    nc = l // lc
    bpc = lc // DSW_BLOCK
    kern = functools.partial(_dsw_kernel, lc=lc, nh=nh)
    cur = lambda off: pl.BlockSpec((1, 1, nh, lc, dh), lambda b, z, c: (b, z, off, c, 0))
    prev = lambda off: pl.BlockSpec((1, 1, nh, DSW_BLOCK, dh),
                                    lambda b, z, c: (b, z, off, jnp.maximum(c * bpc - 1, 0), 0))
    return pl.pallas_call(
        kern,
        grid=(bsz, dil, nc),
        in_specs=[cur(0), cur(1), prev(1), cur(2), prev(2),
                  pl.BlockSpec((nh, DSW_BLOCK, 2 * DSW_BLOCK), lambda b, z, c: (0, 0, 0))],
        out_specs=[pl.BlockSpec((1, 1, lc, nh * dh), lambda b, z, c: (b, z, c, 0)),
                   pl.BlockSpec((1, 1, lc, LANES), lambda b, z, c: (b, z, c, 0))],
        out_shape=[jax.ShapeDtypeStruct((bsz, dil, l, nh * dh), BF16),
                   jax.ShapeDtypeStruct((bsz, dil, l, LANES), F32)],
        compiler_params=_params("parallel", "parallel", "arbitrary"),
        name=f"dsw_attention_d{dil}",
    )(qkv, qkv, qkv, qkv, qkv, bias)


def _dsw_merge_kernel(*refs, dils, tm, nh):
    ng = len(dils)
    o_refs = refs[:ng]
    l_refs = refs[ng:2 * ng]
    out_ref = refs[2 * ng]
    o_scr = refs[2 * ng + 1:2 * ng + 1 + ng]
    l_scr = refs[2 * ng + 1 + ng:]
    for g, dil in enumerate(dils):
        rows = tm // dil
        for z in range(dil):
            rsel = slice(None) if dil == 1 else pl.ds(z, rows, stride=dil)
            l_scr[g][rsel, :] = l_refs[g][0, z]
            for h in range(nh):
                o_scr[g][h, rsel, :] = o_refs[g][0, z, :, h * HEAD_DIM:(h + 1) * HEAD_DIM].astype(F32)
    lses = [l_scr[g][...] for g in range(ng)]
    m = functools.reduce(jnp.maximum, lses)
    ws = [jnp.exp(l - m) for l in lses]
    inv = 1.0 / functools.reduce(lambda a, b: a + b, ws)
    for h in range(nh):
        acc = None
        for g in range(ng):
            alpha = (ws[g] * inv)[:, h:h + 1]
            term = alpha * o_scr[g][h]
            acc = term if acc is None else acc + term
        out_ref[:, h * HEAD_DIM:(h + 1) * HEAD_DIM] = acc.astype(out_ref.dtype)


def _dsw_merge(os_, lses, tm=512):
    bsz, _, t, w = os_[0].shape
    nh = w // HEAD_DIM
    dils = tuple(o.shape[1] for o in os_)
    t = t * dils[0]
    nt = t // tm
    kern = functools.partial(_dsw_merge_kernel, dils=dils, tm=tm, nh=nh)
    o_specs = [pl.BlockSpec((1, d, tm // d, w), lambda b, i: (b, 0, i, 0)) for d in dils]
    l_specs = [pl.BlockSpec((1, d, tm // d, LANES), lambda b, i: (b, 0, i, 0)) for d in dils]
    return pl.pallas_call(
        kern,
        grid=(bsz, nt),
        in_specs=o_specs + l_specs,
        out_specs=pl.BlockSpec((tm, w), lambda b, i: (b * nt + i, 0)),
        out_shape=jax.ShapeDtypeStruct((bsz * t, w), BF16),
        scratch_shapes=([pltpu.VMEM((nh, tm, HEAD_DIM), F32) for _ in dils]
                        + [pltpu.VMEM((tm, LANES), F32) for _ in dils]),
        compiler_params=_params("parallel", "parallel"),
        name="dsw_merge",
    )(*os_, *lses)


def _t5_bucket(dist):
    max_exact = N_BUCKETS // 2
    large = max_exact + (jnp.log(jnp.maximum(dist, 1).astype(F32) / max_exact)
                         / math.log(MAX_DISTANCE / max_exact) * (N_BUCKETS - max_exact)).astype(jnp.int32)
    return jnp.where(dist < max_exact, dist, jnp.minimum(large, N_BUCKETS - 1))


def _dsw_bias(table, window, dil):
    span = window // dil
    qi = jnp.arange(DSW_BLOCK)[:, None]
    kj = jnp.arange(2 * DSW_BLOCK)[None, :]
    delta = qi + DSW_BLOCK - kj
    band = (delta >= 0) & (delta <= span)
    bucket = _t5_bucket(jnp.maximum(delta, 0) * dil)
    onehot = (bucket[None, :, :] == jnp.arange(N_BUCKETS)[:, None, None])
    table_t = table.astype(F32).T
    bias = jnp.sum(jnp.where(onehot[None], table_t[:, :, None, None], 0.0), axis=1)
    return jnp.where(band[None], bias, -jnp.inf)


def _router_kernel(x_ref, g_ref, whi_ref, wlo_ref, b_ref, sel_ref, wts_ref):
    x = x_ref[...]
    tm = x.shape[0]
    ms = jnp.mean(x * x, axis=-1, keepdims=True)
    h = x * lax.rsqrt(ms + RMS_EPS) * g_ref[...]
    hi = h.astype(BF16)
    lo = (h - hi.astype(F32)).astype(BF16)
    logits = (jnp.dot(hi, whi_ref[...], preferred_element_type=F32)
              + (jnp.dot(hi, wlo_ref[...], preferred_element_type=F32)
                 + jnp.dot(lo, whi_ref[...], preferred_element_type=F32))) + b_ref[...]
    lane = lax.broadcasted_iota(jnp.int32, (tm, LANES), 1)
    first_max = lambda v, vmax: jnp.min(jnp.where(v == vmax, lane, LANES), axis=-1, keepdims=True)

    glog = jnp.where(lane < N_GROUPS, logits, -jnp.inf)
    gmax = jnp.max(glog, axis=-1, keepdims=True)
    gsel = first_max(glog, gmax)
    pg = 1.0 / jnp.sum(jnp.exp(glog - gmax), axis=-1, keepdims=True)

    ex = lane - N_GROUPS
    in_group = (ex >= 0) & (ex < N_EXPERTS) & ((ex // EXPERTS_PER_GROUP) == gsel)
    elog = jnp.where(in_group, logits, -jnp.inf)
    v1 = jnp.max(elog, axis=-1, keepdims=True)
    j1 = first_max(elog, v1)
    elog2 = jnp.where(lane == j1, -jnp.inf, elog)
    v2 = jnp.max(elog2, axis=-1, keepdims=True)
    j2 = first_max(elog2, v2)
    e2 = jnp.exp(v2 - v1)
    w1 = pg / (1.0 + e2)
    w2 = pg * e2 / (1.0 + e2)
    sel_ref[...] = jnp.where(lane == 0, j1 - N_GROUPS, jnp.where(lane == 1, j2 - N_GROUPS, 0))
    wts_ref[:, 0:LANES] = jnp.broadcast_to(w1, (tm, LANES))
    wts_ref[:, LANES:2 * LANES] = jnp.broadcast_to(w2, (tm, LANES))


def _router(x, g, w_group, b_group, w_router, b_router, tm=256):
    m, d = x.shape
    pad = LANES - N_GROUPS - N_EXPERTS
    w_rt = jnp.concatenate([w_group, w_router, jnp.zeros((d, pad), F32)], axis=1)
    w_hi = w_rt.astype(BF16)
    w_lo = (w_rt - w_hi.astype(F32)).astype(BF16)
    b_rt = jnp.concatenate([b_group.astype(F32), b_router.astype(F32), jnp.zeros((pad,), F32)])
    return pl.pallas_call(
        _router_kernel,
        grid=(m // tm,),
        in_specs=[pl.BlockSpec((tm, d), lambda i: (i, 0)),
                  pl.BlockSpec((1, d), lambda i: (0, 0)),
                  pl.BlockSpec((d, LANES), lambda i: (0, 0)),
                  pl.BlockSpec((d, LANES), lambda i: (0, 0)),
                  pl.BlockSpec((1, LANES), lambda i: (0, 0))],
        out_specs=[pl.BlockSpec((tm, LANES), lambda i: (i, 0)),
                   pl.BlockSpec((tm, 2 * LANES), lambda i: (i, 0))],
        out_shape=[jax.ShapeDtypeStruct((m, LANES), jnp.int32),
                   jax.ShapeDtypeStruct((m, 2 * LANES), F32)],
        compiler_params=_params("parallel"),
        name="moe_router",
    )(x, g.reshape(1, d), w_hi, w_lo, b_rt.reshape(1, LANES))


GATHER_UNROLL = 8


def _issue_row_gather(idx_ref, src_hbm, dst_ref, sem, n_rows):
    def issue(i, carry):
        row = idx_ref[0, 0, i]
        pltpu.make_async_copy(src_hbm.at[pl.ds(row, 1)], dst_ref.at[pl.ds(i, 1)], sem).start()
        return carry
    lax.fori_loop(0, n_rows, issue, 0, unroll=8)


def _wait_row_gather(src_hbm, dst_ref, sem, n_rows):
    pltpu.make_async_copy(src_hbm.at[pl.ds(0, n_rows)], dst_ref, sem).wait()


def _gather_norm_kernel(nv_ref, idx_ref, idx_next_ref, x_hbm, g_ref, o_ref, xs_scr, sems, *, tm, d):
    r = pl.program_id(0)
    nv = nv_ref[0]
    slot = r % 2
    nc = d // LANES

    def issue(idx, s):
        def group(i8, carry):
            for u in range(GATHER_UNROLL):
                i = i8 * GATHER_UNROLL + u
                pltpu.make_async_copy(x_hbm.at[idx[0, 0, i]], xs_scr.at[s, :, i, :],
                                      sems.at[s]).start(priority=u % 2)
            return carry
        lax.fori_loop(0, tm // GATHER_UNROLL, group, 0)

    @pl.when(r == 0)
    def _():
        issue(idx_ref, 0)

    @pl.when(r + 1 < nv)
    def _():
        issue(idx_next_ref, 1 - slot)

    @pl.when(r < nv)
    def _():
        done = xs_scr.at[slot, :, 0:tm, :]
        pltpu.make_async_copy(done, done, sems.at[slot]).wait()
        ssq = jnp.zeros((tm, LANES), F32)
        for c in range(nc):
            xc = xs_scr[slot, c, 0:tm, :]
            ssq = ssq + xc * xc
        inv = lax.rsqrt(jnp.sum(ssq, axis=-1, keepdims=True) / d + RMS_EPS)
        for c in range(nc):
            cs = slice(c * LANES, (c + 1) * LANES)
            o_ref[:, cs] = (xs_scr[slot, c, 0:tm, :] * inv * g_ref[:, cs]).astype(o_ref.dtype)

    @pl.when(r >= nv)
    def _():
        o_ref[...] = jnp.zeros_like(o_ref)


def _gather_norm(x, g, buf_tok, nvalid, tm):
    n, d = x.shape
    nb = buf_tok.shape[0] // tm
    idx = buf_tok.reshape(nb, 1, tm)
    x3 = x.reshape(n, d // LANES, LANES)
    last = lambda r, nv: jnp.minimum(r, nv[0] - 1)
    return pl.pallas_call(
        functools.partial(_gather_norm_kernel, tm=tm, d=d),
        grid_spec=pltpu.PrefetchScalarGridSpec(
            num_scalar_prefetch=1,
            grid=(nb,),
            in_specs=[pl.BlockSpec((1, 1, tm), lambda r, nv: (last(r, nv), 0, 0),
                                   memory_space=pltpu.SMEM),
                      pl.BlockSpec((1, 1, tm), lambda r, nv: (last(r + 1, nv), 0, 0),
                                   memory_space=pltpu.SMEM),
                      pl.BlockSpec(memory_space=pl.ANY),
                      pl.BlockSpec((1, d), lambda r, nv: (0, 0))],
            out_specs=pl.BlockSpec((tm, d), lambda r, nv: (r, 0)),
            scratch_shapes=[pltpu.VMEM((2, d // LANES, tm + SUBLANES, LANES), F32),
                            pltpu.SemaphoreType.DMA((2,))],
        ),
        out_shape=jax.ShapeDtypeStruct((nb * tm, d), BF16),
        compiler_params=_params("arbitrary"),
        name="moe_gather_norm",
    )(nvalid, idx, idx, x3, g.reshape(1, d))


def _combine_kernel(idx_ref, idx_next_ref, x_ref, w_ref, y_hbm, *rest, tm, with_norm):
    if with_norm:
        g_ref, o_ref, h_ref, ys_scr, sems = rest
    else:
        o_ref, ys_scr, sems = rest
    r = pl.program_id(0)
    nr = pl.num_programs(0)
    slot = r % 2
    n_rows = TOP_K * tm

    @pl.when(r == 0)
    def _():
        _issue_row_gather(idx_ref, y_hbm, ys_scr.at[0], sems.at[0], n_rows)

    @pl.when(r + 1 < nr)
    def _():
        _issue_row_gather(idx_next_ref, y_hbm, ys_scr.at[1 - slot], sems.at[1 - slot], n_rows)

    _wait_row_gather(y_hbm, ys_scr.at[slot], sems.at[slot], n_rows)
    w0 = w_ref[:, 0:LANES]
    w1 = w_ref[:, LANES:2 * LANES]
    d = x_ref.shape[1]
    ssq = jnp.zeros((tm, LANES), F32)
    for c in range(d // LANES):
        cs = slice(c * LANES, (c + 1) * LANES)
        o = x_ref[:, cs] + (w0 * ys_scr[slot, 0:tm, cs] + w1 * ys_scr[slot, tm:2 * tm, cs])
        o_ref[:, cs] = o
        if with_norm:
            ssq = ssq + o * o
    if with_norm:
        inv = lax.rsqrt(jnp.sum(ssq, axis=-1, keepdims=True) / d + RMS_EPS)
        h_ref[...] = (o_ref[...] * inv * g_ref[...]).astype(h_ref.dtype)


def _combine(x, w_exp, yb, dest, next_norm_g, tm=128):
    n, d = x.shape
    nr = n // tm
    with_norm = next_norm_g is not None
    in_specs = [pl.BlockSpec((1, 1, TOP_K * tm), lambda r: (r, 0, 0), memory_space=pltpu.SMEM),
                pl.BlockSpec((1, 1, TOP_K * tm), lambda r: (jnp.minimum(r + 1, nr - 1), 0, 0),
                             memory_space=pltpu.SMEM),
                pl.BlockSpec((tm, d), lambda r: (r, 0)),
                pl.BlockSpec((tm, 2 * LANES), lambda r: (r, 0)),
                pl.BlockSpec(memory_space=pl.ANY)]
    args = [dest, dest, x, w_exp, yb]
    out_specs = [pl.BlockSpec((tm, d), lambda r: (r, 0))]
    out_shape = [jax.ShapeDtypeStruct((n, d), F32)]
    if with_norm:
        in_specs.append(pl.BlockSpec((1, d), lambda r: (0, 0)))
        args.append(next_norm_g.reshape(1, d))
        out_specs.append(pl.BlockSpec((tm, d), lambda r: (r, 0)))
        out_shape.append(jax.ShapeDtypeStruct((n, d), BF16))
    res = pl.pallas_call(
        functools.partial(_combine_kernel, tm=tm, with_norm=with_norm),
        grid=(nr,),
        in_specs=in_specs,
        out_specs=out_specs,
        out_shape=out_shape,
        scratch_shapes=[pltpu.VMEM((2, TOP_K * tm, d), F32), pltpu.SemaphoreType.DMA((2,))],
        compiler_params=_params("arbitrary"),
        name="moe_combine",
    )(*args)
    return (res[0], res[1]) if with_norm else (res[0], None)


def _ffn_up_kernel(be_ref, nv_ref, first_ref, run_ref, nxt_ref, nruns_ref, x_ref, wg_hbm, wu_hbm,
                   o_ref, stage, w_scr, sems, *, li, tf, nf):
    j = pl.program_id(0)
    r = pl.program_id(1)
    valid = r < nv_ref[0]
    cols = lambda jj: pl.ds(pl.multiple_of(jj * tf, LANES), tf)

    def copies(e, jj, s):
        return (pltpu.make_async_copy(wg_hbm.at[li, e, :, cols(jj)], stage.at[s, 0], sems.at[s]),
                pltpu.make_async_copy(wu_hbm.at[li, e, :, cols(jj)], stage.at[s, 1], sems.at[s]))

    @pl.when((j == 0) & (r == 0))
    def _():
        for cp in copies(be_ref[0], 0, 0):
            cp.start()

    @pl.when(valid & (first_ref[r] == 1))
    def _():
        s = (j * nruns_ref[0] + run_ref[r]) % 2
        nxt = nxt_ref[r]

        @pl.when(nxt >= 0)
        def _():
            for cp in copies(nxt, j, 1 - s):
                cp.start()

        @pl.when((nxt < 0) & (j + 1 < nf))
        def _():
            for cp in copies(be_ref[0], j + 1, 1 - s):
                cp.start()

        for cp in copies(be_ref[r], j, s):
            cp.wait()
        w_scr[:, 0:tf] = stage[s, 0].astype(BF16)
        w_scr[:, tf:2 * tf] = stage[s, 1].astype(BF16)

    @pl.when(valid)
    def _():
        gu = jnp.dot(x_ref[...], w_scr[...], preferred_element_type=F32)
        o_ref[...] = (jax.nn.silu(gu[:, 0:tf]) * gu[:, tf:2 * tf]).astype(o_ref.dtype)

    @pl.when(jnp.logical_not(valid))
    def _():
        o_ref[...] = jnp.zeros_like(o_ref)


def _ffn_down_kernel(be_ref, nv_ref, first_ref, run_ref, nxt_ref, a_ref, wd_hbm, o_ref,
                     stage, w_scr, sems, *, li):
    r = pl.program_id(0)
    valid = r < nv_ref[0]
    copy = lambda e, s: pltpu.make_async_copy(wd_hbm.at[li, e], stage.at[s], sems.at[s])

    @pl.when(r == 0)
    def _():
        copy(be_ref[0], 0).start()

    @pl.when(valid & (first_ref[r] == 1))
    def _():
        s = run_ref[r] % 2
        nxt = nxt_ref[r]

        @pl.when(nxt >= 0)
        def _():
            copy(nxt, 1 - s).start()

        copy(be_ref[r], s).wait()
        w_scr[...] = stage[s].astype(BF16)

    @pl.when(valid)
    def _():
        o_ref[...] = jnp.dot(a_ref[...], w_scr[...], preferred_element_type=F32)

    @pl.when(jnp.logical_not(valid))
    def _():
        o_ref[...] = jnp.zeros_like(o_ref)


def _expert_ffn(xg, blk_e, nvalid, w_gate, w_up, w_down, li, tm, tf=384):
    p, d = xg.shape
    f = w_gate.shape[3]
    nb = p // tm
    nf = f // tf
    tiles = jnp.arange(nb, dtype=jnp.int32)
    valid = tiles < nvalid[0]
    first = valid & ((tiles == 0) | (blk_e != jnp.roll(blk_e, 1)))
    run = (jnp.cumsum(first.astype(jnp.int32)) - 1).astype(jnp.int32)
    nruns = jnp.sum(first.astype(jnp.int32)).reshape(1)
    run_e = jnp.sum(jnp.where(first[None, :] & (run[None, :] == tiles[:, None]), blk_e[None, :], 0),
                    axis=1)
    nxt = jnp.sum(jnp.where(tiles[None, :] == run[:, None] + 1, run_e[None, :], 0), axis=1)
    nxt = jnp.where(run + 1 < nruns[0], nxt, -1).astype(jnp.int32)
    first = first.astype(jnp.int32)
    last = lambda r, nv: jnp.minimum(r, nv[0] - 1)
    act = pl.pallas_call(
        functools.partial(_ffn_up_kernel, li=li, tf=tf, nf=nf),
        grid_spec=pltpu.PrefetchScalarGridSpec(
            num_scalar_prefetch=6,
            grid=(nf, nb),
            in_specs=[pl.BlockSpec((tm, d), lambda j, r, be, nv, *_: (last(r, nv), 0)),
                      pl.BlockSpec(memory_space=pl.ANY),
                      pl.BlockSpec(memory_space=pl.ANY)],
            out_specs=pl.BlockSpec((tm, tf), lambda j, r, *_: (r, j)),
            scratch_shapes=[pltpu.VMEM((2, 2, d, tf), F32), pltpu.VMEM((d, 2 * tf), BF16),
                            pltpu.SemaphoreType.DMA((2,))],
        ),
        out_shape=jax.ShapeDtypeStruct((p, f), BF16),
        compiler_params=_params("arbitrary", "arbitrary"),
        name="moe_ffn_up",
    )(blk_e, nvalid, first, run, nxt, nruns, xg, w_gate, w_up)
    return pl.pallas_call(
        functools.partial(_ffn_down_kernel, li=li),
        grid_spec=pltpu.PrefetchScalarGridSpec(
            num_scalar_prefetch=5,
            grid=(nb,),
            in_specs=[pl.BlockSpec((tm, f), lambda r, be, nv, *_: (last(r, nv), 0)),
                      pl.BlockSpec(memory_space=pl.ANY)],
            out_specs=pl.BlockSpec((tm, d), lambda r, *_: (r, 0)),
            scratch_shapes=[pltpu.VMEM((2, f, d), F32), pltpu.VMEM((f, d), BF16),
                            pltpu.SemaphoreType.DMA((2,))],
        ),
        out_shape=jax.ShapeDtypeStruct((p, d), F32),
        compiler_params=_params("arbitrary"),
        name="moe_ffn_down",
    )(blk_e, nvalid, first, run, nxt, act, w_down)


def _hier_moe(x, norm_g, w_group, b_group, w_router, b_router, w_gate, w_up, w_down, li,
              next_norm_g, tm=MOE_TM, tc=128):
    n_tok, d = x.shape
    sel, w_exp = _router(x, norm_g, w_group, b_group, w_router, b_router)

    n_as = n_tok * TOP_K
    nb = -(-n_as // tm) + N_EXPERTS
    p_rows = nb * tm
    n_pad = p_rows - n_as
    flat_e = sel[:, :TOP_K].reshape(n_as)
    experts = jnp.arange(N_EXPERTS, dtype=jnp.int32)
    counts = jnp.sum((flat_e[:, None] == experts[None, :]).astype(jnp.int32), axis=0)
    pcounts = (counts + tm - 1) // tm * tm
    nvalid = (jnp.sum(pcounts) // tm).astype(jnp.int32).reshape(1)
    pad_end = jnp.cumsum(pcounts - counts)
    pad_e = jnp.sum((jnp.arange(n_pad, dtype=jnp.int32)[:, None] >= pad_end[None, :]).astype(jnp.int32),
                    axis=1)
    keys = jnp.concatenate([2 * flat_e, 2 * pad_e + 1])
    vals = jnp.concatenate([jnp.arange(n_as, dtype=jnp.int32), jnp.full((n_pad,), -1, jnp.int32)])
    skeys, svals = lax.sort((keys, vals), num_keys=1)
    buf_tok = jnp.where(svals >= 0, svals // TOP_K, jnp.arange(p_rows, dtype=jnp.int32) % n_tok)
    _, pos = lax.sort((svals, jnp.arange(p_rows, dtype=jnp.int32)), num_keys=1)
    dest = pos[n_pad:].reshape(n_tok, TOP_K)
    blk_e = jnp.minimum(skeys.reshape(nb, tm)[:, 0] // 2, N_EXPERTS - 1).astype(jnp.int32)

    xg = _gather_norm(x, norm_g, buf_tok, nvalid, tm)
    yb = _expert_ffn(xg, blk_e, nvalid, w_gate, w_up, w_down, li, tm)
    dest_t = dest.reshape(n_tok // tc, tc, TOP_K).transpose(0, 2, 1).reshape(n_tok // tc, 1, TOP_K * tc)
    return _combine(x, w_exp, yb, dest_t, next_norm_g, tm=tc)


def _ab_layer(x, h, w_in, li, conv_w, conv_b, wa, ba, wi, bi, lam, bf, qn, kn, w_out, bsz):
    n_tok, d = x.shape
    t = n_tok // bsz
    d_rnn = conv_w.shape[1]
    nh = bf.shape[0]
    fw = nh * HEAD_DIM
    zga = _proj_f32(h, w_in, li, 0, 2 * d_rnn, wt=True).reshape(bsz, t, 2 * d_rnn)
    ya = _rglru(zga, bsz, conv_w, conv_b, wa, ba, wi, bi, lam)
    normw = jnp.concatenate([jnp.tile(qn, nh), jnp.tile(kn * FOX_QK_SCALE, nh),
                             jnp.ones((fw,), F32)]).reshape(1, 3 * fw)
    qkv = _proj_heads(h, w_in, li, 2 * d_rnn, 3 * fw, normw, 2 * fw, bsz, 1, wt=True)
    b_f = jnp.pad(bf.astype(F32), (0, LANES - nh)).reshape(1, LANES)
    cum = _forget_cum(h, w_in, li, 2 * d_rnn + 3 * fw, nh, b_f, bsz, wt=True)
    fox_tq = 512
    cum_row = cum[:, :, :nh].transpose(0, 2, 1).reshape(bsz, nh, t // fox_tq, fox_tq)
    yb = _fox_attention(qkv, cum, cum_row, tq=fox_tq)
    return _proj_res([ya.reshape(n_tok, d_rnn), yb.reshape(n_tok, fw)], w_out, li, x)


def _dilated_layer(x, h, w_in, li, qn, kn, table, w_out, bsz):
    n_tok, d = x.shape
    t = n_tok // bsz
    nh = table.shape[1] // len(DSW_PATTERNS)
    fw = nh * HEAD_DIM
    os_, lses = [], []
    for g, (window, dil) in enumerate(DSW_PATTERNS):
        normw = jnp.concatenate([jnp.tile(qn[g], nh), jnp.tile(kn[g], nh),
                                 jnp.ones((fw,), F32)]).reshape(1, 3 * fw)
        qkv = _proj_heads(h, w_in, li, g * 3 * fw, 3 * fw, normw, 2 * fw, bsz, dil)
        bias = _dsw_bias(table[:, g * nh:(g + 1) * nh], window, dil)
        o, lse = _dsw_group(qkv, bias, min(t // dil, 1024))
        os_.append(o)
        lses.append(lse)
    y = _dsw_merge(os_, lses)
    return _proj_res([y], w_out, li, x)


def kernel(x, mix_norm, ffn_norm, w_in_ab, conv_w, conv_b, lru_wa, lru_ba, lru_wi, lru_bi,
           lru_lambda, fox_bf, fox_qnorm, fox_knorm, w_out_ab, w_in_c, dsw_qnorm, dsw_knorm,
           w_out_c, rel_bias, moe_w_group, moe_b_group, moe_w_router, moe_b_router,
           moe_w_gate, moe_w_up, moe_w_down):
    bsz, t, d = x.shape
    depth = mix_norm.shape[0]
    xf = x.reshape(bsz * t, d)
    h = _rmsnorm(xf, mix_norm[0])
    for layer in range(depth):
        i = layer // 2
        if layer % 2 == 0:
            xf = _ab_layer(xf, h, jnp.swapaxes(w_in_ab, 1, 2), i, conv_w[i], conv_b[i], lru_wa[i],
                           lru_ba[i], lru_wi[i], lru_bi[i], lru_lambda[i], fox_bf[i],
                           fox_qnorm[i], fox_knorm[i], w_out_ab, bsz)
        else:
            xf = _dilated_layer(xf, h, w_in_c, i, dsw_qnorm[i], dsw_knorm[i],
                                rel_bias, w_out_c, bsz)
        next_g = mix_norm[layer + 1] if layer + 1 < depth else None
        xf, h = _hier_moe(xf, ffn_norm[layer], moe_w_group[layer], moe_b_group[layer],
                          moe_w_router[layer], moe_b_router[layer], moe_w_gate, moe_w_up,
                          moe_w_down, layer, next_g)
    return xf.reshape(bsz, t, d)
```

```python
import functools
import math

import jax
import jax.numpy as jnp
from jax import lax
from jax.experimental import pallas as pl
from jax.experimental.pallas import tpu as pltpu

F32 = jnp.float32
BF16 = jnp.bfloat16

LANES = 128
SUBLANES = 8
VMEM_LIMIT = 56 * 1024 * 1024

HEAD_DIM = 128
RMS_EPS = 1e-6
RNN_BLOCKS = 16
CONV_WIDTH = 4
LRU_C = 8.0
DSW_PATTERNS = ((128, 1), (512, 4), (2048, 16))
DSW_BLOCK = 128
DSW_CHAINS = 8
DSW_SKEW = 2
N_BUCKETS = 32
MAX_DISTANCE = 2048
N_GROUPS = 4
EXPERTS_PER_GROUP = 8
N_EXPERTS = N_GROUPS * EXPERTS_PER_GROUP
TOP_K = 2
LOG2E = math.log2(math.e)

MOE_TM = 256
PROJ_SUB = 256
MAX_ROW_STRIDE = 4


def _params(*sem):
    return pltpu.CompilerParams(dimension_semantics=sem, vmem_limit_bytes=VMEM_LIMIT)


def _rmsnorm_kernel(x_ref, g_ref, o_ref):
    x = x_ref[...]
    ms = jnp.mean(x * x, axis=-1, keepdims=True)
    o_ref[...] = (x * lax.rsqrt(ms + RMS_EPS) * g_ref[...]).astype(o_ref.dtype)


def _rmsnorm(x, g, tm=256):
    m, d = x.shape
    return pl.pallas_call(
        _rmsnorm_kernel,
        grid=(m // tm,),
        in_specs=[pl.BlockSpec((tm, d), lambda i: (i, 0)),
                  pl.BlockSpec((1, d), lambda i: (0, 0))],
        out_specs=pl.BlockSpec((tm, d), lambda i: (i, 0)),
        out_shape=jax.ShapeDtypeStruct((m, d), BF16),
        compiler_params=_params("parallel"),
        name="rmsnorm",
    )(x, g.reshape(1, d))


def _mm(a, w_blk, wt):
    w = w_blk.astype(BF16)
    if wt:
        return lax.dot_general(a, w, (((1,), (1,)), ((), ())), preferred_element_type=F32)
    return jnp.dot(a, w, preferred_element_type=F32)


def _w_spec(k, tn, li, jb, wt, grid_rank):
    if wt:
        return pl.BlockSpec((None, tn, k), lambda *g: (li, jb + g[grid_rank - 1], 0))
    return pl.BlockSpec((None, k, tn), lambda *g: (li, 0, jb + g[grid_rank - 1]))


def _proj_f32_kernel(a_ref, w_ref, o_ref, *, wt):
    for sb in range(o_ref.shape[1] // PROJ_SUB):
        cs = slice(sb * PROJ_SUB, (sb + 1) * PROJ_SUB)
        o_ref[:, cs] = _mm(a_ref[...], w_ref[cs, :] if wt else w_ref[:, cs], wt)


def _proj_f32(a, w, li, col0, ncols, wt=False, tm=1024, tn=512):
    m, k = a.shape
    jb = col0 // tn
    return pl.pallas_call(
        functools.partial(_proj_f32_kernel, wt=wt),
        grid=(m // tm, ncols // tn),
        in_specs=[pl.BlockSpec((tm, k), lambda i, j: (i, 0)),
                  _w_spec(k, tn, li, jb, wt, 2)],
        out_specs=pl.BlockSpec((tm, tn), lambda i, j: (i, j)),
        out_shape=jax.ShapeDtypeStruct((m, ncols), F32),
        compiler_params=_params("parallel", "arbitrary"),
        name="proj_f32",
    )(a, w)


def _proj_res_kernel(*refs, n_in):
    a_refs, w_refs = refs[:n_in], refs[n_in:2 * n_in]
    r_ref, o_ref = refs[2 * n_in], refs[2 * n_in + 1]
    for sb in range(o_ref.shape[1] // PROJ_SUB):
        cs = slice(sb * PROJ_SUB, (sb + 1) * PROJ_SUB)
        acc = r_ref[:, cs]
        for a_ref, w_ref in zip(a_refs, w_refs):
            acc = acc + jnp.dot(a_ref[...], w_ref[:, cs].astype(BF16), preferred_element_type=F32)
        o_ref[:, cs] = acc


def _proj_res(a_list, w, li, res, tm=1024, tn=512):
    m, k = a_list[0].shape
    n_in = len(a_list)
    n = w.shape[2]
    a_specs = [pl.BlockSpec((tm, k), lambda i, j: (i, 0)) for _ in a_list]
    w_specs = [pl.BlockSpec((None, k, tn), lambda i, j, p=p: (li, p, j)) for p in range(n_in)]
    return pl.pallas_call(
        functools.partial(_proj_res_kernel, n_in=n_in),
        grid=(m // tm, n // tn),
        in_specs=a_specs + w_specs + [pl.BlockSpec((tm, tn), lambda i, j: (i, j))],
        out_specs=pl.BlockSpec((tm, tn), lambda i, j: (i, j)),
        out_shape=jax.ShapeDtypeStruct((m, n), F32),
        compiler_params=_params("parallel", "arbitrary"),
        name="proj_res",
    )(*a_list, *([w] * n_in), res)


def _proj_heads_kernel(a_ref, w_ref, nw_ref, o_ref, acc_scr, y_scr, t_scr, *,
                       dil, n_norm_steps, nj, nsteps, tm, wt):
    s = pl.program_id(0)
    hps = acc_scr.shape[2] // HEAD_DIM
    rows = tm // dil

    @pl.when(s == 0)
    def _():
        acc_scr[1] = jnp.zeros(acc_scr.shape[1:], F32)

    prev = jnp.maximum(s - 1, 0)
    is_norm = (prev % nj) < n_norm_steps

    def step(slot):
        acc_scr[slot] = _mm(a_ref[...], w_ref[...], wt)
        acc = acc_scr[1 - slot]
        for hh in range(hps):
            y = acc[:, hh * HEAD_DIM:(hh + 1) * HEAD_DIM]
            ms = jnp.mean(y * y, axis=-1, keepdims=True)
            yn = y * lax.rsqrt(ms + RMS_EPS) * nw_ref[:, hh * HEAD_DIM:(hh + 1) * HEAD_DIM]
            y = jnp.where(is_norm, yn, y)
            if dil == 1:
                o_ref[0, 0, hh] = y.astype(o_ref.dtype)
            else:
                y_scr[hh] = y
        if dil > 1:
            s1 = min(dil, MAX_ROW_STRIDE)
            s2 = dil // s1
            for hh in range(hps):
                for z1 in range(s1):
                    if s2 == 1:
                        o_ref[0, z1, hh] = (
                            y_scr[hh, pl.ds(z1, rows, stride=s1), :].astype(o_ref.dtype))
                        continue
                    t_scr[z1] = y_scr[hh, pl.ds(z1, tm // s1, stride=s1), :]
                    for z2 in range(s2):
                        o_ref[0, z1 + s1 * z2, hh] = (
                            t_scr[z1, pl.ds(z2, rows, stride=s2), :].astype(o_ref.dtype))

    @pl.when(s % 2 == 0)
    def _():
        step(0)

    @pl.when(s % 2 == 1)
    def _():
        step(1)


def _proj_heads(a, w, li, col0, ncols, normw, n_norm_cols, bsz, dil, wt=False, tm=1024, tn=512):
    m, k = a.shape
    t = m // bsz
    nt = t // tm
    hps = tn // HEAD_DIM
    nj = ncols // tn
    jb = col0 // tn
    nsteps = bsz * nt * nj

    def mm_step(s):
        s = jnp.minimum(s, nsteps - 1)
        return s // nj, s % nj

    def out_step(s):
        s = jnp.maximum(s - 1, 0)
        return s // (nj * nt), (s // nj) % nt, s % nj

    if wt:
        w_spec = pl.BlockSpec((None, tn, k), lambda s: (li, jb + mm_step(s)[1], 0))
    else:
        w_spec = pl.BlockSpec((None, k, tn), lambda s: (li, 0, jb + mm_step(s)[1]))
    kern = functools.partial(_proj_heads_kernel, dil=dil, n_norm_steps=n_norm_cols // tn, nj=nj,
                             nsteps=nsteps, tm=tm, wt=wt)
    return pl.pallas_call(
        kern,
        grid=(nsteps + 1,),
        in_specs=[pl.BlockSpec((tm, k), lambda s: (mm_step(s)[0], 0)),
                  w_spec,
                  pl.BlockSpec((1, tn), lambda s: (0, out_step(s)[2]))],
        out_specs=pl.BlockSpec((1, dil, hps, tm // dil, HEAD_DIM),
                               lambda s: (out_step(s)[0], 0, out_step(s)[2], out_step(s)[1], 0)),
        out_shape=jax.ShapeDtypeStruct((bsz, dil, ncols // HEAD_DIM, t // dil, HEAD_DIM), BF16),
        scratch_shapes=[pltpu.VMEM((2, tm, tn), F32),
                        pltpu.VMEM((hps, tm, HEAD_DIM), F32),
                        pltpu.VMEM((MAX_ROW_STRIDE, tm // MAX_ROW_STRIDE, HEAD_DIM), F32)],
        compiler_params=_params("arbitrary"),
        name=f"proj_heads_d{dil}",
    )(a, w, normw)


def _forget_cum_kernel(a_ref, w_ref, b_ref, o_ref, carry_ref, *, tm, nh, wt):
    i = pl.program_id(1)

    @pl.when(i == 0)
    def _():
        carry_ref[...] = jnp.zeros_like(carry_ref)

    lane = lax.broadcasted_iota(jnp.int32, (tm, LANES), 1)
    fz = _mm(a_ref[...], w_ref[...], wt)
    c = jax.nn.log_sigmoid(jnp.where(lane < nh, fz + b_ref[...], 0.0))
    row = lax.broadcasted_iota(jnp.int32, c.shape, 0)
    s = 1
    while s < tm:
        c = c + jnp.where(row >= s, pltpu.roll(c, s, 0), 0.0)
        s *= 2
    c = c + carry_ref[0:1, :]
    o_ref[0] = c
    carry_ref[...] = jnp.broadcast_to(c[tm - 1:tm, :], carry_ref.shape)


def _forget_cum(a, w, li, col0, nh, b_f, bsz, wt=False, tm=512):
    m, k = a.shape
    t = m // bsz
    nt = t // tm
    jb = col0 // LANES
    return pl.pallas_call(
        functools.partial(_forget_cum_kernel, tm=tm, nh=nh, wt=wt),
        grid=(bsz, nt),
        in_specs=[pl.BlockSpec((tm, k), lambda b, i: (b * nt + i, 0)),
                  (pl.BlockSpec((None, LANES, k), lambda b, i: (li, jb, 0)) if wt
                   else pl.BlockSpec((None, k, LANES), lambda b, i: (li, 0, jb))),
                  pl.BlockSpec((1, LANES), lambda b, i: (0, 0))],
        out_specs=pl.BlockSpec((1, tm, LANES), lambda b, i: (b, i, 0)),
        out_shape=jax.ShapeDtypeStruct((bsz, t, LANES), F32),
        scratch_shapes=[pltpu.VMEM((SUBLANES, LANES), F32)],
        compiler_params=_params("parallel", "arbitrary"),
        name="forget_cum",
    )(a, w, b_f)


def _gelu_tanh(x):
    return 0.5 * x * (1.0 + jnp.tanh(math.sqrt(2.0 / math.pi) * (x + 0.044715 * (x * x * x))))


def _rglru_kernel(g_ref, x_ref, cw_ref, cb_ref, wa_ref, ba_ref, wi_ref, bi_ref, lam_ref,
                  o_ref, xs_scr, a_scr, u_scr, h_scr, carry_scr, *, tc, nblk):
    ti = pl.program_id(2)
    halo = SUBLANES

    @pl.when(ti == 0)
    def _():
        xs_scr[0:halo, :] = jnp.zeros((halo, xs_scr.shape[1]), F32)
        carry_scr[...] = jnp.zeros_like(carry_scr)

    xs_scr[halo:halo + tc, :] = x_ref[0]
    xc = cb_ref[...] + cw_ref[CONV_WIDTH - 1:CONV_WIDTH, :] * xs_scr[halo:halo + tc, :]
    for j in range(CONV_WIDTH - 1):
        sh = CONV_WIDTH - 1 - j
        xc = xc + cw_ref[j:j + 1, :] * xs_scr[halo - sh:halo - sh + tc, :]
    xs_scr[0:halo, :] = xs_scr[tc:tc + halo, :]

    neg_sp = -LRU_C * jax.nn.softplus(-lam_ref[...])
    for n in range(nblk):
        cs = slice(n * HEAD_DIM, (n + 1) * HEAD_DIM)
        xb = xc[:, cs]
        xbb = xb.astype(BF16)
        r = jax.nn.sigmoid(jnp.dot(xbb, wa_ref[n].astype(BF16), preferred_element_type=F32)
                           + ba_ref[:, cs])
        ig = jax.nn.sigmoid(jnp.dot(xbb, wi_ref[n].astype(BF16), preferred_element_type=F32)
                            + bi_ref[:, cs])
        log_a = neg_sp[:, cs] * r
        a = jnp.exp(log_a)
        a_scr[:, cs] = a
        u_scr[:, cs] = jnp.sqrt(-jnp.tanh(log_a) * (a * a + 1.0)) * (ig * xb)

    row = lax.broadcasted_iota(jnp.int32, (SUBLANES, a_scr.shape[1]), 0)

    def body(g, carry):
        r0 = pl.multiple_of(g * SUBLANES, SUBLANES)
        a = a_scr[pl.ds(r0, SUBLANES), :]
        u = u_scr[pl.ds(r0, SUBLANES), :]
        s = 1
        while s < SUBLANES:
            keep = row >= s
            u = jnp.where(keep, a * pltpu.roll(u, s, 0) + u, u)
            a = jnp.where(keep, a * pltpu.roll(a, s, 0), a)
            s *= 2
        h = a * carry + u
        h_scr[pl.ds(r0, SUBLANES), :] = h
        return jnp.broadcast_to(h[SUBLANES - 1:SUBLANES, :], h.shape)

    carry = lax.fori_loop(0, tc // SUBLANES, body, carry_scr[...], unroll=4)
    carry_scr[...] = carry
    o_ref[0] = (h_scr[...] * _gelu_tanh(g_ref[0])).astype(o_ref.dtype)


def _rglru(zga, bsz, conv_w, conv_b, wa, ba, wi, bi, lam, tc=256, cw=512):
    _, t, c2 = zga.shape
    c = c2 // 2
    nblk = cw // HEAD_DIM
    ncb = c // cw
    vec = lambda v: v.reshape(1, c)
    vspec = pl.BlockSpec((1, cw), lambda b, j, i: (0, j))
    kern = functools.partial(_rglru_kernel, tc=tc, nblk=nblk)
    return pl.pallas_call(
        kern,
        grid=(bsz, ncb, t // tc),
        in_specs=[pl.BlockSpec((1, tc, cw), lambda b, j, i: (b, i, j)),
                  pl.BlockSpec((1, tc, cw), lambda b, j, i: (b, i, ncb + j)),
                  pl.BlockSpec((CONV_WIDTH, cw), lambda b, j, i: (0, j)),
                  vspec,
                  pl.BlockSpec((nblk, HEAD_DIM, HEAD_DIM), lambda b, j, i: (j, 0, 0)),
                  vspec,
                  pl.BlockSpec((nblk, HEAD_DIM, HEAD_DIM), lambda b, j, i: (j, 0, 0)),
                  vspec, vspec],
        out_specs=pl.BlockSpec((1, tc, cw), lambda b, j, i: (b, i, j)),
        out_shape=jax.ShapeDtypeStruct((bsz, t, c), BF16),
        scratch_shapes=[pltpu.VMEM((tc + SUBLANES, cw), F32),
                        pltpu.VMEM((tc, cw), F32),
                        pltpu.VMEM((tc, cw), F32),
                        pltpu.VMEM((tc, cw), F32),
                        pltpu.VMEM((SUBLANES, cw), F32)],
        compiler_params=_params("parallel", "parallel", "arbitrary"),
        name="rglru",
    )(zga, zga, conv_w, vec(conv_b), wa, vec(ba), wi, vec(bi), vec(lam))


FOX_STRIP = 128
FOX_QK_SCALE = LOG2E / math.sqrt(HEAD_DIM)
FOX_SKEW = 3


def _fox_kernel(q_ref, k_ref, v_ref, cq_ref, ck_ref, o_ref, m_scr, l_scr, acc_scr, cq_scr, *, tq, hp):
    g = pl.program_id(1)
    qi = pl.program_id(2)
    tk = tq
    sr = FOX_STRIP
    lane = lax.broadcasted_iota(jnp.int32, (tq, LANES), 1)
    for hh in range(hp):
        cq = jnp.sum(jnp.where(lane == g * hp + hh, cq_ref[0], 0.0), axis=-1, keepdims=True)
        cq_scr[hh] = jnp.broadcast_to(cq * LOG2E, (tq, LANES))
        m_scr[hh] = jnp.full(m_scr.shape[1:], -jnp.inf, F32)
        l_scr[hh] = jnp.zeros(l_scr.shape[1:], F32)
        acc_scr[hh] = jnp.zeros(acc_scr.shape[1:], F32)
    tri = (lax.broadcasted_iota(jnp.int32, (sr, LANES), 1)
           <= lax.broadcasted_iota(jnp.int32, (sr, LANES), 0))

    def scores(hh, r, j, diagonal):
        rs = slice(r * sr, (r + 1) * sr)
        k0 = pl.multiple_of(j * tk, tk)
        nk = (r + 1) * sr if diagonal else tk
        return lax.dot_general(q_ref[0, 0, hh, rs, :], k_ref[0, 0, hh, pl.ds(k0, nk), :],
                               (((1,), (1,)), ((), ())), preferred_element_type=F32)

    def softmax_pv(s, hh, r, j, ck, diagonal):
        rs = slice(r * sr, (r + 1) * sr)
        k0 = pl.multiple_of(j * tk, tk)
        nk = s.shape[1]
        cq = cq_scr[hh, rs, :]
        parts = []
        for c in range(nk // LANES):
            cs = slice(c * LANES, (c + 1) * LANES)
            part = s[:, cs] + (cq - ck[:, cs])
            if diagonal and c == r:
                part = jnp.where(tri, part, -jnp.inf)
            parts.append(part)
        m_prev = m_scr[hh, rs, :]
        m_new = jnp.maximum(m_prev, jnp.max(functools.reduce(jnp.maximum, parts),
                                            axis=-1, keepdims=True))
        alpha = jnp.exp2(m_prev - m_new)
        ps = [jnp.exp2(part - m_new) for part in parts]
        row_sum = jnp.sum(functools.reduce(lambda a, b: a + b, ps), axis=-1, keepdims=True)
        l_scr[hh, rs, :] = alpha * l_scr[hh, rs, :] + row_sum
        p = jnp.concatenate(ps, axis=1).astype(BF16)
        acc_scr[hh, rs, :] = alpha * acc_scr[hh, rs, :] + jnp.dot(
            p, v_ref[0, 0, hh, pl.ds(k0, nk), :], preferred_element_type=F32)
        m_scr[hh, rs, :] = m_new

    def kv_step(j, diagonal):
        cks = [ck_ref[0, hh, pl.ds(j, 1), :] * LOG2E for hh in range(hp)]
        strips = [(hh, r) for hh in range(hp) for r in range(tq // sr)]
        pending = [scores(*strips[n], j, diagonal) for n in range(FOX_SKEW)]
        for n, (hh, r) in enumerate(strips):
            if n + FOX_SKEW < len(strips):
                pending.append(scores(*strips[n + FOX_SKEW], j, diagonal))
            softmax_pv(pending.pop(0), hh, r, j, cks[hh], diagonal)

    def below_diagonal(j, carry):
        kv_step(j, False)
        return carry

    lax.fori_loop(0, qi, below_diagonal, 0)
    kv_step(qi, True)
    for hh in range(hp):
        o_ref[0, :, hh * HEAD_DIM:(hh + 1) * HEAD_DIM] = (acc_scr[hh] / l_scr[hh]).astype(o_ref.dtype)


def _fox_attention(qkv, cum_col, cum_row, tq=512, hp=2):
    bsz, _, h3, t, dh = qkv.shape
    nh = h3 // 3
    nq = t // tq
    ng = nh // hp
    kern = functools.partial(_fox_kernel, tq=tq, hp=hp)
    return pl.pallas_call(
        kern,
        grid=(bsz, ng, nq),
        in_specs=[pl.BlockSpec((1, 1, hp, tq, dh), lambda b, g, i: (b, 0, g, i, 0)),
                  pl.BlockSpec((1, 1, hp, t, dh), lambda b, g, i: (b, 0, ng + g, 0, 0)),
                  pl.BlockSpec((1, 1, hp, t, dh), lambda b, g, i: (b, 0, 2 * ng + g, 0, 0)),
                  pl.BlockSpec((1, tq, LANES), lambda b, g, i: (b, i, 0)),
                  pl.BlockSpec((1, hp, nq, tq), lambda b, g, i: (b, g, 0, 0))],
        out_specs=pl.BlockSpec((1, tq, hp * dh), lambda b, g, i: (b, i, g)),
        out_shape=jax.ShapeDtypeStruct((bsz, t, nh * dh), BF16),
        scratch_shapes=[pltpu.VMEM((hp, tq, LANES), F32), pltpu.VMEM((hp, tq, LANES), F32),
                        pltpu.VMEM((hp, tq, dh), F32), pltpu.VMEM((hp, tq, LANES), F32)],
        compiler_params=_params("parallel", "parallel", "arbitrary"),
        name="fox_attention",
    )(qkv, qkv, qkv, cum_col, cum_row)


def _dsw_kernel(q_ref, kc_ref, kp_ref, vc_ref, vp_ref, bias_ref, o_ref, lse_ref, *, lc, nh):
    c = pl.program_id(2)
    blk = DSW_BLOCK
    nb = lc // blk
    lane = lax.broadcasted_iota(jnp.int32, (blk, LANES), 1)
    col = lax.broadcasted_iota(jnp.int32, (blk, 2 * blk), 1)
    scale = 1.0 / math.sqrt(HEAD_DIM)
    lse_ref[...] = jnp.zeros_like(lse_ref)

    def scores(h, n):
        rs = slice(n * blk, (n + 1) * blk)
        if n == 0:
            kprev = kp_ref[0, 0, h]
        else:
            kprev = kc_ref[0, 0, h, (n - 1) * blk:n * blk, :]
        kk = jnp.concatenate([kprev, kc_ref[0, 0, h, rs, :]], axis=0)
        return lax.dot_general(q_ref[0, 0, h, rs, :], kk, (((1,), (1,)), ((), ())),
                               preferred_element_type=F32)

    def softmax_pv(s, h, n):
        rs = slice(n * blk, (n + 1) * blk)
        if n == 0:
            vprev = vp_ref[0, 0, h]
        else:
            vprev = vc_ref[0, 0, h, (n - 1) * blk:n * blk, :]
        vv = jnp.concatenate([vprev, vc_ref[0, 0, h, rs, :]], axis=0)
        s = s * scale + bias_ref[h]
        if n == 0:
            s = jnp.where((c > 0) | (col >= blk), s, -jnp.inf)
        m = jnp.max(s, axis=-1, keepdims=True)
        e = jnp.exp(s - m)
        den = jnp.sum(e, axis=-1, keepdims=True)
        o = jnp.dot(e.astype(BF16), vv, preferred_element_type=F32) / den
        o_ref[0, 0, rs, pl.ds(pl.multiple_of(h * HEAD_DIM, HEAD_DIM), HEAD_DIM)] = (
            o.astype(o_ref.dtype))
        lse = m + jnp.log(den)
        lse_ref[0, 0, rs, :] = jnp.where(lane == h, lse, lse_ref[0, 0, rs, :])

    hu = math.gcd(nh, max(1, DSW_CHAINS // nb))

    def heads(i, carry):
        blocks = [(i * hu + hh, n) for hh in range(hu) for n in range(nb)]
        pending = [scores(*blocks[k]) for k in range(DSW_SKEW)]
        for k, (h, n) in enumerate(blocks):
            if k + DSW_SKEW < len(blocks):
                pending.append(scores(*blocks[k + DSW_SKEW]))
            softmax_pv(pending.pop(0), h, n)
        return carry

    lax.fori_loop(0, nh // hu, heads, 0)


def _dsw_group(qkv, bias, lc):
    bsz, dil, h3, l, dh = qkv.shape
    nh = h3 // 3
    nc = l // lc
    bpc = lc // DSW_BLOCK
    kern = functools.partial(_dsw_kernel, lc=lc, nh=nh)
    cur = lambda off: pl.BlockSpec((1, 1, nh, lc, dh), lambda b, z, c: (b, z, off, c, 0))
    prev = lambda off: pl.BlockSpec((1, 1, nh, DSW_BLOCK, dh),
                                    lambda b, z, c: (b, z, off, jnp.maximum(c * bpc - 1, 0), 0))
    return pl.pallas_call(
        kern,
        grid=(bsz, dil, nc),
        in_specs=[cur(0), cur(1), prev(1), cur(2), prev(2),
                  pl.BlockSpec((nh, DSW_BLOCK, 2 * DSW_BLOCK), lambda b, z, c: (0, 0, 0))],
        out_specs=[pl.BlockSpec((1, 1, lc, nh * dh), lambda b, z, c: (b, z, c, 0)),
                   pl.BlockSpec((1, 1, lc, LANES), lambda b, z, c: (b, z, c, 0))],
        out_shape=[jax.ShapeDtypeStruct((bsz, dil, l, nh * dh), BF16),
                   jax.ShapeDtypeStruct((bsz, dil, l, LANES), F32)],
        compiler_params=_params("parallel", "parallel", "arbitrary"),
        name=f"dsw_attention_d{dil}",
    )(qkv, qkv, qkv, qkv, qkv, bias)


def _dsw_merge_kernel(*refs, dils, tm, nh):
    ng = len(dils)
    o_refs = refs[:ng]
    l_refs = refs[ng:2 * ng]
    out_ref = refs[2 * ng]
    o_scr = refs[2 * ng + 1:2 * ng + 1 + ng]
    l_scr = refs[2 * ng + 1 + ng:]
    for g, dil in enumerate(dils):
        rows = tm // dil
        for z in range(dil):
            rsel = slice(None) if dil == 1 else pl.ds(z, rows, stride=dil)
            l_scr[g][rsel, :] = l_refs[g][0, z]
            for h in range(nh):
                o_scr[g][h, rsel, :] = o_refs[g][0, z, :, h * HEAD_DIM:(h + 1) * HEAD_DIM].astype(F32)
    lses = [l_scr[g][...] for g in range(ng)]
    m = functools.reduce(jnp.maximum, lses)
    ws = [jnp.exp(l - m) for l in lses]
    inv = 1.0 / functools.reduce(lambda a, b: a + b, ws)
    for h in range(nh):
        acc = None
        for g in range(ng):
            alpha = (ws[g] * inv)[:, h:h + 1]
            term = alpha * o_scr[g][h]
            acc = term if acc is None else acc + term
        out_ref[:, h * HEAD_DIM:(h + 1) * HEAD_DIM] = acc.astype(out_ref.dtype)


def _dsw_merge(os_, lses, tm=512):
    bsz, _, t, w = os_[0].shape
    nh = w // HEAD_DIM
    dils = tuple(o.shape[1] for o in os_)
    t = t * dils[0]
    nt = t // tm
    kern = functools.partial(_dsw_merge_kernel, dils=dils, tm=tm, nh=nh)
    o_specs = [pl.BlockSpec((1, d, tm // d, w), lambda b, i: (b, 0, i, 0)) for d in dils]
    l_specs = [pl.BlockSpec((1, d, tm // d, LANES), lambda b, i: (b, 0, i, 0)) for d in dils]
    return pl.pallas_call(
        kern,
        grid=(bsz, nt),
        in_specs=o_specs + l_specs,
        out_specs=pl.BlockSpec((tm, w), lambda b, i: (b * nt + i, 0)),
        out_shape=jax.ShapeDtypeStruct((bsz * t, w), BF16),
        scratch_shapes=([pltpu.VMEM((nh, tm, HEAD_DIM), F32) for _ in dils]
                        + [pltpu.VMEM((tm, LANES), F32) for _ in dils]),
        compiler_params=_params("parallel", "parallel"),
        name="dsw_merge",
    )(*os_, *lses)


def _t5_bucket(dist):
    max_exact = N_BUCKETS // 2
    large = max_exact + (jnp.log(jnp.maximum(dist, 1).astype(F32) / max_exact)
                         / math.log(MAX_DISTANCE / max_exact) * (N_BUCKETS - max_exact)).astype(jnp.int32)
    return jnp.where(dist < max_exact, dist, jnp.minimum(large, N_BUCKETS - 1))


def _dsw_bias(table, window, dil):
    span = window // dil
    qi = jnp.arange(DSW_BLOCK)[:, None]
    kj = jnp.arange(2 * DSW_BLOCK)[None, :]
    delta = qi + DSW_BLOCK - kj
    band = (delta >= 0) & (delta <= span)
    bucket = _t5_bucket(jnp.maximum(delta, 0) * dil)
    onehot = (bucket[None, :, :] == jnp.arange(N_BUCKETS)[:, None, None])
    table_t = table.astype(F32).T
    bias = jnp.sum(jnp.where(onehot[None], table_t[:, :, None, None], 0.0), axis=1)
    return jnp.where(band[None], bias, -jnp.inf)


def _router_kernel(x_ref, g_ref, wcat_ref, b_ref, sel_ref, wts_ref):
    x = x_ref[...]
    tm = x.shape[0]
    ms = jnp.mean(x * x, axis=-1, keepdims=True)
    h = x * lax.rsqrt(ms + RMS_EPS) * g_ref[...]
    hi = h.astype(BF16)
    lo = (h - hi.astype(F32)).astype(BF16)
    hw = jnp.dot(hi, wcat_ref[...], preferred_element_type=F32)
    lw = jnp.dot(lo, wcat_ref[:, 0:LANES], preferred_element_type=F32)
    logits = (hw[:, 0:LANES] + (hw[:, LANES:2 * LANES] + lw)) + b_ref[...]
    lane = lax.broadcasted_iota(jnp.int32, (tm, LANES), 1)
    first_max = lambda v, vmax: jnp.min(jnp.where(v == vmax, lane, LANES), axis=-1, keepdims=True)

    glog = jnp.where(lane < N_GROUPS, logits, -jnp.inf)
    gmax = jnp.max(glog, axis=-1, keepdims=True)
    gsel = first_max(glog, gmax)
    pg = 1.0 / jnp.sum(jnp.exp(glog - gmax), axis=-1, keepdims=True)

    ex = lane - N_GROUPS
    in_group = (ex >= 0) & (ex < N_EXPERTS) & ((ex // EXPERTS_PER_GROUP) == gsel)
    elog = jnp.where(in_group, logits, -jnp.inf)
    v1 = jnp.max(elog, axis=-1, keepdims=True)
    j1 = first_max(elog, v1)
    elog2 = jnp.where(lane == j1, -jnp.inf, elog)
    v2 = jnp.max(elog2, axis=-1, keepdims=True)
    j2 = first_max(elog2, v2)
    e2 = jnp.exp(v2 - v1)
    w1 = pg / (1.0 + e2)
    w2 = pg * e2 / (1.0 + e2)
    sel_ref[...] = jnp.where(lane == 0, j1 - N_GROUPS, jnp.where(lane == 1, j2 - N_GROUPS, 0))
    wts_ref[:, 0:LANES] = jnp.broadcast_to(w1, (tm, LANES))
    wts_ref[:, LANES:2 * LANES] = jnp.broadcast_to(w2, (tm, LANES))


def _router(x, g, w_group, b_group, w_router, b_router, tm=256):
    m, d = x.shape
    pad = LANES - N_GROUPS - N_EXPERTS
    w_rt = jnp.concatenate([w_group, w_router, jnp.zeros((d, pad), F32)], axis=1)
    w_hi = w_rt.astype(BF16)
    w_lo = (w_rt - w_hi.astype(F32)).astype(BF16)
    b_rt = jnp.concatenate([b_group.astype(F32), b_router.astype(F32), jnp.zeros((pad,), F32)])
    return pl.pallas_call(
        _router_kernel,
        grid=(m // tm,),
        in_specs=[pl.BlockSpec((tm, d), lambda i: (i, 0)),
                  pl.BlockSpec((1, d), lambda i: (0, 0)),
                  pl.BlockSpec((d, 2 * LANES), lambda i: (0, 0)),
                  pl.BlockSpec((1, LANES), lambda i: (0, 0))],
        out_specs=[pl.BlockSpec((tm, LANES), lambda i: (i, 0)),
                   pl.BlockSpec((tm, 2 * LANES), lambda i: (i, 0))],
        out_shape=[jax.ShapeDtypeStruct((m, LANES), jnp.int32),
                   jax.ShapeDtypeStruct((m, 2 * LANES), F32)],
        compiler_params=_params("parallel"),
        name="moe_router",
    )(x, g.reshape(1, d), jnp.concatenate([w_hi, w_lo], axis=1), b_rt.reshape(1, LANES))


GATHER_UNROLL = 8


def _issue_row_gather(idx_ref, src_hbm, dst_ref, sem, n_rows):
    def issue(i, carry):
        row = idx_ref[0, 0, i]
        pltpu.make_async_copy(src_hbm.at[pl.ds(row, 1)], dst_ref.at[pl.ds(i, 1)], sem).start()
        return carry
    lax.fori_loop(0, n_rows, issue, 0, unroll=8)


def _wait_row_gather(src_hbm, dst_ref, sem, n_rows):
    pltpu.make_async_copy(src_hbm.at[pl.ds(0, n_rows)], dst_ref, sem).wait()


def _gather_norm_kernel(nv_ref, idx_ref, idx_next_ref, x_hbm, g_ref, o_ref, xs_scr, sems, *, tm, d):
    r = pl.program_id(0)
    nv = nv_ref[0]
    slot = r % 2
    nc = d // LANES

    def issue(idx, s):
        def group(i8, carry):
            for u in range(GATHER_UNROLL):
                i = i8 * GATHER_UNROLL + u
                pltpu.make_async_copy(x_hbm.at[idx[0, 0, i]], xs_scr.at[s, :, i, :],
                                      sems.at[s]).start(priority=u % 2)
            return carry
        lax.fori_loop(0, tm // GATHER_UNROLL, group, 0)

    @pl.when(r == 0)
    def _():
        issue(idx_ref, 0)

    @pl.when(r + 1 < nv)
    def _():
        issue(idx_next_ref, 1 - slot)

    @pl.when(r < nv)
    def _():
        done = xs_scr.at[slot, :, 0:tm, :]
        pltpu.make_async_copy(done, done, sems.at[slot]).wait()
        ssq = jnp.zeros((tm, LANES), F32)
        for c in range(nc):
            xc = xs_scr[slot, c, 0:tm, :]
            ssq = ssq + xc * xc
        inv = lax.rsqrt(jnp.sum(ssq, axis=-1, keepdims=True) / d + RMS_EPS)
        for c in range(nc):
            cs = slice(c * LANES, (c + 1) * LANES)
            o_ref[:, cs] = (xs_scr[slot, c, 0:tm, :] * inv * g_ref[:, cs]).astype(o_ref.dtype)

    @pl.when(r >= nv)
    def _():
        o_ref[...] = jnp.zeros_like(o_ref)


def _gather_norm(x, g, buf_tok, nvalid, tm):
    n, d = x.shape
    nb = buf_tok.shape[0] // tm
    idx = buf_tok.reshape(nb, 1, tm)
    x3 = x.reshape(n, d // LANES, LANES)
    last = lambda r, nv: jnp.minimum(r, nv[0] - 1)
    return pl.pallas_call(
        functools.partial(_gather_norm_kernel, tm=tm, d=d),
        grid_spec=pltpu.PrefetchScalarGridSpec(
            num_scalar_prefetch=1,
            grid=(nb,),
            in_specs=[pl.BlockSpec((1, 1, tm), lambda r, nv: (last(r, nv), 0, 0),
                                   memory_space=pltpu.SMEM),
                      pl.BlockSpec((1, 1, tm), lambda r, nv: (last(r + 1, nv), 0, 0),
                                   memory_space=pltpu.SMEM),
                      pl.BlockSpec(memory_space=pl.ANY),
                      pl.BlockSpec((1, d), lambda r, nv: (0, 0))],
            out_specs=pl.BlockSpec((tm, d), lambda r, nv: (r, 0)),
            scratch_shapes=[pltpu.VMEM((2, d // LANES, tm + SUBLANES, LANES), F32),
                            pltpu.SemaphoreType.DMA((2,))],
        ),
        out_shape=jax.ShapeDtypeStruct((nb * tm, d), BF16),
        compiler_params=_params("arbitrary"),
        name="moe_gather_norm",
    )(nvalid, idx, idx, x3, g.reshape(1, d))


def _combine_kernel(idx_ref, idx_next_ref, x_ref, w_ref, y_hbm, *rest, tm, with_norm):
    if with_norm:
        g_ref, o_ref, h_ref, ys_scr, sems = rest
    else:
        o_ref, ys_scr, sems = rest
    r = pl.program_id(0)
    nr = pl.num_programs(0)
    slot = r % 2
    n_rows = TOP_K * tm

    @pl.when(r == 0)
    def _():
        _issue_row_gather(idx_ref, y_hbm, ys_scr.at[0], sems.at[0], n_rows)

    @pl.when(r + 1 < nr)
    def _():
        _issue_row_gather(idx_next_ref, y_hbm, ys_scr.at[1 - slot], sems.at[1 - slot], n_rows)

    _wait_row_gather(y_hbm, ys_scr.at[slot], sems.at[slot], n_rows)
    w0 = w_ref[:, 0:LANES]
    w1 = w_ref[:, LANES:2 * LANES]
    d = x_ref.shape[1]
    ssq = jnp.zeros((tm, LANES), F32)
    for c in range(d // LANES):
        cs = slice(c * LANES, (c + 1) * LANES)
        o = x_ref[:, cs] + (w0 * ys_scr[slot, 0:tm, cs] + w1 * ys_scr[slot, tm:2 * tm, cs])
        o_ref[:, cs] = o
        if with_norm:
            ssq = ssq + o * o
    if with_norm:
        inv = lax.rsqrt(jnp.sum(ssq, axis=-1, keepdims=True) / d + RMS_EPS)
        h_ref[...] = (o_ref[...] * inv * g_ref[...]).astype(h_ref.dtype)


def _combine(x, w_exp, yb, dest, next_norm_g, tm=128):
    n, d = x.shape
    nr = n // tm
    with_norm = next_norm_g is not None
    in_specs = [pl.BlockSpec((1, 1, TOP_K * tm), lambda r: (r, 0, 0), memory_space=pltpu.SMEM),
                pl.BlockSpec((1, 1, TOP_K * tm), lambda r: (jnp.minimum(r + 1, nr - 1), 0, 0),
                             memory_space=pltpu.SMEM),
                pl.BlockSpec((tm, d), lambda r: (r, 0)),
                pl.BlockSpec((tm, 2 * LANES), lambda r: (r, 0)),
                pl.BlockSpec(memory_space=pl.ANY)]
    args = [dest, dest, x, w_exp, yb]
    out_specs = [pl.BlockSpec((tm, d), lambda r: (r, 0))]
    out_shape = [jax.ShapeDtypeStruct((n, d), F32)]
    if with_norm:
        in_specs.append(pl.BlockSpec((1, d), lambda r: (0, 0)))
        args.append(next_norm_g.reshape(1, d))
        out_specs.append(pl.BlockSpec((tm, d), lambda r: (r, 0)))
        out_shape.append(jax.ShapeDtypeStruct((n, d), BF16))
    res = pl.pallas_call(
        functools.partial(_combine_kernel, tm=tm, with_norm=with_norm),
        grid=(nr,),
        in_specs=in_specs,
        out_specs=out_specs,
        out_shape=out_shape,
        scratch_shapes=[pltpu.VMEM((2, TOP_K * tm, d), F32), pltpu.SemaphoreType.DMA((2,))],
        compiler_params=_params("arbitrary"),
        name="moe_combine",
    )(*args)
    return (res[0], res[1]) if with_norm else (res[0], None)


def _ffn_up_kernel(be_ref, nv_ref, first_ref, run_ref, nxt_ref, nruns_ref, x_ref, wg_hbm, wu_hbm,
                   o_ref, stage, w_scr, sems, *, li, tf, nf):
    j = pl.program_id(0)
    r = pl.program_id(1)
    valid = r < nv_ref[0]
    cols = lambda jj: pl.ds(pl.multiple_of(jj * tf, LANES), tf)

    def copies(e, jj, s):
        return (pltpu.make_async_copy(wg_hbm.at[li, e, :, cols(jj)], stage.at[s, 0], sems.at[s]),
                pltpu.make_async_copy(wu_hbm.at[li, e, :, cols(jj)], stage.at[s, 1], sems.at[s]))

    @pl.when((j == 0) & (r == 0))
    def _():
        for cp in copies(be_ref[0], 0, 0):
            cp.start()

    @pl.when(valid & (first_ref[r] == 1))
    def _():
        s = (j * nruns_ref[0] + run_ref[r]) % 2
        nxt = nxt_ref[r]

        @pl.when(nxt >= 0)
        def _():
            for cp in copies(nxt, j, 1 - s):
                cp.start()

        @pl.when((nxt < 0) & (j + 1 < nf))
        def _():
            for cp in copies(be_ref[0], j + 1, 1 - s):
                cp.start()

        for cp in copies(be_ref[r], j, s):
            cp.wait()
        w_scr[:, 0:tf] = stage[s, 0].astype(BF16)
        w_scr[:, tf:2 * tf] = stage[s, 1].astype(BF16)

    @pl.when(valid)
    def _():
        gu = jnp.dot(x_ref[...], w_scr[...], preferred_element_type=F32)
        o_ref[...] = (jax.nn.silu(gu[:, 0:tf]) * gu[:, tf:2 * tf]).astype(o_ref.dtype)

    @pl.when(jnp.logical_not(valid))
    def _():
        o_ref[...] = jnp.zeros_like(o_ref)


def _ffn_down_kernel(be_ref, nv_ref, first_ref, run_ref, nxt_ref, a_ref, wd_hbm, o_ref,
                     stage, w_scr, sems, *, li):
    r = pl.program_id(0)
    valid = r < nv_ref[0]
    copy = lambda e, s: pltpu.make_async_copy(wd_hbm.at[li, e], stage.at[s], sems.at[s])

    @pl.when(r == 0)
    def _():
        copy(be_ref[0], 0).start()

    @pl.when(valid & (first_ref[r] == 1))
    def _():
        s = run_ref[r] % 2
        nxt = nxt_ref[r]

        @pl.when(nxt >= 0)
        def _():
            copy(nxt, 1 - s).start()

        copy(be_ref[r], s).wait()
        w_scr[...] = stage[s].astype(BF16)

    @pl.when(valid)
    def _():
        o_ref[...] = jnp.dot(a_ref[...], w_scr[...], preferred_element_type=F32)

    @pl.when(jnp.logical_not(valid))
    def _():
        o_ref[...] = jnp.zeros_like(o_ref)


def _expert_ffn(xg, blk_e, nvalid, w_gate, w_up, w_down, li, tm, tf=384):
    p, d = xg.shape
    f = w_gate.shape[3]
    nb = p // tm
    nf = f // tf
    tiles = jnp.arange(nb, dtype=jnp.int32)
    valid = tiles < nvalid[0]
    first = valid & ((tiles == 0) | (blk_e != jnp.roll(blk_e, 1)))
    run = (jnp.cumsum(first.astype(jnp.int32)) - 1).astype(jnp.int32)
    nruns = jnp.sum(first.astype(jnp.int32)).reshape(1)
    run_e = jnp.sum(jnp.where(first[None, :] & (run[None, :] == tiles[:, None]), blk_e[None, :], 0),
                    axis=1)
    nxt = jnp.sum(jnp.where(tiles[None, :] == run[:, None] + 1, run_e[None, :], 0), axis=1)
    nxt = jnp.where(run + 1 < nruns[0], nxt, -1).astype(jnp.int32)
    first = first.astype(jnp.int32)
    last = lambda r, nv: jnp.minimum(r, nv[0] - 1)
    act = pl.pallas_call(
        functools.partial(_ffn_up_kernel, li=li, tf=tf, nf=nf),
        grid_spec=pltpu.PrefetchScalarGridSpec(
            num_scalar_prefetch=6,
            grid=(nf, nb),
            in_specs=[pl.BlockSpec((tm, d), lambda j, r, be, nv, *_: (last(r, nv), 0)),
                      pl.BlockSpec(memory_space=pl.ANY),
                      pl.BlockSpec(memory_space=pl.ANY)],
            out_specs=pl.BlockSpec((tm, tf), lambda j, r, *_: (r, j)),
            scratch_shapes=[pltpu.VMEM((2, 2, d, tf), F32), pltpu.VMEM((d, 2 * tf), BF16),
                            pltpu.SemaphoreType.DMA((2,))],
        ),
        out_shape=jax.ShapeDtypeStruct((p, f), BF16),
        compiler_params=_params("arbitrary", "arbitrary"),
        name="moe_ffn_up",
    )(blk_e, nvalid, first, run, nxt, nruns, xg, w_gate, w_up)
    return pl.pallas_call(
        functools.partial(_ffn_down_kernel, li=li),
        grid_spec=pltpu.PrefetchScalarGridSpec(
            num_scalar_prefetch=5,
            grid=(nb,),
            in_specs=[pl.BlockSpec((tm, f), lambda r, be, nv, *_: (last(r, nv), 0)),
                      pl.BlockSpec(memory_space=pl.ANY)],
            out_specs=pl.BlockSpec((tm, d), lambda r, *_: (r, 0)),
            scratch_shapes=[pltpu.VMEM((2, f, d), F32), pltpu.VMEM((f, d), BF16),
                            pltpu.SemaphoreType.DMA((2,))],
        ),
        out_shape=jax.ShapeDtypeStruct((p, d), F32),
        compiler_params=_params("arbitrary"),
        name="moe_ffn_down",
    )(blk_e, nvalid, first, run, nxt, act, w_down)


def _hier_moe(x, norm_g, w_group, b_group, w_router, b_router, w_gate, w_up, w_down, li,
              next_norm_g, tm=MOE_TM, tc=128):
    n_tok, d = x.shape
    sel, w_exp = _router(x, norm_g, w_group, b_group, w_router, b_router)

    n_as = n_tok * TOP_K
    nb = -(-n_as // tm) + N_EXPERTS
    p_rows = nb * tm
    n_pad = p_rows - n_as
    flat_e = sel[:, :TOP_K].reshape(n_as)
    experts = jnp.arange(N_EXPERTS, dtype=jnp.int32)
    counts = jnp.sum((flat_e[:, None] == experts[None, :]).astype(jnp.int32), axis=0)
    pcounts = (counts + tm - 1) // tm * tm
    nvalid = (jnp.sum(pcounts) // tm).astype(jnp.int32).reshape(1)
    pad_end = jnp.cumsum(pcounts - counts)
    pad_e = jnp.sum((jnp.arange(n_pad, dtype=jnp.int32)[:, None] >= pad_end[None, :]).astype(jnp.int32),
                    axis=1)
    keys = jnp.concatenate([2 * flat_e, 2 * pad_e + 1])
    vals = jnp.concatenate([jnp.arange(n_as, dtype=jnp.int32), jnp.full((n_pad,), -1, jnp.int32)])
    skeys, svals = lax.sort((keys, vals), num_keys=1)
    buf_tok = jnp.where(svals >= 0, svals // TOP_K, jnp.arange(p_rows, dtype=jnp.int32) % n_tok)
    _, pos = lax.sort((svals, jnp.arange(p_rows, dtype=jnp.int32)), num_keys=1)
    dest = pos[n_pad:].reshape(n_tok, TOP_K)
    blk_e = jnp.minimum(skeys.reshape(nb, tm)[:, 0] // 2, N_EXPERTS - 1).astype(jnp.int32)

    xg = _gather_norm(x, norm_g, buf_tok, nvalid, tm)
    yb = _expert_ffn(xg, blk_e, nvalid, w_gate, w_up, w_down, li, tm)
    dest_t = dest.reshape(n_tok // tc, tc, TOP_K).transpose(0, 2, 1).reshape(n_tok // tc, 1, TOP_K * tc)
    return _combine(x, w_exp, yb, dest_t, next_norm_g, tm=tc)


def _ab_layer(x, h, w_in, li, conv_w, conv_b, wa, ba, wi, bi, lam, bf, qn, kn, w_out, bsz):
    n_tok, d = x.shape
    t = n_tok // bsz
    d_rnn = conv_w.shape[1]
    nh = bf.shape[0]
    fw = nh * HEAD_DIM
    zga = _proj_f32(h, w_in, li, 0, 2 * d_rnn, wt=True).reshape(bsz, t, 2 * d_rnn)
    ya = _rglru(zga, bsz, conv_w, conv_b, wa, ba, wi, bi, lam)
    normw = jnp.concatenate([jnp.tile(qn, nh), jnp.tile(kn * FOX_QK_SCALE, nh),
                             jnp.ones((fw,), F32)]).reshape(1, 3 * fw)
    qkv = _proj_heads(h, w_in, li, 2 * d_rnn, 3 * fw, normw, 2 * fw, bsz, 1, wt=True)
    b_f = jnp.pad(bf.astype(F32), (0, LANES - nh)).reshape(1, LANES)
    cum = _forget_cum(h, w_in, li, 2 * d_rnn + 3 * fw, nh, b_f, bsz, wt=True)
    fox_tq = 512
    cum_row = cum[:, :, :nh].transpose(0, 2, 1).reshape(bsz, nh, t // fox_tq, fox_tq)
    yb = _fox_attention(qkv, cum, cum_row, tq=fox_tq)
    return _proj_res([ya.reshape(n_tok, d_rnn), yb.reshape(n_tok, fw)], w_out, li, x)


def _dilated_layer(x, h, w_in, li, qn, kn, table, w_out, bsz):
    n_tok, d = x.shape
    t = n_tok // bsz
    nh = table.shape[1] // len(DSW_PATTERNS)
    fw = nh * HEAD_DIM
    os_, lses = [], []
    for g, (window, dil) in enumerate(DSW_PATTERNS):
        normw = jnp.concatenate([jnp.tile(qn[g], nh), jnp.tile(kn[g], nh),
                                 jnp.ones((fw,), F32)]).reshape(1, 3 * fw)
        qkv = _proj_heads(h, w_in, li, g * 3 * fw, 3 * fw, normw, 2 * fw, bsz, dil)
        bias = _dsw_bias(table[:, g * nh:(g + 1) * nh], window, dil)
        o, lse = _dsw_group(qkv, bias, min(t // dil, 1024))
        os_.append(o)
        lses.append(lse)
    y = _dsw_merge(os_, lses)
    return _proj_res([y], w_out, li, x)


def kernel(x, mix_norm, ffn_norm, w_in_ab, conv_w, conv_b, lru_wa, lru_ba, lru_wi, lru_bi,
           lru_lambda, fox_bf, fox_qnorm, fox_knorm, w_out_ab, w_in_c, dsw_qnorm, dsw_knorm,
           w_out_c, rel_bias, moe_w_group, moe_b_group, moe_w_router, moe_b_router,
           moe_w_gate, moe_w_up, moe_w_down):
    bsz, t, d = x.shape
    depth = mix_norm.shape[0]
    xf = x.reshape(bsz * t, d)
    h = _rmsnorm(xf, mix_norm[0])
    for layer in range(depth):
        i = layer // 2
        if layer % 2 == 0:
            xf = _ab_layer(xf, h, jnp.swapaxes(w_in_ab, 1, 2), i, conv_w[i], conv_b[i], lru_wa[i],
                           lru_ba[i], lru_wi[i], lru_bi[i], lru_lambda[i], fox_bf[i],
                           fox_qnorm[i], fox_knorm[i], w_out_ab, bsz)
        else:
            xf = _dilated_layer(xf, h, w_in_c, i, dsw_qnorm[i], dsw_knorm[i],
                                rel_bias, w_out_c, bsz)
        next_g = mix_norm[layer + 1] if layer + 1 < depth else None
        xf, h = _hier_moe(xf, ffn_norm[layer], moe_w_group[layer], moe_b_group[layer],
                          moe_w_router[layer], moe_b_router[layer], moe_w_gate, moe_w_up,
                          moe_w_down, layer, next_g)
    return xf.reshape(bsz, t, d)
```
